```python
import jax, jax.numpy as jnp
from jax import lax
import numpy as np

D_MODEL = 1024
BATCH = 4
SEQ = 8192
DEPTH = 1
DEC_BATCH = 16
DEC_SEQ = 16
PAST_LEN = 2048

CHUNK = 64
D_MIX = D_MODEL
GDN_HEADS = 8
GDN_DK = 64
GDN_DV = 64
GDN_QK = GDN_HEADS * GDN_DK
GDN_WIDTH = GDN_HEADS * GDN_DV
GDN_CONV_CH = 2 * GDN_QK + GDN_WIDTH
CONV_W = 4
HG_HEADS = 4
HG_DK = 128
HG_DV = 128
HG_QK = HG_HEADS * HG_DK
HG_WIDTH = HG_HEADS * HG_DV
D_IN = GDN_CONV_CH + GDN_WIDTH + 2 * GDN_HEADS + 2 * HG_QK + 2 * HG_WIDTH
EPS = 1e-6

kernel_name = "hybrid_gdn_hgrn2_stream_step"


def _rmsnorm(x, w):
    xf = x.astype(jnp.float32)
    return xf * lax.rsqrt(jnp.mean(xf * xf, axis=-1, keepdims=True) + EPS) * w.astype(jnp.float32)


def _l2norm(x):
    return x * lax.rsqrt(jnp.sum(x * x, axis=-1, keepdims=True) + EPS)


def _to_chunks(t, c):
    B, L, H, d = t.shape
    return t.reshape(B, L // c, c, H, d).transpose(1, 0, 3, 2, 4)


def _from_chunks(t):
    n, B, H, c, d = t.shape
    return t.transpose(1, 0, 3, 2, 4).reshape(B, n * c, H, d)


def _causal_conv(u, buf, w):
    full = jnp.concatenate([buf.astype(jnp.float32), u], axis=1)
    L = u.shape[1]
    wf = w.astype(jnp.float32)
    out = full[:, 0:L] * wf[0]
    for j in range(1, CONV_W):
        out = out + full[:, j:j + L] * wf[j]
    return jax.nn.silu(out), full[:, full.shape[1] - (CONV_W - 1):]


def _gated_delta(q, k, v, g, beta, S0, c):
    dv = v.shape[-1]
    qc, kc, vc = _to_chunks(q, c), _to_chunks(k, c), _to_chunks(v, c)
    gc = _to_chunks(g[..., None], c)[..., 0]
    bc = _to_chunks(beta[..., None], c)[..., 0]
    idx = jnp.arange(c)
    causal = idx[:, None] >= idx[None, :]
    strict = idx[:, None] > idx[None, :]
    eye = jnp.eye(c, dtype=jnp.float32)

    def step(S, inp):
        qi, ki, vi, gi, bi = inp
        gcum = jnp.cumsum(gi, axis=-1)
        diff = gcum[..., :, None] - gcum[..., None, :]
        decay = jnp.where(causal, jnp.exp(jnp.where(causal, diff, 0.0)), 0.0)
        kb = ki * bi[..., None]
        lower = jnp.where(strict, jnp.einsum("bhid,bhjd->bhij", kb, ki) * decay, 0.0)
        rhs = jnp.concatenate([vi * bi[..., None], kb * jnp.exp(gcum)[..., None]], axis=-1)
        a = jnp.broadcast_to(eye, lower.shape) + lower
        sol = lax.linalg.triangular_solve(a, rhs, left_side=True, lower=True, unit_diagonal=True)
        u, w = sol[..., :dv], sol[..., dv:]
        v_new = u - jnp.einsum("bhid,bhde->bhie", w, S)
        attn = jnp.einsum("bhid,bhjd->bhij", qi, ki) * decay
        o = (jnp.einsum("bhid,bhde->bhie", qi * jnp.exp(gcum)[..., None], S)
             + jnp.einsum("bhij,bhje->bhie", attn, v_new))
        glast = gcum[..., -1]
        S_new = (S * jnp.exp(glast)[..., None, None]
                 + jnp.einsum("bhid,bhie->bhde", ki * jnp.exp(glast[..., None] - gcum)[..., None], v_new))
        return S_new, o

    S_fin, o = lax.scan(step, S0.astype(jnp.float32), (qc, kc, vc, gc, bc))
    return _from_chunks(o), S_fin


def _hgrn2(q, k, v, logf, S0, c):
    qc, kc, vc, fc = _to_chunks(q, c), _to_chunks(k, c), _to_chunks(v, c), _to_chunks(logf, c)
    idx = jnp.arange(c)
    causal = (idx[:, None] >= idx[None, :])[..., None]

    def step(S, inp):
        qi, ki, vi, lfi = inp
        b = jnp.cumsum(lfi, axis=2)
        diff = b[:, :, :, None, :] - b[:, :, None, :, :]
        dec = jnp.where(causal, jnp.exp(jnp.where(causal, diff, 0.0)), 0.0)
        attn = jnp.einsum("bhid,bhijd,bhjd->bhij", qi, dec, ki)
        o = (jnp.einsum("bhid,bhde->bhie", qi * jnp.exp(b), S)
             + jnp.einsum("bhij,bhje->bhie", attn, vi))
        blast = b[:, :, -1]
        S_new = (jnp.exp(blast)[..., None] * S
                 + jnp.einsum("bhid,bhie->bhde", ki * jnp.exp(blast[:, :, None] - b), vi))
        return S_new, o

    S_fin, o = lax.scan(step, S0.astype(jnp.float32), (qc, kc, vc, fc))
    return _from_chunks(o), S_fin


def _layer(x, conv_buf, S_gdn, S_hg, c, lb, norm_w, w_in, conv_w, A_log, dt_bias, gdn_norm_w, hg_norm_w, w_out):
    B, L, _ = x.shape
    h = _rmsnorm(x, norm_w)
    proj = h @ w_in.astype(jnp.float32)
    sizes = [GDN_CONV_CH, GDN_WIDTH, GDN_HEADS, GDN_HEADS, HG_QK, HG_QK, HG_WIDTH, HG_WIDTH]
    offs = np.cumsum(sizes)[:-1].tolist()
    qkv, z_a, b_a, a_a, hq, hf, hi, z_b = jnp.split(proj, offs, axis=-1)
    qkv_c, new_buf = _causal_conv(qkv, conv_buf, conv_w)
    q_a, k_a, v_a = jnp.split(qkv_c, [GDN_QK, 2 * GDN_QK], axis=-1)
    q_a = _l2norm(q_a.reshape(B, L, GDN_HEADS, GDN_DK)) * (GDN_DK ** -0.5)
    k_a = _l2norm(k_a.reshape(B, L, GDN_HEADS, GDN_DK))
    v_a = v_a.reshape(B, L, GDN_HEADS, GDN_DV)
    beta = jax.nn.sigmoid(b_a)
    g = -jnp.exp(A_log.astype(jnp.float32)) * jax.nn.softplus(a_a + dt_bias.astype(jnp.float32))
    o_a, S_gdn_new = _gated_delta(q_a, k_a, v_a, g, beta, S_gdn, c)
    o_a = _rmsnorm(o_a, gdn_norm_w) * jax.nn.silu(z_a.reshape(B, L, GDN_HEADS, GDN_DV))
    f = lb + (1.0 - lb) * jax.nn.sigmoid(hf)
    logf = jnp.log(f).reshape(B, L, HG_HEADS, HG_DK)
    k_b = (1.0 - f).reshape(B, L, HG_HEADS, HG_DK)
    q_b = jax.nn.silu(hq).reshape(B, L, HG_HEADS, HG_DK)
    v_b = hi.reshape(B, L, HG_HEADS, HG_DV)
    o_b, S_hg_new = _hgrn2(q_b, k_b, v_b, logf, S_hg, c)
    o_b = _rmsnorm(o_b, hg_norm_w) * jax.nn.silu(z_b.reshape(B, L, HG_HEADS, HG_DV))
    o = jnp.concatenate([o_a.reshape(B, L, GDN_WIDTH), o_b.reshape(B, L, HG_WIDTH)], axis=-1)
    y = x.astype(jnp.float32) + o @ w_out.astype(jnp.float32)
    return y, new_buf, S_gdn_new, S_hg_new


def setup_inputs(seed: int = 0) -> dict:
    key = jax.random.key(seed)
    ks = jax.random.split(key, 16)
    f32 = jnp.float32
    x_prompt = jax.random.normal(ks[0], (BATCH, SEQ, D_MODEL), f32)
    x_sample = jax.random.normal(ks[1], (DEC_BATCH, DEC_SEQ, D_MODEL), f32)
    state_conv = jax.random.normal(ks[2], (DEPTH, DEC_BATCH, CONV_W - 1, GDN_CONV_CH), f32)
    state_gdn = 0.5 * jax.random.normal(ks[3], (DEPTH, DEC_BATCH, GDN_HEADS, GDN_DK, GDN_DV), f32)
    state_hgrn = jax.random.normal(ks[4], (DEPTH, DEC_BATCH, HG_HEADS, HG_DK, HG_DV), f32)
    norm_w = 1.0 + 0.02 * jax.random.normal(ks[5], (DEPTH, D_MODEL), f32)
    w_in = jax.random.normal(ks[6], (DEPTH, D_MODEL, D_IN), f32) * D_MODEL ** -0.5
    conv_w = jax.random.normal(ks[7], (DEPTH, CONV_W, GDN_CONV_CH), f32) * CONV_W ** -0.5
    gdn_A_log = jnp.log(jax.random.uniform(ks[8], (DEPTH, GDN_HEADS), f32, 1.0, 16.0))
    dt = jnp.exp(jax.random.uniform(ks[9], (DEPTH, GDN_HEADS), f32, float(np.log(1e-3)), float(np.log(1e-1))))
    gdn_dt_bias = dt + jnp.log(-jnp.expm1(-dt))
    gdn_norm_w = 1.0 + 0.02 * jax.random.normal(ks[10], (DEPTH, GDN_DV), f32)
    hgrn_lb_logits = 0.1 * jax.random.normal(ks[11], (DEPTH + 1, HG_QK), f32)
    hgrn_norm_w = 1.0 + 0.02 * jax.random.normal(ks[12], (DEPTH, HG_DV), f32)
    w_out = jax.random.normal(ks[13], (DEPTH, D_MIX, D_MODEL), f32) * D_MIX ** -0.5
    final_norm_w = 1.0 + 0.02 * jax.random.normal(ks[14], (D_MODEL,), f32)
    return {"x_prompt": x_prompt, "x_sample": x_sample, "state_conv": state_conv,
            "state_gdn": state_gdn, "state_hgrn": state_hgrn, "norm_w": norm_w, "w_in": w_in,
            "conv_w": conv_w, "gdn_A_log": gdn_A_log, "gdn_dt_bias": gdn_dt_bias,
            "gdn_norm_w": gdn_norm_w, "hgrn_lb_logits": hgrn_lb_logits, "hgrn_norm_w": hgrn_norm_w,
            "w_out": w_out, "final_norm_w": final_norm_w}


def reference(x_prompt, x_sample, state_conv, state_gdn, state_hgrn, norm_w, w_in, conv_w, gdn_A_log,
              gdn_dt_bias, gdn_norm_w, hgrn_lb_logits, hgrn_norm_w, w_out, final_norm_w):
    f32 = jnp.float32
    lb_all = jnp.cumsum(jax.nn.softmax(hgrn_lb_logits.astype(f32), axis=0), axis=0)
    Bp, Ls = x_prompt.shape[0], x_sample.shape[1]
    hp = x_prompt.astype(f32)
    hs = x_sample.astype(f32)
    pc, pg, ph, sc, sg, sh = [], [], [], [], [], []
    for l in range(DEPTH):
        params = (lb_all[l], norm_w[l], w_in[l], conv_w[l], gdn_A_log[l], gdn_dt_bias[l],
                  gdn_norm_w[l], hgrn_norm_w[l], w_out[l])
        hp, c1, g1, r1 = _layer(hp, jnp.zeros((Bp, CONV_W - 1, GDN_CONV_CH), f32),
                                jnp.zeros((Bp, GDN_HEADS, GDN_DK, GDN_DV), f32),
                                jnp.zeros((Bp, HG_HEADS, HG_DK, HG_DV), f32), CHUNK, *params)
        hs, c2, g2, r2 = _layer(hs, state_conv[l], state_gdn[l], state_hgrn[l], Ls, *params)
        pc.append(c1); pg.append(g1); ph.append(r1)
        sc.append(c2); sg.append(g2); sh.append(r2)
    y_prompt = _rmsnorm(hp, final_norm_w).astype(x_prompt.dtype)
    y_sample = _rmsnorm(hs, final_norm_w).astype(x_sample.dtype)
    new_conv_prompt = jnp.stack(pc).astype(x_prompt.dtype)
    new_gdn_prompt = jnp.stack(pg).astype(x_prompt.dtype)
    new_hgrn_prompt = jnp.stack(ph).astype(x_prompt.dtype)
    new_conv_sample = jnp.stack(sc).astype(state_conv.dtype)
    new_gdn_sample = jnp.stack(sg).astype(state_gdn.dtype)
    new_hgrn_sample = jnp.stack(sh).astype(state_hgrn.dtype)
    return (y_prompt, y_sample, new_conv_prompt, new_gdn_prompt, new_hgrn_prompt,
            new_conv_sample, new_gdn_sample, new_hgrn_sample)
```

```python
import functools

import numpy as np
import jax
import jax.numpy as jnp
from jax import lax
from jax.experimental import pallas as pl
from jax.experimental.pallas import tpu as pltpu

F32 = jnp.float32
BF16 = jnp.bfloat16

D_MODEL = 1024
CHUNK = 64
GDN_HEADS = 8
GDN_DK = 64
GDN_QK = GDN_HEADS * GDN_DK
GDN_CONV_CH = 3 * GDN_QK
CONV_W = 4
HG_HEADS = 4
HG_DK = 128
HG_QK = HG_HEADS * HG_DK
EPS = 1e-6

LANES = 128
SUBLANES = 8
SLAB = 512
GATE_COL = GDN_CONV_CH + GDN_QK + 4 * HG_QK
N_PROJ = GATE_COL + LANES
VMEM_LIMIT = 48 * 1024 * 1024


def _sigmoid(x):
    return 1.0 / (1.0 + jnp.exp(-x))


def _split3(x):
    hi = x.astype(BF16)
    r = x - hi.astype(F32)
    mid = r.astype(BF16)
    lo = (r - mid.astype(F32)).astype(BF16)
    return hi, mid, lo


def _dot(a, b):
    return jnp.dot(a, b, preferred_element_type=F32)


def _dot_nt(a, b):
    return lax.dot_general(a, b, (((1,), (1,)), ((), ())), preferred_element_type=F32)


def _dot_tn(a, b):
    return lax.dot_general(a, b, (((0,), (0,)), ((), ())), preferred_element_type=F32)


def _sel_left(sel01, x):
    hi, mid, lo = _split3(x)
    return _dot(sel01, hi) + _dot(sel01, mid) + _dot(sel01, lo)


def _sel_right(x, sel01):
    hi, mid, lo = _split3(x)
    return _dot(hi, sel01) + _dot(mid, sel01) + _dot(lo, sel01)


def _inproj_kernel(x_ref, nw_ref, w_ref, cw_ref, cs_ref, proj_ref, nc_ref, ubuf, *, n_valid):
    t = pl.program_id(1)
    T = x_ref.shape[1]

    @pl.when(t == 0)
    def _():
        ubuf[0:SUBLANES, :] = cs_ref[0]

    x = x_ref[0]
    h = x * lax.rsqrt(jnp.mean(x * x, axis=-1, keepdims=True) + EPS) * nw_ref[...]
    hb = h.astype(BF16)

    for s in range(0, GDN_CONV_CH, SLAB):
        ubuf[SUBLANES:SUBLANES + T, s:s + SLAB] = _dot(hb, w_ref[:, s:s + SLAB])
    for s in range(GDN_CONV_CH, GATE_COL, SLAB):
        proj_ref[0, :, s:s + SLAB] = _dot(hb, w_ref[:, s:s + SLAB])
    proj_ref[0, :, GATE_COL:N_PROJ] = _dot(hb, w_ref[:, GATE_COL:N_PROJ])

    for s in range(0, GDN_CONV_CH, SLAB):
        y = cw_ref[CONV_W - 1:CONV_W, s:s + SLAB] * ubuf[SUBLANES:SUBLANES + T, s:s + SLAB]
        for j in range(1, CONV_W):
            y = y + (cw_ref[CONV_W - 1 - j:CONV_W - j, s:s + SLAB]
                     * ubuf[SUBLANES - j:SUBLANES - j + T, s:s + SLAB])
        proj_ref[0, :, s:s + SLAB] = y * _sigmoid(y)

    nc_ref[0] = ubuf[n_valid:n_valid + SUBLANES, :]
    ubuf[0:SUBLANES, :] = ubuf[T:T + SUBLANES, :]


def _inproj(x, norm_w, w_in_r, conv_w, conv_state8, *, tile, n_valid):
    B, L, _ = x.shape
    kern = functools.partial(_inproj_kernel, n_valid=n_valid)
    return pl.pallas_call(
        kern,
        grid=(B, L // tile),
        in_specs=[
            pl.BlockSpec((1, tile, D_MODEL), lambda b, t: (b, t, 0)),
            pl.BlockSpec((1, D_MODEL), lambda b, t: (0, 0)),
            pl.BlockSpec((D_MODEL, N_PROJ), lambda b, t: (0, 0)),
            pl.BlockSpec((CONV_W, GDN_CONV_CH), lambda b, t: (0, 0)),
            pl.BlockSpec((1, SUBLANES, GDN_CONV_CH), lambda b, t: (b, 0, 0)),
        ],
        out_specs=[
            pl.BlockSpec((1, tile, N_PROJ), lambda b, t: (b, t, 0)),
            pl.BlockSpec((1, SUBLANES, GDN_CONV_CH), lambda b, t: (b, 0, 0)),
        ],
        out_shape=[
            jax.ShapeDtypeStruct((B, L, N_PROJ), F32),
            jax.ShapeDtypeStruct((B, SUBLANES, GDN_CONV_CH), F32),
        ],
        scratch_shapes=[pltpu.VMEM((tile + SUBLANES, GDN_CONV_CH), F32)],
        compiler_params=pltpu.CompilerParams(
            dimension_semantics=("arbitrary", "arbitrary"), vmem_limit_bytes=VMEM_LIMIT),
        name="inproj_conv",
    )(x, norm_w, w_in_r, conv_w, conv_state8)


def _gdn_consts():
    ltri = np.tril(np.ones((CHUNK, CHUNK), np.float32))
    lane = np.arange(GDN_QK)
    eb = np.zeros((LANES, GDN_QK), np.float32)
    eb[lane // GDN_DK, lane] = 1.0
    eg = np.zeros((LANES, GDN_QK + GDN_HEADS * LANES), np.float32)
    eg[GDN_HEADS + lane // GDN_DK, lane] = 1.0
    lane2 = np.arange(GDN_HEADS * LANES)
    eg[GDN_HEADS + lane2 // LANES, GDN_QK + lane2] = 1.0
    bd = np.kron(np.eye(LANES // GDN_DK, dtype=np.float32), np.ones((GDN_DK, GDN_DK), np.float32))
    return (jnp.asarray(ltri, BF16), jnp.asarray(eb, BF16), jnp.asarray(eg, BF16), jnp.asarray(bd, BF16))


def _seg_sum64(x, bd):
    hi = x.astype(BF16)
    lo = (x - hi.astype(F32)).astype(BF16)
    return _dot(hi, bd) + _dot(lo, bd)


def _gdn_kernel(q_ref, k_ref, v_ref, z_ref, g_ref, s0_ref, alog_ref, dtb_ref, nw_ref,
                ltri_ref, eb_ref, eg_ref, bd_ref, o_ref, s_ref, *, n_chunks, n_valid):
    c = CHUNK
    t = pl.program_id(1)

    @pl.when(t == 0)
    def _():
        s_ref[...] = s0_ref[...]

    row = lax.broadcasted_iota(jnp.int32, (c, LANES), 0)
    lane = lax.broadcasted_iota(jnp.int32, (c, LANES), 1)
    eye = lane == row
    causal = lane <= row
    strict = (lane < row)[:, :c]
    eye_c = eye[:, :c].astype(F32)
    left = lane < GDN_DK
    right = lane >= GDN_DK
    left2 = (lax.broadcasted_iota(jnp.int32, (c, 2 * LANES), 1) & GDN_DK) == 0
    bd_mask = ((lax.broadcasted_iota(jnp.int32, (LANES, LANES), 0) & GDN_DK)
               == (lax.broadcasted_iota(jnp.int32, (LANES, LANES), 1) & GDN_DK))
    neg_a = -jnp.exp(alog_ref[...])
    ltri = ltri_ref[...]
    bd = bd_ref[...]

    for ci in range(n_chunks):
        r0 = ci * c
        gz = g_ref[0, r0:r0 + c, :]
        xg = gz + dtb_ref[...]
        softplus = jnp.maximum(xg, 0.0) + jnp.log(1.0 + jnp.exp(-jnp.abs(xg)))
        gb = jnp.where(lane < GDN_HEADS, _sigmoid(gz),
                       jnp.where(lane < 2 * GDN_HEADS, neg_a * softplus, 0.0))
        if n_valid is not None:
            gb = jnp.where(row + r0 < n_valid, gb, 0.0)
        gcum = _sel_left(ltri, gb)
        bx = _sel_right(gb, eb_ref[...])
        gg = _sel_right(gcum, eg_ref[...])

        for p in range(GDN_HEADS // 2):
            ls = slice(p * LANES, (p + 1) * LANES)
            qp = q_ref[0, r0:r0 + c, ls]
            kp = k_ref[0, r0:r0 + c, ls]
            vp = v_ref[0, r0:r0 + c, ls]
            qn = qp * lax.rsqrt(_seg_sum64(qp * qp, bd) + EPS) * (GDN_DK ** -0.5)
            kn = kp * lax.rsqrt(_seg_sum64(kp * kp, bd) + EPS)
            bp = bx[:, ls]
            gp = gg[:, ls]
            glast = gp[c - 1:c, :]
            eg_p = jnp.exp(gp)
            kb = kn * bp
            kbg = kb * eg_p
            qg = qn * eg_p
            kdec = kn * jnp.exp(glast - gp)
            rhs = jnp.concatenate([vp * bp, kbg], axis=1).astype(BF16)
            kn_b = kn.astype(BF16)

            sols, attns = [], []
            for hh in range(2):
                h = 2 * p + hh
                mine = left if hh == 0 else right
                gd = gg[:, GDN_QK + h * LANES:GDN_QK + (h + 1) * LANES]
                grow = jnp.sum(jnp.where(eye, gd, 0.0), axis=0, keepdims=True)
                dec = jnp.where(causal, jnp.exp(jnp.minimum(gd - grow, 0.0)), 0.0)[:, :c]
                lhs = jnp.concatenate([jnp.where(mine, kb, 0.0), jnp.where(mine, qn, 0.0)], axis=0)
                kq = _dot_nt(lhs.astype(BF16), kn_b)
                a_neg = jnp.where(strict, -(kq[:c] * dec), 0.0)
                attns.append((kq[c:] * dec).astype(BF16))
                a_b = a_neg.astype(BF16)
                pw = _dot(a_b, a_b)
                xk = eye_c + a_neg
                span = 4
                while span < c:
                    both = _dot(jnp.concatenate([pw, xk], axis=0).astype(BF16), pw.astype(BF16))
                    pw = both[:c]
                    xk = xk + both[c:]
                    span *= 2
                xk = xk + _dot(xk.astype(BF16), pw.astype(BF16))
                sols.append(_dot(xk.astype(BF16), rhs))
            sol = jnp.where(left2, sols[0], sols[1])
            u = sol[:, :LANES]
            w = sol[:, LANES:]
            s_old = s_ref[0, p]
            wq = _dot(jnp.concatenate([w, qg], axis=0).astype(BF16), s_old.astype(BF16))
            v_new = u - wq[:c]
            v_new_b = v_new.astype(BF16)
            av = jnp.where(left, _dot(attns[0], v_new_b), _dot(attns[1], v_new_b))
            o = wq[c:] + av
            upd = _dot_tn(kdec.astype(BF16), v_new_b)
            s_ref[0, p] = s_old * jnp.exp(glast) + jnp.where(bd_mask, upd, 0.0)

            ms = _seg_sum64(o * o, bd) * (1.0 / GDN_DK)
            zp = z_ref[0, r0:r0 + c, ls]
            o_ref[0, r0:r0 + c, ls] = o * lax.rsqrt(ms + EPS) * nw_ref[...] * (zp * _sigmoid(zp))


def _gdn(proj, s0_bd, alog_row, dtb_row, nw_row, consts, *, n_chunks, n_valid):
    B, L, _ = proj.shape
    tile = n_chunks * CHUNK
    ltri, eb, eg, bd = consts
    kern = functools.partial(_gdn_kernel, n_chunks=n_chunks, n_valid=n_valid)
    full = lambda shape: pl.BlockSpec(shape, lambda b, t: (0,) * len(shape))
    col = lambda j: pl.BlockSpec((1, tile, GDN_QK), lambda b, t: (b, t, j))
    return pl.pallas_call(
        kern,
        grid=(B, L // tile),
        in_specs=[
            col(0), col(1), col(2), col(3),
            pl.BlockSpec((1, tile, LANES), lambda b, t: (b, t, GATE_COL // LANES)),
            pl.BlockSpec((1, GDN_HEADS // 2, LANES, LANES), lambda b, t: (b, 0, 0, 0)),
            full((1, LANES)), full((1, LANES)), full((1, LANES)),
            full(ltri.shape), full(eb.shape), full(eg.shape), full(bd.shape),
        ],
        out_specs=[
            pl.BlockSpec((1, tile, GDN_QK), lambda b, t: (b, t, 0)),
            pl.BlockSpec((1, GDN_HEADS // 2, LANES, LANES), lambda b, t: (b, 0, 0, 0)),
        ],
        out_shape=[
            jax.ShapeDtypeStruct((B, L, GDN_QK), F32),
            jax.ShapeDtypeStruct((B, GDN_HEADS // 2, LANES, LANES), F32),
        ],
        compiler_params=pltpu.CompilerParams(
            dimension_semantics=("arbitrary", "arbitrary"), vmem_limit_bytes=VMEM_LIMIT),
        name="gdn_chunks",
    )(proj, proj, proj, proj, proj, s0_bd, alog_row, dtb_row, nw_row, ltri, eb, eg, bd)


def _hgrn_levels():
    out = []
    m = 1
    while m < CHUNK:
        out.append(m)
        m *= 2
    return out


def _hgrn_consts():
    c = CHUNK
    ltri = np.tril(np.ones((c, c), np.float32))
    idx = np.arange(c)
    mats = [ltri]
    for m in _hgrn_levels():
        ref = (idx // (2 * m)) * 2 * m + m
        mats.append(ltri[ref])
    return jnp.asarray(np.concatenate(mats, axis=0), BF16)


def _hgrn_kernel(q_ref, f_ref, i_ref, z_ref, s0_ref, lb_ref, nw_ref, wsel_ref, o_ref, s_ref,
                 *, n_chunks, n_valid):
    c = CHUNK
    t = pl.program_id(1)

    @pl.when(t == 0)
    def _():
        s_ref[...] = s0_ref[...]

    row = lax.broadcasted_iota(jnp.int32, (c, LANES), 0)
    ri = lax.broadcasted_iota(jnp.int32, (c, c), 0)
    ci_ = lax.broadcasted_iota(jnp.int32, (c, c), 1)
    levels = _hgrn_levels()
    wsel = wsel_ref[...]

    for ci in range(n_chunks):
        r0 = ci * c
        lb = lb_ref[...]
        f = lb + (1.0 - lb) * _sigmoid(f_ref[0, r0:r0 + c, :])
        logf = jnp.log(f)
        kk = 1.0 - f
        if n_valid is not None:
            valid = lax.broadcasted_iota(jnp.int32, (c, HG_QK), 0) + r0 < n_valid
            logf = jnp.where(valid, logf, 0.0)
            kk = jnp.where(valid, kk, 0.0)
        bsel = _sel_left(wsel, logf)

        for h in range(HG_HEADS):
            ls = slice(h * LANES, (h + 1) * LANES)
            hq = q_ref[0, r0:r0 + c, ls]
            q = hq * _sigmoid(hq)
            k = kk[:, ls]
            v = i_ref[0, r0:r0 + c, ls]
            if n_valid is not None:
                v = jnp.where(row + r0 < n_valid, v, 0.0)
            b = bsel[0:c, ls]
            k_b = k.astype(BF16)
            attn = jnp.where(ri == ci_, _dot_nt(q.astype(BF16), k_b), 0.0)
            for li, m in enumerate(levels):
                d = b - bsel[(li + 1) * c:(li + 2) * c, ls]
                e = jnp.exp(-jnp.abs(d))
                second = (row & m) != 0
                qt = jnp.where(second, q * e, 0.0).astype(BF16)
                kt = jnp.where(second, 0.0, k * e).astype(BF16)
                same = (ri & -(2 * m)) == (ci_ & -(2 * m))
                attn = attn + jnp.where(same, _dot_nt(qt, kt), 0.0)
            st = s_ref[0, h]
            blast = b[c - 1:c, :]
            o = _dot_nt((q * jnp.exp(b)).astype(BF16), st.astype(BF16)) + _dot(attn.astype(BF16), v.astype(BF16))
            kdec = k * jnp.exp(blast - b)
            s_ref[0, h] = st * jnp.exp(blast) + _dot_tn(v.astype(BF16), kdec.astype(BF16))

            ms = jnp.mean(o * o, axis=-1, keepdims=True)
            zp = z_ref[0, r0:r0 + c, ls]
            o_ref[0, r0:r0 + c, ls] = o * lax.rsqrt(ms + EPS) * nw_ref[...] * (zp * _sigmoid(zp))


def _hgrn(proj, s0_t, lb_row, nw_row, wsel, *, n_chunks, n_valid):
    B, L, _ = proj.shape
    tile = n_chunks * CHUNK
    kern = functools.partial(_hgrn_kernel, n_chunks=n_chunks, n_valid=n_valid)
    full = lambda shape: pl.BlockSpec(shape, lambda b, t: (0,) * len(shape))
    col = lambda j: pl.BlockSpec((1, tile, HG_QK), lambda b, t: (b, t, j))
    return pl.pallas_call(
        kern,
        grid=(B, L // tile),
        in_specs=[
            col(4), col(5), col(6), col(7),
            pl.BlockSpec((1, HG_HEADS, HG_DK, HG_DK), lambda b, t: (b, 0, 0, 0)),
            full((1, HG_QK)), full((1, HG_DK)), full(wsel.shape),
        ],
        out_specs=[
            pl.BlockSpec((1, tile, HG_QK), lambda b, t: (b, t, 0)),
            pl.BlockSpec((1, HG_HEADS, HG_DK, HG_DK), lambda b, t: (b, 0, 0, 0)),
        ],
        out_shape=[
            jax.ShapeDtypeStruct((B, L, HG_QK), F32),
            jax.ShapeDtypeStruct((B, HG_HEADS, HG_DK, HG_DK), F32),
        ],
        compiler_params=pltpu.CompilerParams(
            dimension_semantics=("arbitrary", "arbitrary"), vmem_limit_bytes=VMEM_LIMIT),
        name="hgrn_chunks",
    )(proj, proj, proj, proj, s0_t, lb_row, nw_row, wsel)


def _outproj_kernel(x_ref, oa_ref, ob_ref, w_ref, fw_ref, y_ref):
    half = oa_ref.shape[2]
    y = (x_ref[0]
         + _dot(oa_ref[0].astype(BF16), w_ref[0:half, :])
         + _dot(ob_ref[0].astype(BF16), w_ref[half:, :]))
    y_ref[0] = y * lax.rsqrt(jnp.mean(y * y, axis=-1, keepdims=True) + EPS) * fw_ref[...]


def _outproj(x, oa, ob, w_out_b, fw_row, *, tile):
    B, L, _ = x.shape
    return pl.pallas_call(
        _outproj_kernel,
        grid=(B, L // tile),
        in_specs=[
            pl.BlockSpec((1, tile, D_MODEL), lambda b, t: (b, t, 0)),
            pl.BlockSpec((1, tile, GDN_QK), lambda b, t: (b, t, 0)),
            pl.BlockSpec((1, tile, HG_QK), lambda b, t: (b, t, 0)),
            pl.BlockSpec((D_MODEL, D_MODEL), lambda b, t: (0, 0)),
            pl.BlockSpec((1, D_MODEL), lambda b, t: (0, 0)),
        ],
        out_specs=pl.BlockSpec((1, tile, D_MODEL), lambda b, t: (b, t, 0)),
        out_shape=jax.ShapeDtypeStruct((B, L, D_MODEL), F32),
        compiler_params=pltpu.CompilerParams(
            dimension_semantics=("arbitrary", "arbitrary"), vmem_limit_bytes=VMEM_LIMIT),
        name="outproj_norm",
    )(x, oa, ob, w_out_b, fw_row)


def _layer(x, conv_state, s_gdn, s_hg, params, consts, *, n_valid, in_tile, n_chunks, out_tile):
    (norm_w, w_in_r, conv_w, alog_row, dtb_row, gdn_nw_row, lb_row, hg_nw_row, w_out_b, fw_row) = params
    gdn_consts, hg_wsel = consts
    B, L, _ = x.shape
    cs8 = jnp.pad(conv_state, ((0, 0), (SUBLANES - (CONV_W - 1), 0), (0, 0)))
    proj, nc8 = _inproj(x, norm_w, w_in_r, conv_w, cs8, tile=in_tile,
                        n_valid=in_tile if n_valid is None else n_valid)
    new_conv = nc8[:, SUBLANES - (CONV_W - 1):, :]

    sg = s_gdn.reshape(B, GDN_HEADS // 2, 2, GDN_DK, GDN_DK)
    z = jnp.zeros_like(sg[:, :, 0])
    s0_bd = jnp.concatenate([jnp.concatenate([sg[:, :, 0], z], axis=-1),
                             jnp.concatenate([z, sg[:, :, 1]], axis=-1)], axis=-2)
    oa, s_bd = _gdn(proj, s0_bd, alog_row, dtb_row, gdn_nw_row, gdn_consts,
                    n_chunks=n_chunks, n_valid=n_valid)
    new_gdn = jnp.stack([s_bd[:, :, :GDN_DK, :GDN_DK], s_bd[:, :, GDN_DK:, GDN_DK:]], axis=2)
    new_gdn = new_gdn.reshape(B, GDN_HEADS, GDN_DK, GDN_DK)

    ob, s_t = _hgrn(proj, jnp.swapaxes(s_hg, -1, -2), lb_row, hg_nw_row, hg_wsel,
                    n_chunks=n_chunks, n_valid=n_valid)
    new_hg = jnp.swapaxes(s_t, -1, -2)

    y = _outproj(x, oa, ob, w_out_b, fw_row, tile=out_tile)
    return y, new_conv, new_gdn, new_hg


def _prep(norm_w, w_in, conv_w, gdn_A_log, gdn_dt_bias, gdn_norm_w, hgrn_lb_logits, hgrn_norm_w, w_out,
          final_norm_w):
    w = w_in[0]
    o_qkv, o_za = 0, GDN_CONV_CH
    o_b = o_za + GDN_QK
    o_a = o_b + GDN_HEADS
    o_hq = o_a + GDN_HEADS
    gate_cols = jnp.pad(w[:, o_b:o_hq], ((0, 0), (0, LANES - 2 * GDN_HEADS)))
    w_in_r = jnp.concatenate([w[:, o_qkv:o_b], w[:, o_hq:], gate_cols], axis=1).astype(BF16)

    pad_row = lambda v: jnp.pad(v.astype(F32), (GDN_HEADS, LANES - 2 * GDN_HEADS))[None, :]
    lb = jnp.cumsum(jax.nn.softmax(hgrn_lb_logits.astype(F32), axis=0), axis=0)[0]
    params = (norm_w[0][None, :], w_in_r, conv_w[0], pad_row(gdn_A_log[0]), pad_row(gdn_dt_bias[0]),
              jnp.tile(gdn_norm_w[0], LANES // GDN_DK)[None, :], lb[None, :], hgrn_norm_w[0][None, :],
              w_out[0].astype(BF16), final_norm_w[None, :])
    consts = (_gdn_consts(), _hgrn_consts())
    return params, consts


def kernel(x_prompt, x_sample, state_conv, state_gdn, state_hgrn, norm_w, w_in, conv_w, gdn_A_log,
           gdn_dt_bias, gdn_norm_w, hgrn_lb_logits, hgrn_norm_w, w_out, final_norm_w):
    Bp, Lp, _ = x_prompt.shape
    Bs, Ls, _ = x_sample.shape
    params, consts = _prep(norm_w, w_in, conv_w, gdn_A_log, gdn_dt_bias, gdn_norm_w, hgrn_lb_logits,
                           hgrn_norm_w, w_out, final_norm_w)

    zeros = lambda *s: jnp.zeros(s, F32)
    y_p, c_p, g_p, r_p = _layer(
        x_prompt, zeros(Bp, CONV_W - 1, GDN_CONV_CH), zeros(Bp, GDN_HEADS, GDN_DK, GDN_DK),
        zeros(Bp, HG_HEADS, HG_DK, HG_DK), params, consts,
        n_valid=None, in_tile=256, n_chunks=4, out_tile=512)

    xs = jnp.pad(x_sample, ((0, 0), (0, CHUNK - Ls), (0, 0)))
    y_s, c_s, g_s, r_s = _layer(
        xs, state_conv[0], state_gdn[0], state_hgrn[0], params, consts,
        n_valid=Ls, in_tile=CHUNK, n_chunks=1, out_tile=CHUNK)
    y_s = y_s[:, :Ls]

    return (y_p, y_s, c_p[None], g_p[None], r_p[None], c_s[None], g_s[None], r_s[None])
```

```python
import functools

import numpy as np
import jax
import jax.numpy as jnp
from jax import lax
from jax.experimental import pallas as pl
from jax.experimental.pallas import tpu as pltpu

F32 = jnp.float32
BF16 = jnp.bfloat16

D_MODEL = 1024
CHUNK = 64
GDN_HEADS = 8
GDN_DK = 64
GDN_QK = GDN_HEADS * GDN_DK
GDN_CONV_CH = 3 * GDN_QK
CONV_W = 4
HG_HEADS = 4
HG_DK = 128
HG_QK = HG_HEADS * HG_DK
EPS = 1e-6

LANES = 128
SUBLANES = 8
MXU_DIM = 256
SLAB = 512
GATE_COL = GDN_CONV_CH + GDN_QK + 4 * HG_QK
N_PROJ = GATE_COL + LANES
GDN_HALF = MXU_DIM
GDN_HALVES = GDN_QK // GDN_HALF
VMEM_LIMIT = 48 * 1024 * 1024


def _sigmoid(x):
    return 1.0 / (1.0 + jnp.exp(-x))


def _split3(x):
    hi = x.astype(BF16)
    r = x - hi.astype(F32)
    mid = r.astype(BF16)
    lo = (r - mid.astype(F32)).astype(BF16)
    return hi, mid, lo


def _dot(a, b):
    return jnp.dot(a, b, preferred_element_type=F32)


def _dot_nt(a, b):
    return lax.dot_general(a, b, (((1,), (1,)), ((), ())), preferred_element_type=F32)


def _dot_tn(a, b):
    return lax.dot_general(a, b, (((0,), (0,)), ((), ())), preferred_element_type=F32)


def _sel_left(sel01, x):
    hi, mid, lo = _split3(x)
    return _dot(sel01, hi) + _dot(sel01, mid) + _dot(sel01, lo)


def _sel_right(x, sel01):
    hi, mid, lo = _split3(x)
    return _dot(hi, sel01) + _dot(mid, sel01) + _dot(lo, sel01)


def _inproj_kernel(x_ref, nw_ref, w_ref, cw_ref, cs_ref, proj_ref, nc_ref, ubuf, *, n_valid):
    t = pl.program_id(1)
    T = x_ref.shape[1]

    @pl.when(t == 0)
    def _():
        ubuf[0:SUBLANES, :] = cs_ref[0]

    x = x_ref[0]
    h = x * lax.rsqrt(jnp.mean(x * x, axis=-1, keepdims=True) + EPS) * nw_ref[...]
    hb = h.astype(BF16)

    for s in range(0, GDN_CONV_CH, SLAB):
        ubuf[SUBLANES:SUBLANES + T, s:s + SLAB] = _dot(hb, w_ref[:, s:s + SLAB])
    for s in range(GDN_CONV_CH, GATE_COL, SLAB):
        proj_ref[0, :, s:s + SLAB] = _dot(hb, w_ref[:, s:s + SLAB])
    proj_ref[0, :, GATE_COL:N_PROJ] = _dot(hb, w_ref[:, GATE_COL:N_PROJ])

    for s in range(0, GDN_CONV_CH, SLAB):
        y = cw_ref[CONV_W - 1:CONV_W, s:s + SLAB] * ubuf[SUBLANES:SUBLANES + T, s:s + SLAB]
        for j in range(1, CONV_W):
            y = y + (cw_ref[CONV_W - 1 - j:CONV_W - j, s:s + SLAB]
                     * ubuf[SUBLANES - j:SUBLANES - j + T, s:s + SLAB])
        proj_ref[0, :, s:s + SLAB] = y * _sigmoid(y)

    nc_ref[0] = ubuf[n_valid:n_valid + SUBLANES, :]
    ubuf[0:SUBLANES, :] = ubuf[T:T + SUBLANES, :]


def _inproj(x, norm_w, w_in_r, conv_w, conv_state8, *, tile, n_valid):
    B, L, _ = x.shape
    kern = functools.partial(_inproj_kernel, n_valid=n_valid)
    return pl.pallas_call(
        kern,
        grid=(B, L // tile),
        in_specs=[
            pl.BlockSpec((1, tile, D_MODEL), lambda b, t: (b, t, 0)),
            pl.BlockSpec((1, D_MODEL), lambda b, t: (0, 0)),
            pl.BlockSpec((D_MODEL, N_PROJ), lambda b, t: (0, 0)),
            pl.BlockSpec((CONV_W, GDN_CONV_CH), lambda b, t: (0, 0)),
            pl.BlockSpec((1, SUBLANES, GDN_CONV_CH), lambda b, t: (b, 0, 0)),
        ],
        out_specs=[
            pl.BlockSpec((1, tile, N_PROJ), lambda b, t: (b, t, 0)),
            pl.BlockSpec((1, SUBLANES, GDN_CONV_CH), lambda b, t: (b, 0, 0)),
        ],
        out_shape=[
            jax.ShapeDtypeStruct((B, L, N_PROJ), F32),
            jax.ShapeDtypeStruct((B, SUBLANES, GDN_CONV_CH), F32),
        ],
        scratch_shapes=[pltpu.VMEM((tile + SUBLANES, GDN_CONV_CH), F32)],
        compiler_params=pltpu.CompilerParams(
            dimension_semantics=("arbitrary", "arbitrary"), vmem_limit_bytes=VMEM_LIMIT),
        name="inproj_conv",
    )(x, norm_w, w_in_r, conv_w, conv_state8)


def _gdn_consts():
    ltri = np.tril(np.ones((CHUNK, CHUNK), np.float32))
    lane = np.arange(GDN_QK)
    eb = np.zeros((LANES, GDN_QK), np.float32)
    eb[lane // GDN_DK, lane] = 1.0
    eg = np.zeros((LANES, GDN_QK), np.float32)
    eg[GDN_HEADS + lane // GDN_DK, lane] = 1.0
    bd = np.kron(np.eye(GDN_HALF // GDN_DK, dtype=np.float32), np.ones((GDN_DK, GDN_DK), np.float32))
    return (jnp.asarray(ltri, BF16), jnp.asarray(eb, BF16), jnp.asarray(eg, BF16), jnp.asarray(bd, BF16))


def _gdn_kernel(q_ref, k_ref, v_ref, z_ref, g_ref, s0_ref, alog_ref, dtb_ref, nw_ref,
                ltri_ref, eb_ref, eg_ref, bd_ref, o_ref, s_ref, *, n_chunks, n_valid):
    c = CHUNK
    W = GDN_HALF
    t = pl.program_id(1)

    @pl.when(t == 0)
    def _():
        s_ref[...] = s0_ref[...]

    row = lax.broadcasted_iota(jnp.int32, (c, W), 0)
    col = lax.broadcasted_iota(jnp.int32, (c, W), 1) & (GDN_DK - 1)
    eye_t = col == row
    causal_t = col <= row
    strict_t = col < row
    eye_f = eye_t.astype(F32)
    grow_ = lax.broadcasted_iota(jnp.int32, (c, LANES), 0)
    glane = lax.broadcasted_iota(jnp.int32, (c, LANES), 1)
    bd_mask = ((lax.broadcasted_iota(jnp.int32, (W, W), 0) & -GDN_DK)
               == (lax.broadcasted_iota(jnp.int32, (W, W), 1) & -GDN_DK))
    bd01 = bd_ref[...]
    neg_a = -jnp.exp(alog_ref[...])
    ltri = ltri_ref[...]

    def block_diag(x_b):
        return jnp.concatenate([x_b] * (W // c), axis=0) * bd01

    chunks = range(n_chunks)
    items = [(ci, a) for ci in chunks for a in range(GDN_HALVES)]

    gbs = []
    for ci in chunks:
        r0 = ci * c
        gz = g_ref[0, r0:r0 + c, :]
        xg = gz + dtb_ref[...]
        softplus = jnp.maximum(xg, 0.0) + jnp.log(1.0 + jnp.exp(-jnp.abs(xg)))
        gb = jnp.where(glane < GDN_HEADS, _sigmoid(gz),
                       jnp.where(glane < 2 * GDN_HEADS, neg_a * softplus, 0.0))
        if n_valid is not None:
            gb = jnp.where(grow_ + r0 < n_valid, gb, 0.0)
        gbs.append(gb)
    gcums = [_sel_left(ltri, gb) for gb in gbs]
    bxs = [_sel_right(gb, eb_ref[...]) for gb in gbs]
    gws = [_sel_right(gc, eg_ref[...]) for gc in gcums]

    st = {}
    for (ci, a) in items:
        r0 = ci * c
        ls = slice(a * W, (a + 1) * W)
        qp = q_ref[0, r0:r0 + c, ls]
        kp = k_ref[0, r0:r0 + c, ls]
        vp = v_ref[0, r0:r0 + c, ls]
        ss = _dot(jnp.concatenate([qp * qp, kp * kp], axis=0).astype(BF16), bd01)
        qn = qp * lax.rsqrt(ss[:c] + EPS) * (GDN_DK ** -0.5)
        kn = kp * lax.rsqrt(ss[c:] + EPS)
        bp = bxs[ci][:, ls]
        gp = gws[ci][:, ls]
        glast = gp[c - 1:c, :]
        e_g = jnp.exp(gp)
        kb = kn * bp
        gdiag = jnp.sum(jnp.where(eye_t, gp, 0.0), axis=0, keepdims=True)
        dec = jnp.where(causal_t, jnp.exp(jnp.minimum(gp - gdiag, 0.0)), 0.0)
        kq = _dot_nt(jnp.concatenate([kb, qn], axis=0).astype(BF16), block_diag(kn.astype(BF16)))
        a_neg = jnp.where(strict_t, -(kq[:c] * dec), 0.0)
        st[(ci, a)] = dict(
            a_neg=a_neg, attn=(kq[c:] * dec).astype(BF16), vb=(vp * bp).astype(BF16),
            kbg=(kb * e_g).astype(BF16), qg=qn * e_g, kdec=(kn * jnp.exp(glast - gp)).astype(BF16),
            eglast=jnp.exp(glast))

    for it in items:
        a_b = st[it]["a_neg"].astype(BF16)
        st[it]["pw"] = _dot(a_b, block_diag(a_b))
        st[it]["xk"] = eye_f + st[it]["a_neg"]
    span = 4
    while span < c:
        for it in items:
            pw, xk = st[it]["pw"], st[it]["xk"]
            both = _dot(jnp.concatenate([pw, xk], axis=0).astype(BF16), block_diag(pw.astype(BF16)))
            st[it]["pw"] = both[:c]
            st[it]["xk"] = xk + both[c:]
        span *= 2
    for it in items:
        pw, xk = st[it]["pw"], st[it]["xk"]
        t_b = (xk + _dot(xk.astype(BF16), block_diag(pw.astype(BF16)))).astype(BF16)
        st[it]["u"] = _dot(t_b, block_diag(st[it]["vb"]))
        st[it]["w"] = _dot(t_b, block_diag(st[it]["kbg"]))

    for ci in chunks:
        r0 = ci * c
        for a in range(GDN_HALVES):
            d = st[(ci, a)]
            ls = slice(a * W, (a + 1) * W)
            s_old = s_ref[0, a]
            wq = _dot(jnp.concatenate([d["w"], d["qg"]], axis=0).astype(BF16), s_old.astype(BF16))
            v_new_b = (d["u"] - wq[:c]).astype(BF16)
            o = wq[c:] + _dot(d["attn"], block_diag(v_new_b))
            upd = _dot_tn(d["kdec"], v_new_b)
            s_ref[0, a] = s_old * d["eglast"] + jnp.where(bd_mask, upd, 0.0)

            ms = _dot((o * o).astype(BF16), bd01) * (1.0 / GDN_DK)
            zp = z_ref[0, r0:r0 + c, ls]
            o_ref[0, r0:r0 + c, ls] = o * lax.rsqrt(ms + EPS) * nw_ref[...] * (zp * _sigmoid(zp))


def _gdn(proj, s0_bd, alog_row, dtb_row, nw_row, consts, *, n_chunks, n_valid):
    B, L, _ = proj.shape
    tile = n_chunks * CHUNK
    ltri, eb, eg, bd = consts
    kern = functools.partial(_gdn_kernel, n_chunks=n_chunks, n_valid=n_valid)
    full = lambda shape: pl.BlockSpec(shape, lambda b, t: (0,) * len(shape))
    col = lambda j: pl.BlockSpec((1, tile, GDN_QK), lambda b, t: (b, t, j))
    state = pl.BlockSpec((1, GDN_HALVES, GDN_HALF, GDN_HALF), lambda b, t: (b, 0, 0, 0))
    return pl.pallas_call(
        kern,
        grid=(B, L // tile),
        in_specs=[
            col(0), col(1), col(2), col(3),
            pl.BlockSpec((1, tile, LANES), lambda b, t: (b, t, GATE_COL // LANES)),
            state,
            full((1, LANES)), full((1, LANES)), full((1, GDN_HALF)),
            full(ltri.shape), full(eb.shape), full(eg.shape), full(bd.shape),
        ],
        out_specs=[pl.BlockSpec((1, tile, GDN_QK), lambda b, t: (b, t, 0)), state],
        out_shape=[
            jax.ShapeDtypeStruct((B, L, GDN_QK), F32),
            jax.ShapeDtypeStruct((B, GDN_HALVES, GDN_HALF, GDN_HALF), F32),
        ],
        compiler_params=pltpu.CompilerParams(
            dimension_semantics=("arbitrary", "arbitrary"), vmem_limit_bytes=VMEM_LIMIT),
        name="gdn_chunks",
    )(proj, proj, proj, proj, proj, s0_bd, alog_row, dtb_row, nw_row, ltri, eb, eg, bd)


def _hgrn_levels():
    out = []
    m = 1
    while m < CHUNK:
        out.append(m)
        m *= 2
    return out


def _hgrn_consts():
    c = CHUNK
    ltri = np.tril(np.ones((c, c), np.float32))
    idx = np.arange(c)
    mats = [ltri]
    for m in _hgrn_levels():
        ref = (idx // (2 * m)) * 2 * m + m
        mats.append(ltri[ref])
    return jnp.asarray(np.concatenate(mats, axis=0), BF16)


def _hgrn_kernel(q_ref, f_ref, i_ref, z_ref, s0_ref, lb_ref, nw_ref, wsel_ref, o_ref, s_ref,
                 *, n_chunks, n_valid):
    c = CHUNK
    t = pl.program_id(1)

    @pl.when(t == 0)
    def _():
        s_ref[...] = s0_ref[...]

    row = lax.broadcasted_iota(jnp.int32, (c, LANES), 0)
    ri = lax.broadcasted_iota(jnp.int32, (c, c), 0)
    ci_ = lax.broadcasted_iota(jnp.int32, (c, c), 1)
    levels = _hgrn_levels()
    wsel = wsel_ref[...]

    for ci in range(n_chunks):
        r0 = ci * c
        lb = lb_ref[...]
        f = lb + (1.0 - lb) * _sigmoid(f_ref[0, r0:r0 + c, :])
        logf = jnp.log(f)
        kk = 1.0 - f
        if n_valid is not None:
            valid = lax.broadcasted_iota(jnp.int32, (c, HG_QK), 0) + r0 < n_valid
            logf = jnp.where(valid, logf, 0.0)
            kk = jnp.where(valid, kk, 0.0)
        bsel = _sel_left(wsel, logf)

        for h in range(HG_HEADS):
            ls = slice(h * LANES, (h + 1) * LANES)
            hq = q_ref[0, r0:r0 + c, ls]
            q = hq * _sigmoid(hq)
            k = kk[:, ls]
            v = i_ref[0, r0:r0 + c, ls]
            if n_valid is not None:
                v = jnp.where(row + r0 < n_valid, v, 0.0)
            b = bsel[0:c, ls]
            k_b = k.astype(BF16)
            attn = jnp.where(ri == ci_, _dot_nt(q.astype(BF16), k_b), 0.0)
            for li, m in enumerate(levels):
                d = b - bsel[(li + 1) * c:(li + 2) * c, ls]
                e = jnp.exp(-jnp.abs(d))
                second = (row & m) != 0
                qt = jnp.where(second, q * e, 0.0).astype(BF16)
                kt = jnp.where(second, 0.0, k * e).astype(BF16)
                same = (ri & -(2 * m)) == (ci_ & -(2 * m))
                attn = attn + jnp.where(same, _dot_nt(qt, kt), 0.0)
            st = s_ref[0, h]
            blast = b[c - 1:c, :]
            o = _dot_nt((q * jnp.exp(b)).astype(BF16), st.astype(BF16)) + _dot(attn.astype(BF16), v.astype(BF16))
            kdec = k * jnp.exp(blast - b)
            s_ref[0, h] = st * jnp.exp(blast) + _dot_tn(v.astype(BF16), kdec.astype(BF16))

            ms = jnp.mean(o * o, axis=-1, keepdims=True)
            zp = z_ref[0, r0:r0 + c, ls]
            o_ref[0, r0:r0 + c, ls] = o * lax.rsqrt(ms + EPS) * nw_ref[...] * (zp * _sigmoid(zp))


def _hgrn(proj, s0_t, lb_row, nw_row, wsel, *, n_chunks, n_valid):
    B, L, _ = proj.shape
    tile = n_chunks * CHUNK
    kern = functools.partial(_hgrn_kernel, n_chunks=n_chunks, n_valid=n_valid)
    full = lambda shape: pl.BlockSpec(shape, lambda b, t: (0,) * len(shape))
    col = lambda j: pl.BlockSpec((1, tile, HG_QK), lambda b, t: (b, t, j))
    return pl.pallas_call(
        kern,
        grid=(B, L // tile),
        in_specs=[
            col(4), col(5), col(6), col(7),
            pl.BlockSpec((1, HG_HEADS, HG_DK, HG_DK), lambda b, t: (b, 0, 0, 0)),
            full((1, HG_QK)), full((1, HG_DK)), full(wsel.shape),
        ],
        out_specs=[
            pl.BlockSpec((1, tile, HG_QK), lambda b, t: (b, t, 0)),
            pl.BlockSpec((1, HG_HEADS, HG_DK, HG_DK), lambda b, t: (b, 0, 0, 0)),
        ],
        out_shape=[
            jax.ShapeDtypeStruct((B, L, HG_QK), F32),
            jax.ShapeDtypeStruct((B, HG_HEADS, HG_DK, HG_DK), F32),
        ],
        compiler_params=pltpu.CompilerParams(
            dimension_semantics=("arbitrary", "arbitrary"), vmem_limit_bytes=VMEM_LIMIT),
        name="hgrn_chunks",
    )(proj, proj, proj, proj, s0_t, lb_row, nw_row, wsel)


def _outproj_kernel(x_ref, oa_ref, ob_ref, w_ref, fw_ref, y_ref):
    half = oa_ref.shape[2]
    y = (x_ref[0]
         + _dot(oa_ref[0].astype(BF16), w_ref[0:half, :])
         + _dot(ob_ref[0].astype(BF16), w_ref[half:, :]))
    y_ref[0] = y * lax.rsqrt(jnp.mean(y * y, axis=-1, keepdims=True) + EPS) * fw_ref[...]


def _outproj(x, oa, ob, w_out_b, fw_row, *, tile):
    B, L, _ = x.shape
    return pl.pallas_call(
        _outproj_kernel,
        grid=(B, L // tile),
        in_specs=[
            pl.BlockSpec((1, tile, D_MODEL), lambda b, t: (b, t, 0)),
            pl.BlockSpec((1, tile, GDN_QK), lambda b, t: (b, t, 0)),
            pl.BlockSpec((1, tile, HG_QK), lambda b, t: (b, t, 0)),
            pl.BlockSpec((D_MODEL, D_MODEL), lambda b, t: (0, 0)),
            pl.BlockSpec((1, D_MODEL), lambda b, t: (0, 0)),
        ],
        out_specs=pl.BlockSpec((1, tile, D_MODEL), lambda b, t: (b, t, 0)),
        out_shape=jax.ShapeDtypeStruct((B, L, D_MODEL), F32),
        compiler_params=pltpu.CompilerParams(
            dimension_semantics=("arbitrary", "arbitrary"), vmem_limit_bytes=VMEM_LIMIT),
        name="outproj_norm",
    )(x, oa, ob, w_out_b, fw_row)


def _layer(x, conv_state, s_gdn, s_hg, params, consts, *, n_valid, in_tile, n_chunks, out_tile):
    (norm_w, w_in_r, conv_w, alog_row, dtb_row, gdn_nw_row, lb_row, hg_nw_row, w_out_b, fw_row) = params
    gdn_consts, hg_wsel = consts
    B, L, _ = x.shape
    cs8 = jnp.pad(conv_state, ((0, 0), (SUBLANES - (CONV_W - 1), 0), (0, 0)))
    proj, nc8 = _inproj(x, norm_w, w_in_r, conv_w, cs8, tile=in_tile,
                        n_valid=in_tile if n_valid is None else n_valid)
    new_conv = nc8[:, SUBLANES - (CONV_W - 1):, :]

    hph = GDN_HALF // GDN_DK
    sg = s_gdn.reshape(B, GDN_HALVES, hph, GDN_DK, GDN_DK)
    s0_bd = jnp.einsum("bahde,hg->bahdge", sg, jnp.eye(hph, dtype=F32)).reshape(B, GDN_HALVES, GDN_HALF, GDN_HALF)
    oa, s_bd = _gdn(proj, s0_bd, alog_row, dtb_row, gdn_nw_row, gdn_consts,
                    n_chunks=n_chunks, n_valid=n_valid)
    s_bd = s_bd.reshape(B, GDN_HALVES, hph, GDN_DK, hph, GDN_DK)
    new_gdn = jnp.stack([s_bd[:, :, h, :, h, :] for h in range(hph)], axis=2)
    new_gdn = new_gdn.reshape(B, GDN_HEADS, GDN_DK, GDN_DK)

    ob, s_t = _hgrn(proj, jnp.swapaxes(s_hg, -1, -2), lb_row, hg_nw_row, hg_wsel,
                    n_chunks=n_chunks, n_valid=n_valid)
    new_hg = jnp.swapaxes(s_t, -1, -2)

    y = _outproj(x, oa, ob, w_out_b, fw_row, tile=out_tile)
    return y, new_conv, new_gdn, new_hg


def _prep(norm_w, w_in, conv_w, gdn_A_log, gdn_dt_bias, gdn_norm_w, hgrn_lb_logits, hgrn_norm_w, w_out,
          final_norm_w):
    w = w_in[0]
    o_qkv, o_za = 0, GDN_CONV_CH
    o_b = o_za + GDN_QK
    o_a = o_b + GDN_HEADS
    o_hq = o_a + GDN_HEADS
    gate_cols = jnp.pad(w[:, o_b:o_hq], ((0, 0), (0, LANES - 2 * GDN_HEADS)))
    w_in_r = jnp.concatenate([w[:, o_qkv:o_b], w[:, o_hq:], gate_cols], axis=1).astype(BF16)

    pad_row = lambda v: jnp.pad(v.astype(F32), (GDN_HEADS, LANES - 2 * GDN_HEADS))[None, :]
    lb = jnp.cumsum(jax.nn.softmax(hgrn_lb_logits.astype(F32), axis=0), axis=0)[0]
    params = (norm_w[0][None, :], w_in_r, conv_w[0], pad_row(gdn_A_log[0]), pad_row(gdn_dt_bias[0]),
              jnp.tile(gdn_norm_w[0], GDN_HALF // GDN_DK)[None, :], lb[None, :], hgrn_norm_w[0][None, :],
              w_out[0].astype(BF16), final_norm_w[None, :])
    consts = (_gdn_consts(), _hgrn_consts())
    return params, consts


def kernel(x_prompt, x_sample, state_conv, state_gdn, state_hgrn, norm_w, w_in, conv_w, gdn_A_log,
           gdn_dt_bias, gdn_norm_w, hgrn_lb_logits, hgrn_norm_w, w_out, final_norm_w):
    Bp, Lp, _ = x_prompt.shape
    Bs, Ls, _ = x_sample.shape
    params, consts = _prep(norm_w, w_in, conv_w, gdn_A_log, gdn_dt_bias, gdn_norm_w, hgrn_lb_logits,
                           hgrn_norm_w, w_out, final_norm_w)

    zeros = lambda *s: jnp.zeros(s, F32)
    y_p, c_p, g_p, r_p = _layer(
        x_prompt, zeros(Bp, CONV_W - 1, GDN_CONV_CH), zeros(Bp, GDN_HEADS, GDN_DK, GDN_DK),
        zeros(Bp, HG_HEADS, HG_DK, HG_DK), params, consts,
        n_valid=None, in_tile=256, n_chunks=4, out_tile=512)

    xs = jnp.pad(x_sample, ((0, 0), (0, CHUNK - Ls), (0, 0)))
    y_s, c_s, g_s, r_s = _layer(
        xs, state_conv[0], state_gdn[0], state_hgrn[0], params, consts,
        n_valid=Ls, in_tile=CHUNK, n_chunks=1, out_tile=CHUNK)
    y_s = y_s[:, :Ls]

    return (y_p, y_s, c_p[None], g_p[None], r_p[None], c_s[None], g_s[None], r_s[None])
```

```python
import functools

import numpy as np
import jax
import jax.numpy as jnp
from jax import lax
from jax.experimental import pallas as pl
from jax.experimental.pallas import tpu as pltpu

F32 = jnp.float32
BF16 = jnp.bfloat16

D_MODEL = 1024
CHUNK = 64
GDN_HEADS = 8
GDN_DK = 64
GDN_QK = GDN_HEADS * GDN_DK
GDN_CONV_CH = 3 * GDN_QK
CONV_W = 4
HG_HEADS = 4
HG_DK = 128
HG_QK = HG_HEADS * HG_DK
EPS = 1e-6

LANES = 128
SUBLANES = 8
MXU_DIM = 256
SLAB = 512
GATE_COL = GDN_CONV_CH + GDN_QK + 4 * HG_QK
N_PROJ = GATE_COL + LANES
GDN_HALF = MXU_DIM
GDN_HALVES = GDN_QK // GDN_HALF
VMEM_LIMIT = 48 * 1024 * 1024


def _sigmoid(x):
    return 1.0 / (1.0 + jnp.exp(-x))


def _split3(x):
    hi = x.astype(BF16)
    r = x - hi.astype(F32)
    mid = r.astype(BF16)
    lo = (r - mid.astype(F32)).astype(BF16)
    return hi, mid, lo


def _dot(a, b):
    return jnp.dot(a, b, preferred_element_type=F32)


def _dot_nt(a, b):
    return lax.dot_general(a, b, (((1,), (1,)), ((), ())), preferred_element_type=F32)


def _dot_tn(a, b):
    return lax.dot_general(a, b, (((0,), (0,)), ((), ())), preferred_element_type=F32)


def _sel_left(sel01, x):
    hi, mid, lo = _split3(x)
    return _dot(sel01, hi) + _dot(sel01, mid) + _dot(sel01, lo)


def _sel_right(x, sel01):
    hi, mid, lo = _split3(x)
    return _dot(hi, sel01) + _dot(mid, sel01) + _dot(lo, sel01)


def _inproj_kernel(x_ref, nw_ref, w_ref, cw_ref, cs_ref, proj_ref, nc_ref, ubuf, *, n_valid):
    t = pl.program_id(1)
    T = x_ref.shape[1]

    @pl.when(t == 0)
    def _():
        ubuf[0:SUBLANES, :] = cs_ref[0]

    x = x_ref[0]
    h = x * lax.rsqrt(jnp.mean(x * x, axis=-1, keepdims=True) + EPS) * nw_ref[...]
    hb = h.astype(BF16)

    for s in range(0, GDN_CONV_CH, SLAB):
        ubuf[SUBLANES:SUBLANES + T, s:s + SLAB] = _dot(hb, w_ref[:, s:s + SLAB])
    for s in range(GDN_CONV_CH, GATE_COL, SLAB):
        proj_ref[0, :, s:s + SLAB] = _dot(hb, w_ref[:, s:s + SLAB])
    proj_ref[0, :, GATE_COL:N_PROJ] = _dot(hb, w_ref[:, GATE_COL:N_PROJ])

    for s in range(0, GDN_CONV_CH, SLAB):
        y = cw_ref[CONV_W - 1:CONV_W, s:s + SLAB] * ubuf[SUBLANES:SUBLANES + T, s:s + SLAB]
        for j in range(1, CONV_W):
            y = y + (cw_ref[CONV_W - 1 - j:CONV_W - j, s:s + SLAB]
                     * ubuf[SUBLANES - j:SUBLANES - j + T, s:s + SLAB])
        proj_ref[0, :, s:s + SLAB] = y * _sigmoid(y)

    nc_ref[0] = ubuf[n_valid:n_valid + SUBLANES, :]
    ubuf[0:SUBLANES, :] = ubuf[T:T + SUBLANES, :]


def _inproj(x, norm_w, w_in_r, conv_w, conv_state8, *, tile, n_valid):
    B, L, _ = x.shape
    kern = functools.partial(_inproj_kernel, n_valid=n_valid)
    return pl.pallas_call(
        kern,
        grid=(B, L // tile),
        in_specs=[
            pl.BlockSpec((1, tile, D_MODEL), lambda b, t: (b, t, 0)),
            pl.BlockSpec((1, D_MODEL), lambda b, t: (0, 0)),
            pl.BlockSpec((D_MODEL, N_PROJ), lambda b, t: (0, 0)),
            pl.BlockSpec((CONV_W, GDN_CONV_CH), lambda b, t: (0, 0)),
            pl.BlockSpec((1, SUBLANES, GDN_CONV_CH), lambda b, t: (b, 0, 0)),
        ],
        out_specs=[
            pl.BlockSpec((1, tile, N_PROJ), lambda b, t: (b, t, 0)),
            pl.BlockSpec((1, SUBLANES, GDN_CONV_CH), lambda b, t: (b, 0, 0)),
        ],
        out_shape=[
            jax.ShapeDtypeStruct((B, L, N_PROJ), F32),
            jax.ShapeDtypeStruct((B, SUBLANES, GDN_CONV_CH), F32),
        ],
        scratch_shapes=[pltpu.VMEM((tile + SUBLANES, GDN_CONV_CH), F32)],
        compiler_params=pltpu.CompilerParams(
            dimension_semantics=("arbitrary", "arbitrary"), vmem_limit_bytes=VMEM_LIMIT),
        name="inproj_conv",
    )(x, norm_w, w_in_r, conv_w, conv_state8)


def _gdn_consts():
    ltri = np.tril(np.ones((CHUNK, CHUNK), np.float32))
    lane = np.arange(GDN_QK)
    eb = np.zeros((LANES, GDN_QK), np.float32)
    eb[lane // GDN_DK, lane] = 1.0
    eg = np.zeros((LANES, GDN_QK), np.float32)
    eg[GDN_HEADS + lane // GDN_DK, lane] = 1.0
    bd = np.kron(np.eye(GDN_HALF // GDN_DK, dtype=np.float32), np.ones((GDN_DK, GDN_DK), np.float32))
    return (jnp.asarray(ltri, BF16), jnp.asarray(eb, BF16), jnp.asarray(eg, BF16), jnp.asarray(bd, BF16))


def _gdn_kernel(q_ref, k_ref, v_ref, z_ref, g_ref, s0_ref, alog_ref, dtb_ref, nw_ref,
                ltri_ref, eb_ref, eg_ref, bd_ref, o_ref, s_ref, *, n_chunks, n_valid):
    c = CHUNK
    W = GDN_HALF
    t = pl.program_id(1)

    @pl.when(t == 0)
    def _():
        s_ref[...] = s0_ref[...]

    row = lax.broadcasted_iota(jnp.int32, (c, W), 0)
    col = lax.broadcasted_iota(jnp.int32, (c, W), 1) & (GDN_DK - 1)
    eye_t = col == row
    causal_t = col <= row
    strict_t = col < row
    eye_f = eye_t.astype(F32)
    grow_ = lax.broadcasted_iota(jnp.int32, (c, LANES), 0)
    glane = lax.broadcasted_iota(jnp.int32, (c, LANES), 1)
    bd_mask = ((lax.broadcasted_iota(jnp.int32, (W, W), 0) & -GDN_DK)
               == (lax.broadcasted_iota(jnp.int32, (W, W), 1) & -GDN_DK))
    bd01 = bd_ref[...]
    neg_a = -jnp.exp(alog_ref[...])
    ltri = ltri_ref[...]

    def block_diag(x_b):
        return jnp.concatenate([x_b] * (W // c), axis=0) * bd01

    chunks = range(n_chunks)
    items = [(ci, a) for ci in chunks for a in range(GDN_HALVES)]

    gbs = []
    for ci in chunks:
        r0 = ci * c
        gz = g_ref[0, r0:r0 + c, :]
        xg = gz + dtb_ref[...]
        softplus = jnp.maximum(xg, 0.0) + jnp.log(1.0 + jnp.exp(-jnp.abs(xg)))
        gb = jnp.where(glane < GDN_HEADS, _sigmoid(gz),
                       jnp.where(glane < 2 * GDN_HEADS, neg_a * softplus, 0.0))
        if n_valid is not None:
            gb = jnp.where(grow_ + r0 < n_valid, gb, 0.0)
        gbs.append(gb)
    gcums = [_sel_left(ltri, gb) for gb in gbs]
    bxs = [_sel_right(gb, eb_ref[...]) for gb in gbs]
    gws = [_sel_right(gc, eg_ref[...]) for gc in gcums]

    st = {}
    for (ci, a) in items:
        r0 = ci * c
        ls = slice(a * W, (a + 1) * W)
        qp = q_ref[0, r0:r0 + c, ls]
        kp = k_ref[0, r0:r0 + c, ls]
        vp = v_ref[0, r0:r0 + c, ls]
        ss = _dot(jnp.concatenate([qp * qp, kp * kp], axis=0).astype(BF16), bd01)
        qn = qp * lax.rsqrt(ss[:c] + EPS) * (GDN_DK ** -0.5)
        kn = kp * lax.rsqrt(ss[c:] + EPS)
        bp = bxs[ci][:, ls]
        gp = gws[ci][:, ls]
        glast = gp[c - 1:c, :]
        e_g = jnp.exp(gp)
        kb = kn * bp
        gdiag = jnp.sum(jnp.where(eye_t, gp, 0.0), axis=0, keepdims=True)
        dec = jnp.where(causal_t, jnp.exp(jnp.minimum(gp - gdiag, 0.0)), 0.0)
        kq = _dot_nt(jnp.concatenate([kb, qn], axis=0).astype(BF16), block_diag(kn.astype(BF16)))
        a_neg = jnp.where(strict_t, -(kq[:c] * dec), 0.0)
        st[(ci, a)] = dict(
            a_neg=a_neg, attn=(kq[c:] * dec).astype(BF16), vb=(vp * bp).astype(BF16),
            kbg=(kb * e_g).astype(BF16), qg=qn * e_g, kdec=(kn * jnp.exp(glast - gp)).astype(BF16),
            eglast=jnp.exp(glast))

    for it in items:
        a_b = st[it]["a_neg"].astype(BF16)
        st[it]["pw"] = _dot(a_b, block_diag(a_b))
        st[it]["xk"] = eye_f + st[it]["a_neg"]
    span = 4
    while span < c:
        for it in items:
            pw, xk = st[it]["pw"], st[it]["xk"]
            both = _dot(jnp.concatenate([pw, xk], axis=0).astype(BF16), block_diag(pw.astype(BF16)))
            st[it]["pw"] = both[:c]
            st[it]["xk"] = xk + both[c:]
        span *= 2
    for it in items:
        pw, xk = st[it]["pw"], st[it]["xk"]
        t_b = (xk + _dot(xk.astype(BF16), block_diag(pw.astype(BF16)))).astype(BF16)
        st[it]["u"] = _dot(t_b, block_diag(st[it]["vb"]))
        st[it]["w"] = _dot(t_b, block_diag(st[it]["kbg"]))

    for ci in chunks:
        r0 = ci * c
        for a in range(GDN_HALVES):
            d = st[(ci, a)]
            ls = slice(a * W, (a + 1) * W)
            s_old = s_ref[0, a]
            wq = _dot(jnp.concatenate([d["w"], d["qg"]], axis=0).astype(BF16), s_old.astype(BF16))
            v_new_b = (d["u"] - wq[:c]).astype(BF16)
            o = wq[c:] + _dot(d["attn"], block_diag(v_new_b))
            upd = _dot_tn(d["kdec"], v_new_b)
            s_ref[0, a] = s_old * d["eglast"] + jnp.where(bd_mask, upd, 0.0)

            ms = _dot((o * o).astype(BF16), bd01) * (1.0 / GDN_DK)
            zp = z_ref[0, r0:r0 + c, ls]
            o_ref[0, r0:r0 + c, ls] = o * lax.rsqrt(ms + EPS) * nw_ref[...] * (zp * _sigmoid(zp))


def _gdn(proj, s0_bd, alog_row, dtb_row, nw_row, consts, *, n_chunks, n_valid):
    B, L, _ = proj.shape
    tile = n_chunks * CHUNK
    ltri, eb, eg, bd = consts
    kern = functools.partial(_gdn_kernel, n_chunks=n_chunks, n_valid=n_valid)
    full = lambda shape: pl.BlockSpec(shape, lambda b, t: (0,) * len(shape))
    col = lambda j: pl.BlockSpec((1, tile, GDN_QK), lambda b, t: (b, t, j))
    state = pl.BlockSpec((1, GDN_HALVES, GDN_HALF, GDN_HALF), lambda b, t: (b, 0, 0, 0))
    return pl.pallas_call(
        kern,
        grid=(B, L // tile),
        in_specs=[
            col(0), col(1), col(2), col(3),
            pl.BlockSpec((1, tile, LANES), lambda b, t: (b, t, GATE_COL // LANES)),
            state,
            full((1, LANES)), full((1, LANES)), full((1, GDN_HALF)),
            full(ltri.shape), full(eb.shape), full(eg.shape), full(bd.shape),
        ],
        out_specs=[pl.BlockSpec((1, tile, GDN_QK), lambda b, t: (b, t, 0)), state],
        out_shape=[
            jax.ShapeDtypeStruct((B, L, GDN_QK), F32),
            jax.ShapeDtypeStruct((B, GDN_HALVES, GDN_HALF, GDN_HALF), F32),
        ],
        compiler_params=pltpu.CompilerParams(
            dimension_semantics=("arbitrary", "arbitrary"), vmem_limit_bytes=VMEM_LIMIT),
        name="gdn_chunks",
    )(proj, proj, proj, proj, proj, s0_bd, alog_row, dtb_row, nw_row, ltri, eb, eg, bd)


def _hgrn_levels():
    out = []
    m = 1
    while m < CHUNK:
        out.append(m)
        m *= 2
    return out


def _hgrn_consts():
    return jnp.asarray(np.tril(np.ones((CHUNK, CHUNK), np.float32)), BF16)


def _level_ref_rows(b, m):
    c = b.shape[0]
    bcast = lambda r, n: jnp.broadcast_to(b[r:r + 1, :], (n, LANES))
    if 2 * m >= SUBLANES:
        parts = [bcast(blk * 2 * m + m, 2 * m) for blk in range(c // (2 * m))]
    else:
        sub = lax.broadcasted_iota(jnp.int32, (SUBLANES, LANES), 0)
        parts = []
        for g in range(c // SUBLANES):
            acc = bcast(g * SUBLANES + m, SUBLANES)
            for blk in range(1, SUBLANES // (2 * m)):
                acc = jnp.where(sub >= blk * 2 * m, bcast(g * SUBLANES + blk * 2 * m + m, SUBLANES), acc)
            parts.append(acc)
    return parts[0] if len(parts) == 1 else jnp.concatenate(parts, axis=0)


def _hgrn_kernel(q_ref, f_ref, i_ref, z_ref, s0_ref, lb_ref, nw_ref, ltri_ref, o_ref, s_ref,
                 *, n_chunks, n_valid):
    c = CHUNK
    t = pl.program_id(1)

    @pl.when(t == 0)
    def _():
        s_ref[...] = s0_ref[...]

    row = lax.broadcasted_iota(jnp.int32, (c, LANES), 0)
    ri = lax.broadcasted_iota(jnp.int32, (c, c), 0)
    ci_ = lax.broadcasted_iota(jnp.int32, (c, c), 1)
    levels = _hgrn_levels()
    ltri = ltri_ref[...]
    lb = lb_ref[...]
    eye_f = (ri == ci_).astype(F32)
    upper_f = [((row & m) != 0).astype(F32) for m in levels]
    same_f = [((ri & -(2 * m)) == (ci_ & -(2 * m))).astype(F32) for m in levels]

    kks, bs = [], []
    for ci in range(n_chunks):
        r0 = ci * c
        f = lb + (1.0 - lb) * _sigmoid(f_ref[0, r0:r0 + c, :])
        logf = jnp.log(f)
        kk = 1.0 - f
        if n_valid is not None:
            valid = lax.broadcasted_iota(jnp.int32, (c, HG_QK), 0) + r0 < n_valid
            logf = jnp.where(valid, logf, 0.0)
            kk = jnp.where(valid, kk, 0.0)
        kks.append(kk)
        bs.append(_sel_left(ltri, logf))

    st = {}
    for ci in range(n_chunks):
        r0 = ci * c
        for h in range(HG_HEADS):
            ls = slice(h * LANES, (h + 1) * LANES)
            hq = q_ref[0, r0:r0 + c, ls]
            q = hq * _sigmoid(hq)
            k = kks[ci][:, ls]
            v = i_ref[0, r0:r0 + c, ls]
            if n_valid is not None:
                v = jnp.where(row + r0 < n_valid, v, 0.0)
            b = bs[ci][:, ls]
            attn = eye_f * _dot_nt(q.astype(BF16), k.astype(BF16))
            for li, m in enumerate(levels):
                e = jnp.exp(-jnp.abs(b - _level_ref_rows(b, m)))
                e_up = e * upper_f[li]
                qt = (q * e_up).astype(BF16)
                kt = (k * (e - e_up)).astype(BF16)
                attn = attn + same_f[li] * _dot_nt(qt, kt)
            blast = b[c - 1:c, :]
            v_b = v.astype(BF16)
            st[(ci, h)] = dict(
                qe=(q * jnp.exp(b)).astype(BF16), av=_dot(attn.astype(BF16), v_b),
                upd=_dot_tn(v_b, (k * jnp.exp(blast - b)).astype(BF16)), eblast=jnp.exp(blast))

    for ci in range(n_chunks):
        r0 = ci * c
        for h in range(HG_HEADS):
            d = st[(ci, h)]
            ls = slice(h * LANES, (h + 1) * LANES)
            s_old = s_ref[0, h]
            o = _dot_nt(d["qe"], s_old.astype(BF16)) + d["av"]
            s_ref[0, h] = s_old * d["eblast"] + d["upd"]

            ms = jnp.mean(o * o, axis=-1, keepdims=True)
            zp = z_ref[0, r0:r0 + c, ls]
            o_ref[0, r0:r0 + c, ls] = o * lax.rsqrt(ms + EPS) * nw_ref[...] * (zp * _sigmoid(zp))


def _hgrn(proj, s0_t, lb_row, nw_row, wsel, *, n_chunks, n_valid):
    B, L, _ = proj.shape
    tile = n_chunks * CHUNK
    kern = functools.partial(_hgrn_kernel, n_chunks=n_chunks, n_valid=n_valid)
    full = lambda shape: pl.BlockSpec(shape, lambda b, t: (0,) * len(shape))
    col = lambda j: pl.BlockSpec((1, tile, HG_QK), lambda b, t: (b, t, j))
    return pl.pallas_call(
        kern,
        grid=(B, L // tile),
        in_specs=[
            col(4), col(5), col(6), col(7),
            pl.BlockSpec((1, HG_HEADS, HG_DK, HG_DK), lambda b, t: (b, 0, 0, 0)),
            full((1, HG_QK)), full((1, HG_DK)), full(wsel.shape),
        ],
        out_specs=[
            pl.BlockSpec((1, tile, HG_QK), lambda b, t: (b, t, 0)),
            pl.BlockSpec((1, HG_HEADS, HG_DK, HG_DK), lambda b, t: (b, 0, 0, 0)),
        ],
        out_shape=[
            jax.ShapeDtypeStruct((B, L, HG_QK), F32),
            jax.ShapeDtypeStruct((B, HG_HEADS, HG_DK, HG_DK), F32),
        ],
        compiler_params=pltpu.CompilerParams(
            dimension_semantics=("arbitrary", "arbitrary"), vmem_limit_bytes=VMEM_LIMIT),
        name="hgrn_chunks",
    )(proj, proj, proj, proj, s0_t, lb_row, nw_row, wsel)


def _outproj_kernel(x_ref, oa_ref, ob_ref, w_ref, fw_ref, y_ref):
    half = oa_ref.shape[2]
    y = (x_ref[0]
         + _dot(oa_ref[0].astype(BF16), w_ref[0:half, :])
         + _dot(ob_ref[0].astype(BF16), w_ref[half:, :]))
    y_ref[0] = y * lax.rsqrt(jnp.mean(y * y, axis=-1, keepdims=True) + EPS) * fw_ref[...]


def _outproj(x, oa, ob, w_out_b, fw_row, *, tile):
    B, L, _ = x.shape
    return pl.pallas_call(
        _outproj_kernel,
        grid=(B, L // tile),
        in_specs=[
            pl.BlockSpec((1, tile, D_MODEL), lambda b, t: (b, t, 0)),
            pl.BlockSpec((1, tile, GDN_QK), lambda b, t: (b, t, 0)),
            pl.BlockSpec((1, tile, HG_QK), lambda b, t: (b, t, 0)),
            pl.BlockSpec((D_MODEL, D_MODEL), lambda b, t: (0, 0)),
            pl.BlockSpec((1, D_MODEL), lambda b, t: (0, 0)),
        ],
        out_specs=pl.BlockSpec((1, tile, D_MODEL), lambda b, t: (b, t, 0)),
        out_shape=jax.ShapeDtypeStruct((B, L, D_MODEL), F32),
        compiler_params=pltpu.CompilerParams(
            dimension_semantics=("arbitrary", "arbitrary"), vmem_limit_bytes=VMEM_LIMIT),
        name="outproj_norm",
    )(x, oa, ob, w_out_b, fw_row)


def _layer(x, conv_state, s_gdn, s_hg, params, consts, *, n_valid, in_tile, n_chunks, out_tile):
    (norm_w, w_in_r, conv_w, alog_row, dtb_row, gdn_nw_row, lb_row, hg_nw_row, w_out_b, fw_row) = params
    gdn_consts, hg_wsel = consts
    B, L, _ = x.shape
    cs8 = jnp.pad(conv_state, ((0, 0), (SUBLANES - (CONV_W - 1), 0), (0, 0)))
    proj, nc8 = _inproj(x, norm_w, w_in_r, conv_w, cs8, tile=in_tile,
                        n_valid=in_tile if n_valid is None else n_valid)
    new_conv = nc8[:, SUBLANES - (CONV_W - 1):, :]

    hph = GDN_HALF // GDN_DK
    sg = s_gdn.reshape(B, GDN_HALVES, hph, GDN_DK, GDN_DK)
    s0_bd = jnp.einsum("bahde,hg->bahdge", sg, jnp.eye(hph, dtype=F32)).reshape(B, GDN_HALVES, GDN_HALF, GDN_HALF)
    oa, s_bd = _gdn(proj, s0_bd, alog_row, dtb_row, gdn_nw_row, gdn_consts,
                    n_chunks=n_chunks, n_valid=n_valid)
    s_bd = s_bd.reshape(B, GDN_HALVES, hph, GDN_DK, hph, GDN_DK)
    new_gdn = jnp.stack([s_bd[:, :, h, :, h, :] for h in range(hph)], axis=2)
    new_gdn = new_gdn.reshape(B, GDN_HEADS, GDN_DK, GDN_DK)

    ob, s_t = _hgrn(proj, jnp.swapaxes(s_hg, -1, -2), lb_row, hg_nw_row, hg_wsel,
                    n_chunks=n_chunks, n_valid=n_valid)
    new_hg = jnp.swapaxes(s_t, -1, -2)

    y = _outproj(x, oa, ob, w_out_b, fw_row, tile=out_tile)
    return y, new_conv, new_gdn, new_hg


def _prep(norm_w, w_in, conv_w, gdn_A_log, gdn_dt_bias, gdn_norm_w, hgrn_lb_logits, hgrn_norm_w, w_out,
          final_norm_w):
    w = w_in[0]
    o_qkv, o_za = 0, GDN_CONV_CH
    o_b = o_za + GDN_QK
    o_a = o_b + GDN_HEADS
    o_hq = o_a + GDN_HEADS
    gate_cols = jnp.pad(w[:, o_b:o_hq], ((0, 0), (0, LANES - 2 * GDN_HEADS)))
    w_in_r = jnp.concatenate([w[:, o_qkv:o_b], w[:, o_hq:], gate_cols], axis=1).astype(BF16)

    pad_row = lambda v: jnp.pad(v.astype(F32), (GDN_HEADS, LANES - 2 * GDN_HEADS))[None, :]
    lb = jnp.cumsum(jax.nn.softmax(hgrn_lb_logits.astype(F32), axis=0), axis=0)[0]
    params = (norm_w[0][None, :], w_in_r, conv_w[0], pad_row(gdn_A_log[0]), pad_row(gdn_dt_bias[0]),
              jnp.tile(gdn_norm_w[0], GDN_HALF // GDN_DK)[None, :], lb[None, :], hgrn_norm_w[0][None, :],
              w_out[0].astype(BF16), final_norm_w[None, :])
    consts = (_gdn_consts(), _hgrn_consts())
    return params, consts


def kernel(x_prompt, x_sample, state_conv, state_gdn, state_hgrn, norm_w, w_in, conv_w, gdn_A_log,
           gdn_dt_bias, gdn_norm_w, hgrn_lb_logits, hgrn_norm_w, w_out, final_norm_w):
    Bp, Lp, _ = x_prompt.shape
    Bs, Ls, _ = x_sample.shape
    params, consts = _prep(norm_w, w_in, conv_w, gdn_A_log, gdn_dt_bias, gdn_norm_w, hgrn_lb_logits,
                           hgrn_norm_w, w_out, final_norm_w)

    zeros = lambda *s: jnp.zeros(s, F32)
    y_p, c_p, g_p, r_p = _layer(
        x_prompt, zeros(Bp, CONV_W - 1, GDN_CONV_CH), zeros(Bp, GDN_HEADS, GDN_DK, GDN_DK),
        zeros(Bp, HG_HEADS, HG_DK, HG_DK), params, consts,
        n_valid=None, in_tile=256, n_chunks=4, out_tile=512)

    xs = jnp.pad(x_sample, ((0, 0), (0, CHUNK - Ls), (0, 0)))
    y_s, c_s, g_s, r_s = _layer(
        xs, state_conv[0], state_gdn[0], state_hgrn[0], params, consts,
        n_valid=Ls, in_tile=CHUNK, n_chunks=1, out_tile=CHUNK)
    y_s = y_s[:, :Ls]

    return (y_p, y_s, c_p[None], g_p[None], r_p[None], c_s[None], g_s[None], r_s[None])
```

```python
import functools

import numpy as np
import jax
import jax.numpy as jnp
from jax import lax
from jax.experimental import pallas as pl
from jax.experimental.pallas import tpu as pltpu

F32 = jnp.float32
BF16 = jnp.bfloat16

D_MODEL = 1024
CHUNK = 64
GDN_HEADS = 8
GDN_DK = 64
GDN_QK = GDN_HEADS * GDN_DK
GDN_CONV_CH = 3 * GDN_QK
CONV_W = 4
HG_HEADS = 4
HG_DK = 128
HG_QK = HG_HEADS * HG_DK
EPS = 1e-6

LANES = 128
SUBLANES = 8
MXU_DIM = 256
SLAB = 512
GATE_COL = GDN_CONV_CH + GDN_QK + 4 * HG_QK
N_PROJ = GATE_COL + LANES
GDN_HALF = MXU_DIM
GDN_HALVES = GDN_QK // GDN_HALF
VMEM_LIMIT = 48 * 1024 * 1024


def _sigmoid(x):
    return 1.0 / (1.0 + jnp.exp(-x))


def _split3(x):
    hi = x.astype(BF16)
    r = x - hi.astype(F32)
    mid = r.astype(BF16)
    lo = (r - mid.astype(F32)).astype(BF16)
    return hi, mid, lo


def _dot(a, b):
    return jnp.dot(a, b, preferred_element_type=F32)


def _dot_nt(a, b):
    return lax.dot_general(a, b, (((1,), (1,)), ((), ())), preferred_element_type=F32)


def _dot_tn(a, b):
    return lax.dot_general(a, b, (((0,), (0,)), ((), ())), preferred_element_type=F32)


def _sel_left(sel01, x):
    hi, mid, lo = _split3(x)
    return _dot(sel01, hi) + _dot(sel01, mid) + _dot(sel01, lo)


def _sel_right(x, sel01):
    hi, mid, lo = _split3(x)
    return _dot(hi, sel01) + _dot(mid, sel01) + _dot(lo, sel01)


def _inproj_kernel(x_ref, nw_ref, w_ref, cw_ref, cs_ref, proj_ref, nc_ref, ubuf, *, n_valid):
    t = pl.program_id(1)
    T = x_ref.shape[1]

    @pl.when(t == 0)
    def _():
        ubuf[0:SUBLANES, :] = cs_ref[0]

    x = x_ref[0]
    h = x * lax.rsqrt(jnp.mean(x * x, axis=-1, keepdims=True) + EPS) * nw_ref[...]
    hb = h.astype(BF16)

    for s in range(0, GDN_CONV_CH, SLAB):
        ubuf[SUBLANES:SUBLANES + T, s:s + SLAB] = _dot(hb, w_ref[:, s:s + SLAB])
    for s in range(GDN_CONV_CH, GATE_COL, SLAB):
        proj_ref[0, :, s:s + SLAB] = _dot(hb, w_ref[:, s:s + SLAB])
    proj_ref[0, :, GATE_COL:N_PROJ] = _dot(hb, w_ref[:, GATE_COL:N_PROJ])

    for s in range(0, GDN_CONV_CH, SLAB):
        y = cw_ref[CONV_W - 1:CONV_W, s:s + SLAB] * ubuf[SUBLANES:SUBLANES + T, s:s + SLAB]
        for j in range(1, CONV_W):
            y = y + (cw_ref[CONV_W - 1 - j:CONV_W - j, s:s + SLAB]
                     * ubuf[SUBLANES - j:SUBLANES - j + T, s:s + SLAB])
        proj_ref[0, :, s:s + SLAB] = y * _sigmoid(y)

    nc_ref[0] = ubuf[n_valid:n_valid + SUBLANES, :]
    ubuf[0:SUBLANES, :] = ubuf[T:T + SUBLANES, :]


def _inproj(x, norm_w, w_in_r, conv_w, conv_state8, *, tile, n_valid):
    B, L, _ = x.shape
    kern = functools.partial(_inproj_kernel, n_valid=n_valid)
    return pl.pallas_call(
        kern,
        grid=(B, L // tile),
        in_specs=[
            pl.BlockSpec((1, tile, D_MODEL), lambda b, t: (b, t, 0)),
            pl.BlockSpec((1, D_MODEL), lambda b, t: (0, 0)),
            pl.BlockSpec((D_MODEL, N_PROJ), lambda b, t: (0, 0)),
            pl.BlockSpec((CONV_W, GDN_CONV_CH), lambda b, t: (0, 0)),
            pl.BlockSpec((1, SUBLANES, GDN_CONV_CH), lambda b, t: (b, 0, 0)),
        ],
        out_specs=[
            pl.BlockSpec((1, tile, N_PROJ), lambda b, t: (b, t, 0)),
            pl.BlockSpec((1, SUBLANES, GDN_CONV_CH), lambda b, t: (b, 0, 0)),
        ],
        out_shape=[
            jax.ShapeDtypeStruct((B, L, N_PROJ), F32),
            jax.ShapeDtypeStruct((B, SUBLANES, GDN_CONV_CH), F32),
        ],
        scratch_shapes=[pltpu.VMEM((tile + SUBLANES, GDN_CONV_CH), F32)],
        compiler_params=pltpu.CompilerParams(
            dimension_semantics=("arbitrary", "arbitrary"), vmem_limit_bytes=VMEM_LIMIT),
        name="inproj_conv",
    )(x, norm_w, w_in_r, conv_w, conv_state8)


def _gdn_consts():
    ltri = np.tril(np.ones((CHUNK, CHUNK), np.float32))
    lane = np.arange(GDN_QK)
    eb = np.zeros((LANES, GDN_QK), np.float32)
    eb[lane // GDN_DK, lane] = 1.0
    eg = np.zeros((LANES, GDN_QK), np.float32)
    eg[GDN_HEADS + lane // GDN_DK, lane] = 1.0
    bd = np.kron(np.eye(GDN_HALF // GDN_DK, dtype=np.float32), np.ones((GDN_DK, GDN_DK), np.float32))
    return (jnp.asarray(ltri, BF16), jnp.asarray(eb, BF16), jnp.asarray(eg, BF16), jnp.asarray(bd, BF16))


def _gdn_kernel(q_ref, k_ref, v_ref, z_ref, g_ref, s0_ref, alog_ref, dtb_ref, nw_ref,
                ltri_ref, eb_ref, eg_ref, bd_ref, o_ref, s_ref, *, n_chunks, n_valid):
    c = CHUNK
    W = GDN_HALF
    t = pl.program_id(1)

    @pl.when(t == 0)
    def _():
        s_ref[...] = s0_ref[...]

    row = lax.broadcasted_iota(jnp.int32, (c, W), 0)
    col = lax.broadcasted_iota(jnp.int32, (c, W), 1) & (GDN_DK - 1)
    eye_t = col == row
    causal_t = col <= row
    strict_t = col < row
    eye_f = eye_t.astype(F32)
    grow_ = lax.broadcasted_iota(jnp.int32, (c, LANES), 0)
    glane = lax.broadcasted_iota(jnp.int32, (c, LANES), 1)
    bd_mask = ((lax.broadcasted_iota(jnp.int32, (W, W), 0) & -GDN_DK)
               == (lax.broadcasted_iota(jnp.int32, (W, W), 1) & -GDN_DK))
    bd01 = bd_ref[...]
    neg_a = -jnp.exp(alog_ref[...])
    ltri = ltri_ref[...]

    def block_diag(x_b):
        return jnp.concatenate([x_b] * (W // c), axis=0) * bd01

    chunks = range(n_chunks)
    items = [(ci, a) for ci in chunks for a in range(GDN_HALVES)]

    gbs = []
    for ci in chunks:
        r0 = ci * c
        gz = g_ref[0, r0:r0 + c, :]
        xg = gz + dtb_ref[...]
        softplus = jnp.maximum(xg, 0.0) + jnp.log(1.0 + jnp.exp(-jnp.abs(xg)))
        gb = jnp.where(glane < GDN_HEADS, _sigmoid(gz),
                       jnp.where(glane < 2 * GDN_HEADS, neg_a * softplus, 0.0))
        if n_valid is not None:
            gb = jnp.where(grow_ + r0 < n_valid, gb, 0.0)
        gbs.append(gb)
    gcums = [_sel_left(ltri, gb) for gb in gbs]
    bxs = [_sel_right(gb, eb_ref[...]) for gb in gbs]
    gws = [_sel_right(gc, eg_ref[...]) for gc in gcums]

    st = {}
    for (ci, a) in items:
        r0 = ci * c
        ls = slice(a * W, (a + 1) * W)
        qp = q_ref[0, r0:r0 + c, ls]
        kp = k_ref[0, r0:r0 + c, ls]
        ss = _dot(jnp.concatenate([qp * qp, kp * kp], axis=0).astype(BF16), bd01)
        st[(ci, a)] = dict(qp=qp, kp=kp, ss=ss)
    for (ci, a) in items:
        r0 = ci * c
        ls = slice(a * W, (a + 1) * W)
        qp, kp, ss = st[(ci, a)]["qp"], st[(ci, a)]["kp"], st[(ci, a)]["ss"]
        vp = v_ref[0, r0:r0 + c, ls]
        qn = qp * lax.rsqrt(ss[:c] + EPS) * (GDN_DK ** -0.5)
        kn = kp * lax.rsqrt(ss[c:] + EPS)
        bp = bxs[ci][:, ls]
        gp = gws[ci][:, ls]
        glast = gp[c - 1:c, :]
        e_g = jnp.exp(gp)
        kb = kn * bp
        gdiag = jnp.sum(jnp.where(eye_t, gp, 0.0), axis=0, keepdims=True)
        dec = jnp.where(causal_t, jnp.exp(jnp.minimum(gp - gdiag, 0.0)), 0.0)
        kq = _dot_nt(jnp.concatenate([kb, qn], axis=0).astype(BF16), block_diag(kn.astype(BF16)))
        a_neg = jnp.where(strict_t, -(kq[:c] * dec), 0.0)
        st[(ci, a)] = dict(
            a_neg=a_neg, attn=(kq[c:] * dec).astype(BF16), vb=(vp * bp).astype(BF16),
            kbg=(kb * e_g).astype(BF16), qg=qn * e_g, kdec=(kn * jnp.exp(glast - gp)).astype(BF16),
            eglast=jnp.exp(glast))

    for it in items:
        a_b = st[it]["a_neg"].astype(BF16)
        st[it]["pw"] = _dot(a_b, block_diag(a_b))
        st[it]["xk"] = eye_f + st[it]["a_neg"]
    span = 4
    while span < c:
        for it in items:
            pw, xk = st[it]["pw"], st[it]["xk"]
            both = _dot(jnp.concatenate([pw, xk], axis=0).astype(BF16), block_diag(pw.astype(BF16)))
            st[it]["pw"] = both[:c]
            st[it]["xk"] = xk + both[c:]
        span *= 2
    for it in items:
        pw, xk = st[it]["pw"], st[it]["xk"]
        st[it]["t_b"] = (xk + _dot(xk.astype(BF16), block_diag(pw.astype(BF16)))).astype(BF16)
    for it in items:
        d = st[it]
        d["u_b"] = _dot(d["t_b"], block_diag(d["vb"])).astype(BF16)
        d["w_b"] = _dot(d["t_b"], block_diag(d["kbg"])).astype(BF16)

    for it in items:
        d = st[it]
        d["q_eff"] = (d["qg"] - _dot(d["attn"], block_diag(d["w_b"]))).astype(BF16)
        d["o_loc"] = _dot(d["attn"], block_diag(d["u_b"]))
        d["n_t"] = jnp.where(bd_mask, _dot_tn(d["u_b"], d["kdec"]), 0.0)
        d["m_t"] = jnp.where(bd_mask, -_dot_tn(d["w_b"], d["kdec"]), 0.0).astype(BF16)

    for ci in chunks:
        for a in range(GDN_HALVES):
            d = st[(ci, a)]
            s_old = s_ref[0, a]
            s_b = s_old.astype(BF16)
            d["o"] = _dot_nt(d["q_eff"], s_b) + d["o_loc"]
            s_ref[0, a] = s_old * d["eglast"] + d["n_t"] + _dot(s_b, d["m_t"])

    for it in items:
        st[it]["ms"] = _dot((st[it]["o"] * st[it]["o"]).astype(BF16), bd01) * (1.0 / GDN_DK)
    for (ci, a) in items:
        r0 = ci * c
        ls = slice(a * W, (a + 1) * W)
        d = st[(ci, a)]
        zp = z_ref[0, r0:r0 + c, ls]
        o_ref[0, r0:r0 + c, ls] = d["o"] * lax.rsqrt(d["ms"] + EPS) * nw_ref[...] * (zp * _sigmoid(zp))


def _gdn(proj, s0_bd, alog_row, dtb_row, nw_row, consts, *, n_chunks, n_valid):
    B, L, _ = proj.shape
    tile = n_chunks * CHUNK
    ltri, eb, eg, bd = consts
    kern = functools.partial(_gdn_kernel, n_chunks=n_chunks, n_valid=n_valid)
    full = lambda shape: pl.BlockSpec(shape, lambda b, t: (0,) * len(shape))
    col = lambda j: pl.BlockSpec((1, tile, GDN_QK), lambda b, t: (b, t, j))
    state = pl.BlockSpec((1, GDN_HALVES, GDN_HALF, GDN_HALF), lambda b, t: (b, 0, 0, 0))
    return pl.pallas_call(
        kern,
        grid=(B, L // tile),
        in_specs=[
            col(0), col(1), col(2), col(3),
            pl.BlockSpec((1, tile, LANES), lambda b, t: (b, t, GATE_COL // LANES)),
            state,
            full((1, LANES)), full((1, LANES)), full((1, GDN_HALF)),
            full(ltri.shape), full(eb.shape), full(eg.shape), full(bd.shape),
        ],
        out_specs=[pl.BlockSpec((1, tile, GDN_QK), lambda b, t: (b, t, 0)), state],
        out_shape=[
            jax.ShapeDtypeStruct((B, L, GDN_QK), F32),
            jax.ShapeDtypeStruct((B, GDN_HALVES, GDN_HALF, GDN_HALF), F32),
        ],
        compiler_params=pltpu.CompilerParams(
            dimension_semantics=("arbitrary", "arbitrary"), vmem_limit_bytes=VMEM_LIMIT),
        name="gdn_chunks",
    )(proj, proj, proj, proj, proj, s0_bd, alog_row, dtb_row, nw_row, ltri, eb, eg, bd)


def _hgrn_levels():
    out = []
    m = 1
    while m < CHUNK:
        out.append(m)
        m *= 2
    return out


def _hgrn_consts():
    return jnp.asarray(np.tril(np.ones((CHUNK, CHUNK), np.float32)), BF16)


def _level_ref_rows(b, m):
    c = b.shape[0]
    bcast = lambda r, n: jnp.broadcast_to(b[r:r + 1, :], (n, LANES))
    if 2 * m >= SUBLANES:
        parts = [bcast(blk * 2 * m + m, 2 * m) for blk in range(c // (2 * m))]
    else:
        sub = lax.broadcasted_iota(jnp.int32, (SUBLANES, LANES), 0)
        parts = []
        for g in range(c // SUBLANES):
            acc = bcast(g * SUBLANES + m, SUBLANES)
            for blk in range(1, SUBLANES // (2 * m)):
                acc = jnp.where(sub >= blk * 2 * m, bcast(g * SUBLANES + blk * 2 * m + m, SUBLANES), acc)
            parts.append(acc)
    return parts[0] if len(parts) == 1 else jnp.concatenate(parts, axis=0)


def _hgrn_kernel(q_ref, f_ref, i_ref, z_ref, s0_ref, lb_ref, nw_ref, ltri_ref, o_ref, s_ref,
                 *, n_chunks, n_valid):
    c = CHUNK
    t = pl.program_id(1)

    @pl.when(t == 0)
    def _():
        s_ref[...] = s0_ref[...]

    row = lax.broadcasted_iota(jnp.int32, (c, LANES), 0)
    ri = lax.broadcasted_iota(jnp.int32, (c, c), 0)
    ci_ = lax.broadcasted_iota(jnp.int32, (c, c), 1)
    levels = _hgrn_levels()
    ltri = ltri_ref[...]
    lb = lb_ref[...]
    eye_f = (ri == ci_).astype(F32)
    upper_f = [((row & m) != 0).astype(F32) for m in levels]
    same_f = [((ri & -(2 * m)) == (ci_ & -(2 * m))).astype(F32) for m in levels]

    kks, bs = [], []
    for ci in range(n_chunks):
        r0 = ci * c
        f = lb + (1.0 - lb) * _sigmoid(f_ref[0, r0:r0 + c, :])
        logf = jnp.log(f)
        kk = 1.0 - f
        if n_valid is not None:
            valid = lax.broadcasted_iota(jnp.int32, (c, HG_QK), 0) + r0 < n_valid
            logf = jnp.where(valid, logf, 0.0)
            kk = jnp.where(valid, kk, 0.0)
        kks.append(kk)
        bs.append(_sel_left(ltri, logf))

    st = {}
    for ci in range(n_chunks):
        r0 = ci * c
        for h in range(HG_HEADS):
            ls = slice(h * LANES, (h + 1) * LANES)
            hq = q_ref[0, r0:r0 + c, ls]
            q = hq * _sigmoid(hq)
            k = kks[ci][:, ls]
            v = i_ref[0, r0:r0 + c, ls]
            if n_valid is not None:
                v = jnp.where(row + r0 < n_valid, v, 0.0)
            b = bs[ci][:, ls]
            attn = eye_f * _dot_nt(q.astype(BF16), k.astype(BF16))
            for li, m in enumerate(levels):
                e = jnp.exp(-jnp.abs(b - _level_ref_rows(b, m)))
                e_up = e * upper_f[li]
                qt = (q * e_up).astype(BF16)
                kt = (k * (e - e_up)).astype(BF16)
                attn = attn + same_f[li] * _dot_nt(qt, kt)
            blast = b[c - 1:c, :]
            st[(ci, h)] = dict(
                qe=(q * jnp.exp(b)).astype(BF16), attn=attn.astype(BF16), v_b=v.astype(BF16),
                kdec=(k * jnp.exp(blast - b)).astype(BF16), eblast=jnp.exp(blast))
    for d in st.values():
        d["av"] = _dot(d["attn"], d["v_b"])
        d["upd"] = _dot_tn(d["v_b"], d["kdec"])

    for ci in range(n_chunks):
        r0 = ci * c
        for h in range(HG_HEADS):
            d = st[(ci, h)]
            ls = slice(h * LANES, (h + 1) * LANES)
            s_old = s_ref[0, h]
            o = _dot_nt(d["qe"], s_old.astype(BF16)) + d["av"]
            s_ref[0, h] = s_old * d["eblast"] + d["upd"]

            ms = jnp.mean(o * o, axis=-1, keepdims=True)
            zp = z_ref[0, r0:r0 + c, ls]
            o_ref[0, r0:r0 + c, ls] = o * lax.rsqrt(ms + EPS) * nw_ref[...] * (zp * _sigmoid(zp))


def _hgrn(proj, s0_t, lb_row, nw_row, wsel, *, n_chunks, n_valid):
    B, L, _ = proj.shape
    tile = n_chunks * CHUNK
    kern = functools.partial(_hgrn_kernel, n_chunks=n_chunks, n_valid=n_valid)
    full = lambda shape: pl.BlockSpec(shape, lambda b, t: (0,) * len(shape))
    col = lambda j: pl.BlockSpec((1, tile, HG_QK), lambda b, t: (b, t, j))
    return pl.pallas_call(
        kern,
        grid=(B, L // tile),
        in_specs=[
            col(4), col(5), col(6), col(7),
            pl.BlockSpec((1, HG_HEADS, HG_DK, HG_DK), lambda b, t: (b, 0, 0, 0)),
            full((1, HG_QK)), full((1, HG_DK)), full(wsel.shape),
        ],
        out_specs=[
            pl.BlockSpec((1, tile, HG_QK), lambda b, t: (b, t, 0)),
            pl.BlockSpec((1, HG_HEADS, HG_DK, HG_DK), lambda b, t: (b, 0, 0, 0)),
        ],
        out_shape=[
            jax.ShapeDtypeStruct((B, L, HG_QK), F32),
            jax.ShapeDtypeStruct((B, HG_HEADS, HG_DK, HG_DK), F32),
        ],
        compiler_params=pltpu.CompilerParams(
            dimension_semantics=("arbitrary", "arbitrary"), vmem_limit_bytes=VMEM_LIMIT),
        name="hgrn_chunks",
    )(proj, proj, proj, proj, s0_t, lb_row, nw_row, wsel)


def _outproj_kernel(x_ref, oa_ref, ob_ref, w_ref, fw_ref, y_ref):
    half = oa_ref.shape[2]
    y = (x_ref[0]
         + _dot(oa_ref[0].astype(BF16), w_ref[0:half, :])
         + _dot(ob_ref[0].astype(BF16), w_ref[half:, :]))
    y_ref[0] = y * lax.rsqrt(jnp.mean(y * y, axis=-1, keepdims=True) + EPS) * fw_ref[...]


def _outproj(x, oa, ob, w_out_b, fw_row, *, tile):
    B, L, _ = x.shape
    return pl.pallas_call(
        _outproj_kernel,
        grid=(B, L // tile),
        in_specs=[
            pl.BlockSpec((1, tile, D_MODEL), lambda b, t: (b, t, 0)),
            pl.BlockSpec((1, tile, GDN_QK), lambda b, t: (b, t, 0)),
            pl.BlockSpec((1, tile, HG_QK), lambda b, t: (b, t, 0)),
            pl.BlockSpec((D_MODEL, D_MODEL), lambda b, t: (0, 0)),
            pl.BlockSpec((1, D_MODEL), lambda b, t: (0, 0)),
        ],
        out_specs=pl.BlockSpec((1, tile, D_MODEL), lambda b, t: (b, t, 0)),
        out_shape=jax.ShapeDtypeStruct((B, L, D_MODEL), F32),
        compiler_params=pltpu.CompilerParams(
            dimension_semantics=("arbitrary", "arbitrary"), vmem_limit_bytes=VMEM_LIMIT),
        name="outproj_norm",
    )(x, oa, ob, w_out_b, fw_row)


def _layer(x, conv_state, s_gdn, s_hg, params, consts, *, n_valid, in_tile, n_chunks, out_tile):
    (norm_w, w_in_r, conv_w, alog_row, dtb_row, gdn_nw_row, lb_row, hg_nw_row, w_out_b, fw_row) = params
    gdn_consts, hg_wsel = consts
    B, L, _ = x.shape
    cs8 = jnp.pad(conv_state, ((0, 0), (SUBLANES - (CONV_W - 1), 0), (0, 0)))
    proj, nc8 = _inproj(x, norm_w, w_in_r, conv_w, cs8, tile=in_tile,
                        n_valid=in_tile if n_valid is None else n_valid)
    new_conv = nc8[:, SUBLANES - (CONV_W - 1):, :]

    hph = GDN_HALF // GDN_DK
    sg = s_gdn.reshape(B, GDN_HALVES, hph, GDN_DK, GDN_DK)
    s0_bd = jnp.einsum("bahde,hg->bahegd", sg, jnp.eye(hph, dtype=F32)).reshape(B, GDN_HALVES, GDN_HALF, GDN_HALF)
    oa, s_bd = _gdn(proj, s0_bd, alog_row, dtb_row, gdn_nw_row, gdn_consts,
                    n_chunks=n_chunks, n_valid=n_valid)
    s_bd = s_bd.reshape(B, GDN_HALVES, hph, GDN_DK, hph, GDN_DK)
    new_gdn = jnp.stack([s_bd[:, :, h, :, h, :] for h in range(hph)], axis=2)
    new_gdn = jnp.swapaxes(new_gdn, -1, -2).reshape(B, GDN_HEADS, GDN_DK, GDN_DK)

    ob, s_t = _hgrn(proj, jnp.swapaxes(s_hg, -1, -2), lb_row, hg_nw_row, hg_wsel,
                    n_chunks=n_chunks, n_valid=n_valid)
    new_hg = jnp.swapaxes(s_t, -1, -2)

    y = _outproj(x, oa, ob, w_out_b, fw_row, tile=out_tile)
    return y, new_conv, new_gdn, new_hg


def _prep(norm_w, w_in, conv_w, gdn_A_log, gdn_dt_bias, gdn_norm_w, hgrn_lb_logits, hgrn_norm_w, w_out,
          final_norm_w):
    w = w_in[0]
    o_qkv, o_za = 0, GDN_CONV_CH
    o_b = o_za + GDN_QK
    o_a = o_b + GDN_HEADS
    o_hq = o_a + GDN_HEADS
    gate_cols = jnp.pad(w[:, o_b:o_hq], ((0, 0), (0, LANES - 2 * GDN_HEADS)))
    w_in_r = jnp.concatenate([w[:, o_qkv:o_b], w[:, o_hq:], gate_cols], axis=1).astype(BF16)

    pad_row = lambda v: jnp.pad(v.astype(F32), (GDN_HEADS, LANES - 2 * GDN_HEADS))[None, :]
    lb = jnp.cumsum(jax.nn.softmax(hgrn_lb_logits.astype(F32), axis=0), axis=0)[0]
    params = (norm_w[0][None, :], w_in_r, conv_w[0], pad_row(gdn_A_log[0]), pad_row(gdn_dt_bias[0]),
              jnp.tile(gdn_norm_w[0], GDN_HALF // GDN_DK)[None, :], lb[None, :], hgrn_norm_w[0][None, :],
              w_out[0].astype(BF16), final_norm_w[None, :])
    consts = (_gdn_consts(), _hgrn_consts())
    return params, consts


def kernel(x_prompt, x_sample, state_conv, state_gdn, state_hgrn, norm_w, w_in, conv_w, gdn_A_log,
           gdn_dt_bias, gdn_norm_w, hgrn_lb_logits, hgrn_norm_w, w_out, final_norm_w):
    Bp, Lp, _ = x_prompt.shape
    Bs, Ls, _ = x_sample.shape
    params, consts = _prep(norm_w, w_in, conv_w, gdn_A_log, gdn_dt_bias, gdn_norm_w, hgrn_lb_logits,
                           hgrn_norm_w, w_out, final_norm_w)

    zeros = lambda *s: jnp.zeros(s, F32)
    y_p, c_p, g_p, r_p = _layer(
        x_prompt, zeros(Bp, CONV_W - 1, GDN_CONV_CH), zeros(Bp, GDN_HEADS, GDN_DK, GDN_DK),
        zeros(Bp, HG_HEADS, HG_DK, HG_DK), params, consts,
        n_valid=None, in_tile=256, n_chunks=4, out_tile=512)

    xs = jnp.pad(x_sample, ((0, 0), (0, CHUNK - Ls), (0, 0)))
    y_s, c_s, g_s, r_s = _layer(
        xs, state_conv[0], state_gdn[0], state_hgrn[0], params, consts,
        n_valid=Ls, in_tile=CHUNK, n_chunks=1, out_tile=CHUNK)
    y_s = y_s[:, :Ls]

    return (y_p, y_s, c_p[None], g_p[None], r_p[None], c_s[None], g_s[None], r_s[None])
```

```python
import functools

import numpy as np
import jax
import jax.numpy as jnp
from jax import lax
from jax.experimental import pallas as pl
from jax.experimental.pallas import tpu as pltpu

F32 = jnp.float32
BF16 = jnp.bfloat16

D_MODEL = 1024
CHUNK = 64
GDN_HEADS = 8
GDN_DK = 64
GDN_QK = GDN_HEADS * GDN_DK
GDN_CONV_CH = 3 * GDN_QK
CONV_W = 4
HG_HEADS = 4
HG_DK = 128
HG_QK = HG_HEADS * HG_DK
EPS = 1e-6

LANES = 128
SUBLANES = 8
MXU_DIM = 256
SLAB = 512
OUT_ROWS = 128
GATE_COL = GDN_CONV_CH + GDN_QK + 4 * HG_QK
N_PROJ = GATE_COL + LANES
GDN_HALF = MXU_DIM
GDN_HALVES = GDN_QK // GDN_HALF
VMEM_LIMIT = 48 * 1024 * 1024


def _sigmoid(x):
    return 1.0 / (1.0 + jnp.exp(-x))


def _split3(x):
    hi = x.astype(BF16)
    r = x - hi.astype(F32)
    mid = r.astype(BF16)
    lo = (r - mid.astype(F32)).astype(BF16)
    return hi, mid, lo


def _dot(a, b):
    return jnp.dot(a, b, preferred_element_type=F32)


def _dot_nt(a, b):
    return lax.dot_general(a, b, (((1,), (1,)), ((), ())), preferred_element_type=F32)


def _dot_tn(a, b):
    return lax.dot_general(a, b, (((0,), (0,)), ((), ())), preferred_element_type=F32)


def _sel_left(sel01, x):
    hi, mid, lo = _split3(x)
    return _dot(sel01, hi) + _dot(sel01, mid) + _dot(sel01, lo)


def _sel_right(x, sel01):
    hi, mid, lo = _split3(x)
    return _dot(hi, sel01) + _dot(mid, sel01) + _dot(lo, sel01)


def _inproj_kernel(x_ref, nw_ref, w_ref, cw_ref, cs_ref, proj_ref, nc_ref, carry, *, n_valid):
    t = pl.program_id(1)
    T = x_ref.shape[1]

    @pl.when(t == 0)
    def _():
        carry[...] = cs_ref[0]

    x = x_ref[0]
    h = x * lax.rsqrt(jnp.mean(x * x, axis=-1, keepdims=True) + EPS) * nw_ref[...]
    hb = h.astype(BF16)

    assert CONV_W == 4
    for s in range(0, GDN_CONV_CH, SLAB):
        sl = slice(s, s + SLAB)
        u = _dot(hb, w_ref[:, sl])
        full = jnp.concatenate([carry[:, sl], u], axis=0)
        prev = pltpu.roll(full, 1, 0)
        pair = cw_ref[1:2, sl] * full + cw_ref[0:1, sl] * prev
        y = (cw_ref[3:4, sl] * full + cw_ref[2:3, sl] * prev + pltpu.roll(pair, 2, 0))[SUBLANES:]
        proj_ref[0, :, sl] = y * _sigmoid(y)
        nc_ref[0, :, sl] = full[n_valid:n_valid + SUBLANES]
        carry[:, sl] = full[T:T + SUBLANES]
    for s in range(GDN_CONV_CH, GATE_COL, SLAB):
        proj_ref[0, :, s:s + SLAB] = _dot(hb, w_ref[:, s:s + SLAB])
    proj_ref[0, :, GATE_COL:N_PROJ] = _dot(hb, w_ref[:, GATE_COL:N_PROJ])


def _inproj(x, norm_w, w_in_r, conv_w, conv_state8, *, tile, n_valid):
    B, L, _ = x.shape
    kern = functools.partial(_inproj_kernel, n_valid=n_valid)
    return pl.pallas_call(
        kern,
        grid=(B, L // tile),
        in_specs=[
            pl.BlockSpec((1, tile, D_MODEL), lambda b, t: (b, t, 0)),
            pl.BlockSpec((1, D_MODEL), lambda b, t: (0, 0)),
            pl.BlockSpec((D_MODEL, N_PROJ), lambda b, t: (0, 0)),
            pl.BlockSpec((CONV_W, GDN_CONV_CH), lambda b, t: (0, 0)),
            pl.BlockSpec((1, SUBLANES, GDN_CONV_CH), lambda b, t: (b, 0, 0)),
        ],
        out_specs=[
            pl.BlockSpec((1, tile, N_PROJ), lambda b, t: (b, t, 0)),
            pl.BlockSpec((1, SUBLANES, GDN_CONV_CH), lambda b, t: (b, 0, 0)),
        ],
        out_shape=[
            jax.ShapeDtypeStruct((B, L, N_PROJ), F32),
            jax.ShapeDtypeStruct((B, SUBLANES, GDN_CONV_CH), F32),
        ],
        scratch_shapes=[pltpu.VMEM((SUBLANES, GDN_CONV_CH), F32)],
        compiler_params=pltpu.CompilerParams(
            dimension_semantics=("arbitrary", "arbitrary"), vmem_limit_bytes=VMEM_LIMIT),
        name="inproj_conv",
    )(x, norm_w, w_in_r, conv_w, conv_state8)


def _gdn_consts():
    ltri = np.tril(np.ones((CHUNK, CHUNK), np.float32))
    lane = np.arange(GDN_QK)
    eb = np.zeros((LANES, GDN_QK), np.float32)
    eb[lane // GDN_DK, lane] = 1.0
    eg = np.zeros((LANES, GDN_QK), np.float32)
    eg[GDN_HEADS + lane // GDN_DK, lane] = 1.0
    bd = np.kron(np.eye(GDN_HALF // GDN_DK, dtype=np.float32), np.ones((GDN_DK, GDN_DK), np.float32))
    return (jnp.asarray(ltri, BF16), jnp.asarray(eb, BF16), jnp.asarray(eg, BF16), jnp.asarray(bd, BF16))


def _gdn_kernel(q_ref, k_ref, v_ref, z_ref, g_ref, s0_ref, alog_ref, dtb_ref, nw_ref,
                ltri_ref, eb_ref, eg_ref, bd_ref, o_ref, s_ref, *, n_chunks, n_valid):
    c = CHUNK
    W = GDN_HALF
    t = pl.program_id(1)

    @pl.when(t == 0)
    def _():
        s_ref[...] = s0_ref[...]

    row = lax.broadcasted_iota(jnp.int32, (c, W), 0)
    col = lax.broadcasted_iota(jnp.int32, (c, W), 1) & (GDN_DK - 1)
    eye_t = col == row
    causal_t = col <= row
    strict_t = col < row
    eye_f = eye_t.astype(F32)
    grow_ = lax.broadcasted_iota(jnp.int32, (c, LANES), 0)
    glane = lax.broadcasted_iota(jnp.int32, (c, LANES), 1)
    bd_mask = ((lax.broadcasted_iota(jnp.int32, (W, W), 0) & -GDN_DK)
               == (lax.broadcasted_iota(jnp.int32, (W, W), 1) & -GDN_DK))
    bd01 = bd_ref[...]
    neg_a = -jnp.exp(alog_ref[...])
    ltri = ltri_ref[...]

    def block_diag(x_b):
        return jnp.concatenate([x_b] * (W // c), axis=0) * bd01

    chunks = range(n_chunks)
    items = [(ci, a) for ci in chunks for a in range(GDN_HALVES)]

    gbs = []
    for ci in chunks:
        r0 = ci * c
        gz = g_ref[0, r0:r0 + c, :]
        xg = gz + dtb_ref[...]
        softplus = jnp.maximum(xg, 0.0) + jnp.log(1.0 + jnp.exp(-jnp.abs(xg)))
        gb = jnp.where(glane < GDN_HEADS, _sigmoid(gz),
                       jnp.where(glane < 2 * GDN_HEADS, neg_a * softplus, 0.0))
        if n_valid is not None:
            gb = jnp.where(grow_ + r0 < n_valid, gb, 0.0)
        gbs.append(gb)
    gcums = [_sel_left(ltri, gb) for gb in gbs]
    bxs = [_sel_right(gb, eb_ref[...]) for gb in gbs]
    gws = [_sel_right(gc, eg_ref[...]) for gc in gcums]

    st = {}
    for (ci, a) in items:
        r0 = ci * c
        ls = slice(a * W, (a + 1) * W)
        qp = q_ref[0, r0:r0 + c, ls]
        kp = k_ref[0, r0:r0 + c, ls]
        ss = _dot(jnp.concatenate([qp * qp, kp * kp], axis=0).astype(BF16), bd01)
        st[(ci, a)] = dict(qp=qp, kp=kp, ss=ss)
    for (ci, a) in items:
        r0 = ci * c
        ls = slice(a * W, (a + 1) * W)
        qp, kp, ss = st[(ci, a)]["qp"], st[(ci, a)]["kp"], st[(ci, a)]["ss"]
        vp = v_ref[0, r0:r0 + c, ls]
        qn = qp * lax.rsqrt(ss[:c] + EPS) * (GDN_DK ** -0.5)
        kn = kp * lax.rsqrt(ss[c:] + EPS)
        bp = bxs[ci][:, ls]
        gp = gws[ci][:, ls]
        glast = gp[c - 1:c, :]
        e_g = jnp.exp(gp)
        kb = kn * bp
        gdiag = jnp.sum(jnp.where(eye_t, gp, 0.0), axis=0, keepdims=True)
        dec = jnp.where(causal_t, jnp.exp(jnp.minimum(gp - gdiag, 0.0)), 0.0)
        kq = _dot_nt(jnp.concatenate([kb, qn], axis=0).astype(BF16), block_diag(kn.astype(BF16)))
        a_neg = jnp.where(strict_t, -(kq[:c] * dec), 0.0)
        st[(ci, a)] = dict(
            a_neg=a_neg, attn=(kq[c:] * dec).astype(BF16), vb=(vp * bp).astype(BF16),
            kbg=(kb * e_g).astype(BF16), qg=qn * e_g, kdec=(kn * jnp.exp(glast - gp)).astype(BF16),
            eglast=jnp.exp(glast))

    for it in items:
        a_b = st[it]["a_neg"].astype(BF16)
        st[it]["pw"] = _dot(a_b, block_diag(a_b))
        st[it]["xk"] = eye_f + st[it]["a_neg"]
    span = 4
    while span < c:
        for it in items:
            pw, xk = st[it]["pw"], st[it]["xk"]
            both = _dot(jnp.concatenate([pw, xk], axis=0).astype(BF16), block_diag(pw.astype(BF16)))
            st[it]["pw"] = both[:c]
            st[it]["xk"] = xk + both[c:]
        span *= 2
    for it in items:
        pw, xk = st[it]["pw"], st[it]["xk"]
        st[it]["t_b"] = (xk + _dot(xk.astype(BF16), block_diag(pw.astype(BF16)))).astype(BF16)
    for it in items:
        d = st[it]
        d["u_b"] = _dot(d["t_b"], block_diag(d["vb"])).astype(BF16)
        d["w_b"] = _dot(d["t_b"], block_diag(d["kbg"])).astype(BF16)

    for it in items:
        d = st[it]
        d["q_eff"] = (d["qg"] - _dot(d["attn"], block_diag(d["w_b"]))).astype(BF16)
        d["o_loc"] = _dot(d["attn"], block_diag(d["u_b"]))
        d["n_t"] = jnp.where(bd_mask, _dot_tn(d["u_b"], d["kdec"]), 0.0)
        d["m_t"] = jnp.where(bd_mask, -_dot_tn(d["w_b"], d["kdec"]), 0.0).astype(BF16)

    for ci in chunks:
        for a in range(GDN_HALVES):
            d = st[(ci, a)]
            s_old = s_ref[0, a]
            s_b = s_old.astype(BF16)
            d["o"] = _dot_nt(d["q_eff"], s_b) + d["o_loc"]
            s_ref[0, a] = s_old * d["eglast"] + d["n_t"] + _dot(s_b, d["m_t"])

    for it in items:
        st[it]["ms"] = _dot((st[it]["o"] * st[it]["o"]).astype(BF16), bd01) * (1.0 / GDN_DK)
    for (ci, a) in items:
        r0 = ci * c
        ls = slice(a * W, (a + 1) * W)
        d = st[(ci, a)]
        zp = z_ref[0, r0:r0 + c, ls]
        o_ref[0, r0:r0 + c, ls] = d["o"] * lax.rsqrt(d["ms"] + EPS) * nw_ref[...] * (zp * _sigmoid(zp))


def _gdn(proj, s0_bd, alog_row, dtb_row, nw_row, consts, *, n_chunks, n_valid):
    B, L, _ = proj.shape
    tile = n_chunks * CHUNK
    ltri, eb, eg, bd = consts
    kern = functools.partial(_gdn_kernel, n_chunks=n_chunks, n_valid=n_valid)
    full = lambda shape: pl.BlockSpec(shape, lambda b, t: (0,) * len(shape))
    col = lambda j: pl.BlockSpec((1, tile, GDN_QK), lambda b, t: (b, t, j))
    state = pl.BlockSpec((1, GDN_HALVES, GDN_HALF, GDN_HALF), lambda b, t: (b, 0, 0, 0))
    return pl.pallas_call(
        kern,
        grid=(B, L // tile),
        in_specs=[
            col(0), col(1), col(2), col(3),
            pl.BlockSpec((1, tile, LANES), lambda b, t: (b, t, GATE_COL // LANES)),
            state,
            full((1, LANES)), full((1, LANES)), full((1, GDN_HALF)),
            full(ltri.shape), full(eb.shape), full(eg.shape), full(bd.shape),
        ],
        out_specs=[pl.BlockSpec((1, tile, GDN_QK), lambda b, t: (b, t, 0)), state],
        out_shape=[
            jax.ShapeDtypeStruct((B, L, GDN_QK), F32),
            jax.ShapeDtypeStruct((B, GDN_HALVES, GDN_HALF, GDN_HALF), F32),
        ],
        compiler_params=pltpu.CompilerParams(
            dimension_semantics=("arbitrary", "arbitrary"), vmem_limit_bytes=VMEM_LIMIT),
        name="gdn_chunks",
    )(proj, proj, proj, proj, proj, s0_bd, alog_row, dtb_row, nw_row, ltri, eb, eg, bd)


def _hgrn_levels():
    out = []
    m = 1
    while m < CHUNK:
        out.append(m)
        m *= 2
    return out


def _hgrn_consts():
    return jnp.asarray(np.tril(np.ones((CHUNK, CHUNK), np.float32)), BF16)


def _level_ref_rows(b, m):
    c = b.shape[0]
    bcast = lambda r, n: jnp.broadcast_to(b[r:r + 1, :], (n, LANES))
    if 2 * m >= SUBLANES:
        parts = [bcast(blk * 2 * m + m, 2 * m) for blk in range(c // (2 * m))]
    else:
        sub = lax.broadcasted_iota(jnp.int32, (SUBLANES, LANES), 0)
        parts = []
        for g in range(c // SUBLANES):
            acc = bcast(g * SUBLANES + m, SUBLANES)
            for blk in range(1, SUBLANES // (2 * m)):
                acc = jnp.where(sub >= blk * 2 * m, bcast(g * SUBLANES + blk * 2 * m + m, SUBLANES), acc)
            parts.append(acc)
    return parts[0] if len(parts) == 1 else jnp.concatenate(parts, axis=0)


def _hgrn_kernel(q_ref, f_ref, i_ref, z_ref, s0_ref, lb_ref, nw_ref, ltri_ref, o_ref, s_ref,
                 *, n_chunks, n_valid):
    c = CHUNK
    t = pl.program_id(1)

    @pl.when(t == 0)
    def _():
        s_ref[...] = s0_ref[...]

    row = lax.broadcasted_iota(jnp.int32, (c, LANES), 0)
    ri = lax.broadcasted_iota(jnp.int32, (c, c), 0)
    ci_ = lax.broadcasted_iota(jnp.int32, (c, c), 1)
    levels = _hgrn_levels()
    ltri = ltri_ref[...]
    lb = lb_ref[...]
    eye_f = (ri == ci_).astype(F32)
    upper_f = [((row & m) != 0).astype(F32) for m in levels]
    same_f = [((ri & -(2 * m)) == (ci_ & -(2 * m))).astype(F32) for m in levels]

    kks, bs = [], []
    for ci in range(n_chunks):
        r0 = ci * c
        f = lb + (1.0 - lb) * _sigmoid(f_ref[0, r0:r0 + c, :])
        logf = jnp.log(f)
        kk = 1.0 - f
        if n_valid is not None:
            valid = lax.broadcasted_iota(jnp.int32, (c, HG_QK), 0) + r0 < n_valid
            logf = jnp.where(valid, logf, 0.0)
            kk = jnp.where(valid, kk, 0.0)
        kks.append(kk)
        bs.append(_sel_left(ltri, logf))

    st = {}
    for ci in range(n_chunks):
        r0 = ci * c
        for h in range(HG_HEADS):
            ls = slice(h * LANES, (h + 1) * LANES)
            hq = q_ref[0, r0:r0 + c, ls]
            q = hq * _sigmoid(hq)
            k = kks[ci][:, ls]
            v = i_ref[0, r0:r0 + c, ls]
            if n_valid is not None:
                v = jnp.where(row + r0 < n_valid, v, 0.0)
            b = bs[ci][:, ls]
            attn = eye_f * _dot_nt(q.astype(BF16), k.astype(BF16))
            for li, m in enumerate(levels):
                e = jnp.exp(-jnp.abs(b - _level_ref_rows(b, m)))
                e_up = e * upper_f[li]
                qt = (q * e_up).astype(BF16)
                kt = (k * (e - e_up)).astype(BF16)
                attn = attn + same_f[li] * _dot_nt(qt, kt)
            blast = b[c - 1:c, :]
            st[(ci, h)] = dict(
                qe=(q * jnp.exp(b)).astype(BF16), attn=attn.astype(BF16), v_b=v.astype(BF16),
                kdec=(k * jnp.exp(blast - b)).astype(BF16), eblast=jnp.exp(blast))
    for d in st.values():
        d["av"] = _dot(d["attn"], d["v_b"])
        d["upd"] = _dot_tn(d["v_b"], d["kdec"])

    for ci in range(n_chunks):
        r0 = ci * c
        for h in range(HG_HEADS):
            d = st[(ci, h)]
            ls = slice(h * LANES, (h + 1) * LANES)
            s_old = s_ref[0, h]
            o = _dot_nt(d["qe"], s_old.astype(BF16)) + d["av"]
            s_ref[0, h] = s_old * d["eblast"] + d["upd"]

            ms = jnp.mean(o * o, axis=-1, keepdims=True)
            zp = z_ref[0, r0:r0 + c, ls]
            o_ref[0, r0:r0 + c, ls] = o * lax.rsqrt(ms + EPS) * nw_ref[...] * (zp * _sigmoid(zp))


def _hgrn(proj, s0_t, lb_row, nw_row, wsel, *, n_chunks, n_valid):
    B, L, _ = proj.shape
    tile = n_chunks * CHUNK
    kern = functools.partial(_hgrn_kernel, n_chunks=n_chunks, n_valid=n_valid)
    full = lambda shape: pl.BlockSpec(shape, lambda b, t: (0,) * len(shape))
    col = lambda j: pl.BlockSpec((1, tile, HG_QK), lambda b, t: (b, t, j))
    return pl.pallas_call(
        kern,
        grid=(B, L // tile),
        in_specs=[
            col(4), col(5), col(6), col(7),
            pl.BlockSpec((1, HG_HEADS, HG_DK, HG_DK), lambda b, t: (b, 0, 0, 0)),
            full((1, HG_QK)), full((1, HG_DK)), full(wsel.shape),
        ],
        out_specs=[
            pl.BlockSpec((1, tile, HG_QK), lambda b, t: (b, t, 0)),
            pl.BlockSpec((1, HG_HEADS, HG_DK, HG_DK), lambda b, t: (b, 0, 0, 0)),
        ],
        out_shape=[
            jax.ShapeDtypeStruct((B, L, HG_QK), F32),
            jax.ShapeDtypeStruct((B, HG_HEADS, HG_DK, HG_DK), F32),
        ],
        compiler_params=pltpu.CompilerParams(
            dimension_semantics=("arbitrary", "arbitrary"), vmem_limit_bytes=VMEM_LIMIT),
        name="hgrn_chunks",
    )(proj, proj, proj, proj, s0_t, lb_row, nw_row, wsel)


def _outproj_kernel(x_ref, oa_ref, ob_ref, w_ref, fw_ref, y_ref):
    T = x_ref.shape[1]
    rows = min(T, OUT_ROWS)
    for r in range(0, T, rows):
        o = jnp.concatenate([oa_ref[0, r:r + rows, :], ob_ref[0, r:r + rows, :]], axis=1).astype(BF16)
        y = x_ref[0, r:r + rows, :] + _dot(o, w_ref[...])
        y_ref[0, r:r + rows, :] = y * lax.rsqrt(jnp.mean(y * y, axis=-1, keepdims=True) + EPS) * fw_ref[...]


def _outproj(x, oa, ob, w_out_b, fw_row, *, tile):
    B, L, _ = x.shape
    return pl.pallas_call(
        _outproj_kernel,
        grid=(B, L // tile),
        in_specs=[
            pl.BlockSpec((1, tile, D_MODEL), lambda b, t: (b, t, 0)),
            pl.BlockSpec((1, tile, GDN_QK), lambda b, t: (b, t, 0)),
            pl.BlockSpec((1, tile, HG_QK), lambda b, t: (b, t, 0)),
            pl.BlockSpec((D_MODEL, D_MODEL), lambda b, t: (0, 0)),
            pl.BlockSpec((1, D_MODEL), lambda b, t: (0, 0)),
        ],
        out_specs=pl.BlockSpec((1, tile, D_MODEL), lambda b, t: (b, t, 0)),
        out_shape=jax.ShapeDtypeStruct((B, L, D_MODEL), F32),
        compiler_params=pltpu.CompilerParams(
            dimension_semantics=("arbitrary", "arbitrary"), vmem_limit_bytes=VMEM_LIMIT),
        name="outproj_norm",
    )(x, oa, ob, w_out_b, fw_row)


def _layer(x, conv_state, s_gdn, s_hg, params, consts, *, n_valid, in_tile, n_chunks, out_tile):
    (norm_w, w_in_r, conv_w, alog_row, dtb_row, gdn_nw_row, lb_row, hg_nw_row, w_out_b, fw_row) = params
    gdn_consts, hg_wsel = consts
    B, L, _ = x.shape
    cs8 = jnp.pad(conv_state, ((0, 0), (SUBLANES - (CONV_W - 1), 0), (0, 0)))
    proj, nc8 = _inproj(x, norm_w, w_in_r, conv_w, cs8, tile=in_tile,
                        n_valid=in_tile if n_valid is None else n_valid)
    new_conv = nc8[:, SUBLANES - (CONV_W - 1):, :]

    hph = GDN_HALF // GDN_DK
    sg = s_gdn.reshape(B, GDN_HALVES, hph, GDN_DK, GDN_DK)
    s0_bd = jnp.einsum("bahde,hg->bahegd", sg, jnp.eye(hph, dtype=F32)).reshape(B, GDN_HALVES, GDN_HALF, GDN_HALF)
    oa, s_bd = _gdn(proj, s0_bd, alog_row, dtb_row, gdn_nw_row, gdn_consts,
                    n_chunks=n_chunks, n_valid=n_valid)
    s_bd = s_bd.reshape(B, GDN_HALVES, hph, GDN_DK, hph, GDN_DK)
    new_gdn = jnp.stack([s_bd[:, :, h, :, h, :] for h in range(hph)], axis=2)
    new_gdn = jnp.swapaxes(new_gdn, -1, -2).reshape(B, GDN_HEADS, GDN_DK, GDN_DK)

    ob, s_t = _hgrn(proj, jnp.swapaxes(s_hg, -1, -2), lb_row, hg_nw_row, hg_wsel,
                    n_chunks=n_chunks, n_valid=n_valid)
    new_hg = jnp.swapaxes(s_t, -1, -2)

    y = _outproj(x, oa, ob, w_out_b, fw_row, tile=out_tile)
    return y, new_conv, new_gdn, new_hg


def _prep(norm_w, w_in, conv_w, gdn_A_log, gdn_dt_bias, gdn_norm_w, hgrn_lb_logits, hgrn_norm_w, w_out,
          final_norm_w):
    w = w_in[0]
    o_qkv, o_za = 0, GDN_CONV_CH
    o_b = o_za + GDN_QK
    o_a = o_b + GDN_HEADS
    o_hq = o_a + GDN_HEADS
    gate_cols = jnp.pad(w[:, o_b:o_hq], ((0, 0), (0, LANES - 2 * GDN_HEADS)))
    w_in_r = jnp.concatenate([w[:, o_qkv:o_b], w[:, o_hq:], gate_cols], axis=1).astype(BF16)

    pad_row = lambda v: jnp.pad(v.astype(F32), (GDN_HEADS, LANES - 2 * GDN_HEADS))[None, :]
    lb = jnp.cumsum(jax.nn.softmax(hgrn_lb_logits.astype(F32), axis=0), axis=0)[0]
    params = (norm_w[0][None, :], w_in_r, conv_w[0], pad_row(gdn_A_log[0]), pad_row(gdn_dt_bias[0]),
              jnp.tile(gdn_norm_w[0], GDN_HALF // GDN_DK)[None, :], lb[None, :], hgrn_norm_w[0][None, :],
              w_out[0].astype(BF16), final_norm_w[None, :])
    consts = (_gdn_consts(), _hgrn_consts())
    return params, consts


def kernel(x_prompt, x_sample, state_conv, state_gdn, state_hgrn, norm_w, w_in, conv_w, gdn_A_log,
           gdn_dt_bias, gdn_norm_w, hgrn_lb_logits, hgrn_norm_w, w_out, final_norm_w):
    Bp, Lp, _ = x_prompt.shape
    Bs, Ls, _ = x_sample.shape
    params, consts = _prep(norm_w, w_in, conv_w, gdn_A_log, gdn_dt_bias, gdn_norm_w, hgrn_lb_logits,
                           hgrn_norm_w, w_out, final_norm_w)

    zeros = lambda *s: jnp.zeros(s, F32)
    y_p, c_p, g_p, r_p = _layer(
        x_prompt, zeros(Bp, CONV_W - 1, GDN_CONV_CH), zeros(Bp, GDN_HEADS, GDN_DK, GDN_DK),
        zeros(Bp, HG_HEADS, HG_DK, HG_DK), params, consts,
        n_valid=None, in_tile=512, n_chunks=8, out_tile=512)

    xs = jnp.pad(x_sample, ((0, 0), (0, CHUNK - Ls), (0, 0)))
    y_s, c_s, g_s, r_s = _layer(
        xs, state_conv[0], state_gdn[0], state_hgrn[0], params, consts,
        n_valid=Ls, in_tile=CHUNK, n_chunks=1, out_tile=CHUNK)
    y_s = y_s[:, :Ls]

    return (y_p, y_s, c_p[None], g_p[None], r_p[None], c_s[None], g_s[None], r_s[None])
```

```python
import functools

import numpy as np
import jax
import jax.numpy as jnp
from jax import lax
from jax.experimental import pallas as pl
from jax.experimental.pallas import tpu as pltpu

F32 = jnp.float32
BF16 = jnp.bfloat16

D_MODEL = 1024
CHUNK = 64
GDN_HEADS = 8
GDN_DK = 64
GDN_QK = GDN_HEADS * GDN_DK
GDN_CONV_CH = 3 * GDN_QK
CONV_W = 4
HG_HEADS = 4
HG_DK = 128
HG_QK = HG_HEADS * HG_DK
EPS = 1e-6

LANES = 128
SUBLANES = 8
MXU_DIM = 256
SLAB = 512
OUT_ROWS = 128
GATE_COL = GDN_CONV_CH + GDN_QK + 4 * HG_QK
N_PROJ = GATE_COL + LANES
GDN_HALF = MXU_DIM
GDN_HALVES = GDN_QK // GDN_HALF
VMEM_LIMIT = 48 * 1024 * 1024


def _sigmoid(x):
    return 1.0 / (1.0 + jnp.exp(-x))


def _split3(x):
    hi = x.astype(BF16)
    r = x - hi.astype(F32)
    mid = r.astype(BF16)
    lo = (r - mid.astype(F32)).astype(BF16)
    return hi, mid, lo


def _dot(a, b):
    return jnp.dot(a, b, preferred_element_type=F32)


def _dot_nt(a, b):
    return lax.dot_general(a, b, (((1,), (1,)), ((), ())), preferred_element_type=F32)


def _dot_tn(a, b):
    return lax.dot_general(a, b, (((0,), (0,)), ((), ())), preferred_element_type=F32)


def _sel_left(sel01, x):
    hi, mid, lo = _split3(x)
    return _dot(sel01, hi) + _dot(sel01, mid) + _dot(sel01, lo)


def _sel_right(x, sel01):
    hi, mid, lo = _split3(x)
    return _dot(hi, sel01) + _dot(mid, sel01) + _dot(lo, sel01)


def _batch_spec(shape, index):
    return pl.BlockSpec(shape, lambda b, t: (b,) + tuple(index(b, t)))


def _const_spec(shape):
    return pl.BlockSpec(shape, lambda b, t: (0,) * len(shape))


def _inproj_kernel(x_ref, nw_ref, w_ref, cw_ref, cs_ref, proj_ref, nc_ref, carry, *, n_valid):
    t = pl.program_id(1)
    bb, T = x_ref.shape[0], x_ref.shape[1]

    @pl.when(t == 0)
    def _():
        carry[...] = cs_ref[...]

    x = x_ref[...].reshape(bb * T, D_MODEL)
    h = x * lax.rsqrt(jnp.mean(x * x, axis=-1, keepdims=True) + EPS) * nw_ref[...]
    hb = h.astype(BF16)

    assert CONV_W == 4
    for s in range(0, GDN_CONV_CH, SLAB):
        sl = slice(s, s + SLAB)
        u_all = _dot(hb, w_ref[:, sl])
        for bi in range(bb):
            u = u_all[bi * T:(bi + 1) * T]
            full = jnp.concatenate([carry[bi, :, sl], u], axis=0)
            prev = pltpu.roll(full, 1, 0)
            pair = cw_ref[1:2, sl] * full + cw_ref[0:1, sl] * prev
            y = (cw_ref[3:4, sl] * full + cw_ref[2:3, sl] * prev + pltpu.roll(pair, 2, 0))[SUBLANES:]
            proj_ref[bi, :, sl] = y * _sigmoid(y)
            nc_ref[bi, :, sl] = full[n_valid:n_valid + SUBLANES]
            carry[bi, :, sl] = full[T:T + SUBLANES]
    for s in list(range(GDN_CONV_CH, GATE_COL, SLAB)) + [GATE_COL]:
        sl = slice(s, min(s + SLAB, N_PROJ))
        res = _dot(hb, w_ref[:, sl])
        for bi in range(bb):
            proj_ref[bi, :, sl] = res[bi * T:(bi + 1) * T]


def _inproj(x, norm_w, w_in_r, conv_w, conv_state8, *, bb, tile, n_valid):
    B, L, _ = x.shape
    kern = functools.partial(_inproj_kernel, n_valid=n_valid)
    return pl.pallas_call(
        kern,
        grid=(B // bb, L // tile),
        in_specs=[
            _batch_spec((bb, tile, D_MODEL), lambda b, t: (t, 0)),
            _const_spec((1, D_MODEL)),
            _const_spec((D_MODEL, N_PROJ)),
            _const_spec((CONV_W, GDN_CONV_CH)),
            _batch_spec((bb, SUBLANES, GDN_CONV_CH), lambda b, t: (0, 0)),
        ],
        out_specs=[
            _batch_spec((bb, tile, N_PROJ), lambda b, t: (t, 0)),
            _batch_spec((bb, SUBLANES, GDN_CONV_CH), lambda b, t: (0, 0)),
        ],
        out_shape=[
            jax.ShapeDtypeStruct((B, L, N_PROJ), F32),
            jax.ShapeDtypeStruct((B, SUBLANES, GDN_CONV_CH), F32),
        ],
        scratch_shapes=[pltpu.VMEM((bb, SUBLANES, GDN_CONV_CH), F32)],
        compiler_params=pltpu.CompilerParams(
            dimension_semantics=("arbitrary", "arbitrary"), vmem_limit_bytes=VMEM_LIMIT),
        name="inproj_conv",
    )(x, norm_w, w_in_r, conv_w, conv_state8)


def _gdn_consts():
    ltri = np.tril(np.ones((CHUNK, CHUNK), np.float32))
    lane = np.arange(GDN_QK)
    eb = np.zeros((LANES, GDN_QK), np.float32)
    eb[lane // GDN_DK, lane] = 1.0
    eg = np.zeros((LANES, GDN_QK), np.float32)
    eg[GDN_HEADS + lane // GDN_DK, lane] = 1.0
    bd = np.kron(np.eye(GDN_HALF // GDN_DK, dtype=np.float32), np.ones((GDN_DK, GDN_DK), np.float32))
    return (jnp.asarray(ltri, BF16), jnp.asarray(eb, BF16), jnp.asarray(eg, BF16), jnp.asarray(bd, BF16))


def _gdn_kernel(*refs, n_chunks, n_valid, has_state):
    if has_state:
        (q_ref, k_ref, v_ref, z_ref, g_ref, s0_ref, alog_ref, dtb_ref, nw_ref,
         ltri_ref, eb_ref, eg_ref, bd_ref, o_ref, s_ref) = refs
    else:
        (q_ref, k_ref, v_ref, z_ref, g_ref, alog_ref, dtb_ref, nw_ref,
         ltri_ref, eb_ref, eg_ref, bd_ref, o_ref, s_ref) = refs
    c = CHUNK
    W = GDN_HALF
    bb = q_ref.shape[0]
    t = pl.program_id(1)

    @pl.when(t == 0)
    def _():
        s_ref[...] = s0_ref[...] if has_state else jnp.zeros(s_ref.shape, F32)

    row = lax.broadcasted_iota(jnp.int32, (c, W), 0)
    col = lax.broadcasted_iota(jnp.int32, (c, W), 1) & (GDN_DK - 1)
    eye_t = col == row
    causal_t = col <= row
    strict_t = col < row
    eye_f = eye_t.astype(F32)
    grow_ = lax.broadcasted_iota(jnp.int32, (c, LANES), 0)
    glane = lax.broadcasted_iota(jnp.int32, (c, LANES), 1)
    bd_mask = ((lax.broadcasted_iota(jnp.int32, (W, W), 0) & -GDN_DK)
               == (lax.broadcasted_iota(jnp.int32, (W, W), 1) & -GDN_DK))
    bd01 = bd_ref[...]
    neg_a = -jnp.exp(alog_ref[...])
    ltri = ltri_ref[...]

    def block_diag(x_b):
        return jnp.concatenate([x_b] * (W // c), axis=0) * bd01

    groups = [(bi, ci) for bi in range(bb) for ci in range(n_chunks)]
    items = [(bi, ci, a) for (bi, ci) in groups for a in range(GDN_HALVES)]
    rows = lambda ci: slice(ci * c, (ci + 1) * c)
    lanes = lambda a: slice(a * W, (a + 1) * W)

    gbs = {}
    for (bi, ci) in groups:
        gz = g_ref[bi, rows(ci), :]
        xg = gz + dtb_ref[...]
        softplus = jnp.maximum(xg, 0.0) + jnp.log(1.0 + jnp.exp(-jnp.abs(xg)))
        gb = jnp.where(glane < GDN_HEADS, _sigmoid(gz),
                       jnp.where(glane < 2 * GDN_HEADS, neg_a * softplus, 0.0))
        if n_valid is not None:
            gb = jnp.where(grow_ + ci * c < n_valid, gb, 0.0)
        gbs[(bi, ci)] = gb
    gcums = {g: _sel_left(ltri, gbs[g]) for g in groups}
    bxs = {g: _sel_right(gbs[g], eb_ref[...]) for g in groups}
    gws = {g: _sel_right(gcums[g], eg_ref[...]) for g in groups}

    st = {}
    for (bi, ci, a) in items:
        qp = q_ref[bi, rows(ci), lanes(a)]
        kp = k_ref[bi, rows(ci), lanes(a)]
        ss = _dot(jnp.concatenate([qp * qp, kp * kp], axis=0).astype(BF16), bd01)
        st[(bi, ci, a)] = dict(qp=qp, kp=kp, ss=ss)
    for (bi, ci, a) in items:
        d = st[(bi, ci, a)]
        qp, kp, ss = d["qp"], d["kp"], d["ss"]
        vp = v_ref[bi, rows(ci), lanes(a)]
        qn = qp * lax.rsqrt(ss[:c] + EPS) * (GDN_DK ** -0.5)
        kn = kp * lax.rsqrt(ss[c:] + EPS)
        bp = bxs[(bi, ci)][:, lanes(a)]
        gp = gws[(bi, ci)][:, lanes(a)]
        glast = gp[c - 1:c, :]
        e_g = jnp.exp(gp)
        kb = kn * bp
        gdiag = jnp.sum(jnp.where(eye_t, gp, 0.0), axis=0, keepdims=True)
        dec = jnp.where(causal_t, jnp.exp(jnp.minimum(gp - gdiag, 0.0)), 0.0)
        kq = _dot_nt(jnp.concatenate([kb, qn], axis=0).astype(BF16), block_diag(kn.astype(BF16)))
        a_neg = jnp.where(strict_t, -(kq[:c] * dec), 0.0)
        st[(bi, ci, a)] = dict(
            a_neg=a_neg, attn=(kq[c:] * dec).astype(BF16), vb=(vp * bp).astype(BF16),
            kbg=(kb * e_g).astype(BF16), qg=qn * e_g, kdec=(kn * jnp.exp(glast - gp)).astype(BF16),
            eglast=jnp.exp(glast))

    for it in items:
        a_b = st[it]["a_neg"].astype(BF16)
        st[it]["pw"] = _dot(a_b, block_diag(a_b))
        st[it]["xk"] = eye_f + st[it]["a_neg"]
    span = 4
    while span < c:
        for it in items:
            pw, xk = st[it]["pw"], st[it]["xk"]
            both = _dot(jnp.concatenate([pw, xk], axis=0).astype(BF16), block_diag(pw.astype(BF16)))
            st[it]["pw"] = both[:c]
            st[it]["xk"] = xk + both[c:]
        span *= 2
    for it in items:
        pw, xk = st[it]["pw"], st[it]["xk"]
        st[it]["t_b"] = (xk + _dot(xk.astype(BF16), block_diag(pw.astype(BF16)))).astype(BF16)
    for it in items:
        d = st[it]
        d["u_b"] = _dot(d["t_b"], block_diag(d["vb"])).astype(BF16)
        d["w_b"] = _dot(d["t_b"], block_diag(d["kbg"])).astype(BF16)

    for it in items:
        d = st[it]
        d["q_eff"] = (d["qg"] - _dot(d["attn"], block_diag(d["w_b"]))).astype(BF16)
        d["o_loc"] = _dot(d["attn"], block_diag(d["u_b"]))
        d["n_t"] = jnp.where(bd_mask, _dot_tn(d["u_b"], d["kdec"]), 0.0)
        d["m_t"] = jnp.where(bd_mask, -_dot_tn(d["w_b"], d["kdec"]), 0.0).astype(BF16)

    for ci in range(n_chunks):
        for bi in range(bb):
            for a in range(GDN_HALVES):
                d = st[(bi, ci, a)]
                s_old = s_ref[bi, a]
                s_b = s_old.astype(BF16)
                d["o"] = _dot_nt(d["q_eff"], s_b) + d["o_loc"]
                s_ref[bi, a] = s_old * d["eglast"] + d["n_t"] + _dot(s_b, d["m_t"])

    for it in items:
        st[it]["ms"] = _dot((st[it]["o"] * st[it]["o"]).astype(BF16), bd01) * (1.0 / GDN_DK)
    for (bi, ci, a) in items:
        d = st[(bi, ci, a)]
        zp = z_ref[bi, rows(ci), lanes(a)]
        o_ref[bi, rows(ci), lanes(a)] = d["o"] * lax.rsqrt(d["ms"] + EPS) * nw_ref[...] * (zp * _sigmoid(zp))


def _gdn(proj, s0_bd, alog_row, dtb_row, nw_row, consts, *, bb, n_chunks, n_valid):
    B, L, _ = proj.shape
    tile = n_chunks * CHUNK
    ltri, eb, eg, bd = consts
    has_state = s0_bd is not None
    kern = functools.partial(_gdn_kernel, n_chunks=n_chunks, n_valid=n_valid, has_state=has_state)
    col = lambda j: _batch_spec((bb, tile, GDN_QK), lambda b, t: (t, j))
    state = _batch_spec((bb, GDN_HALVES, GDN_HALF, GDN_HALF), lambda b, t: (0, 0, 0))
    consts_specs = [_const_spec((1, LANES)), _const_spec((1, LANES)), _const_spec((1, GDN_HALF)),
                    _const_spec(ltri.shape), _const_spec(eb.shape), _const_spec(eg.shape), _const_spec(bd.shape)]
    return pl.pallas_call(
        kern,
        grid=(B // bb, L // tile),
        in_specs=[col(0), col(1), col(2), col(3),
                  _batch_spec((bb, tile, LANES), lambda b, t: (t, GATE_COL // LANES))]
                 + ([state] if has_state else []) + consts_specs,
        out_specs=[_batch_spec((bb, tile, GDN_QK), lambda b, t: (t, 0)), state],
        out_shape=[
            jax.ShapeDtypeStruct((B, L, GDN_QK), F32),
            jax.ShapeDtypeStruct((B, GDN_HALVES, GDN_HALF, GDN_HALF), F32),
        ],
        compiler_params=pltpu.CompilerParams(
            dimension_semantics=("arbitrary", "arbitrary"), vmem_limit_bytes=VMEM_LIMIT),
        name="gdn_chunks",
    )(*([proj] * 5 + ([s0_bd] if has_state else []) + [alog_row, dtb_row, nw_row, ltri, eb, eg, bd]))


def _hgrn_levels():
    out = []
    m = 1
    while m < CHUNK:
        out.append(m)
        m *= 2
    return out


def _hgrn_consts():
    return jnp.asarray(np.tril(np.ones((CHUNK, CHUNK), np.float32)), BF16)


def _level_ref_rows(b, m):
    c = b.shape[0]
    bcast = lambda r, n: jnp.broadcast_to(b[r:r + 1, :], (n, LANES))
    if 2 * m >= SUBLANES:
        parts = [bcast(blk * 2 * m + m, 2 * m) for blk in range(c // (2 * m))]
    else:
        sub = lax.broadcasted_iota(jnp.int32, (SUBLANES, LANES), 0)
        parts = []
        for g in range(c // SUBLANES):
            acc = bcast(g * SUBLANES + m, SUBLANES)
            for blk in range(1, SUBLANES // (2 * m)):
                acc = jnp.where(sub >= blk * 2 * m, bcast(g * SUBLANES + blk * 2 * m + m, SUBLANES), acc)
            parts.append(acc)
    return parts[0] if len(parts) == 1 else jnp.concatenate(parts, axis=0)


def _hgrn_kernel(*refs, n_chunks, n_valid, has_state):
    if has_state:
        q_ref, f_ref, i_ref, z_ref, s0_ref, lb_ref, nw_ref, ltri_ref, o_ref, s_ref = refs
    else:
        q_ref, f_ref, i_ref, z_ref, lb_ref, nw_ref, ltri_ref, o_ref, s_ref = refs
    c = CHUNK
    bb = q_ref.shape[0]
    t = pl.program_id(1)

    @pl.when(t == 0)
    def _():
        s_ref[...] = s0_ref[...] if has_state else jnp.zeros(s_ref.shape, F32)

    row = lax.broadcasted_iota(jnp.int32, (c, LANES), 0)
    ri = lax.broadcasted_iota(jnp.int32, (c, c), 0)
    ci_ = lax.broadcasted_iota(jnp.int32, (c, c), 1)
    levels = _hgrn_levels()
    ltri = ltri_ref[...]
    lb = lb_ref[...]
    eye_f = (ri == ci_).astype(F32)
    upper = [(row & m) != 0 for m in levels]
    keep_f = [(((ri & -(2 * m)) == (ci_ & -(2 * m))) & ((ri & m) != 0) & ((ci_ & m) == 0)).astype(F32)
              for m in levels]

    groups = [(bi, ci) for bi in range(bb) for ci in range(n_chunks)]
    items = [(bi, ci, h) for (bi, ci) in groups for h in range(HG_HEADS)]
    rows = lambda ci: slice(ci * c, (ci + 1) * c)
    lanes = lambda h: slice(h * LANES, (h + 1) * LANES)

    kks, bs = {}, {}
    for (bi, ci) in groups:
        f = lb + (1.0 - lb) * _sigmoid(f_ref[bi, rows(ci), :])
        logf = jnp.log(f)
        kk = 1.0 - f
        if n_valid is not None:
            valid = lax.broadcasted_iota(jnp.int32, (c, HG_QK), 0) + ci * c < n_valid
            logf = jnp.where(valid, logf, 0.0)
            kk = jnp.where(valid, kk, 0.0)
        kks[(bi, ci)] = kk
        bs[(bi, ci)] = _sel_left(ltri, logf)

    st = {}
    for (bi, ci, h) in items:
        hq = q_ref[bi, rows(ci), lanes(h)]
        q = hq * _sigmoid(hq)
        k = kks[(bi, ci)][:, lanes(h)]
        v = i_ref[bi, rows(ci), lanes(h)]
        if n_valid is not None:
            v = jnp.where(row + ci * c < n_valid, v, 0.0)
        b = bs[(bi, ci)][:, lanes(h)]
        attn = eye_f * _dot_nt(q.astype(BF16), k.astype(BF16))
        for li, m in enumerate(levels):
            e = jnp.exp(-jnp.abs(b - _level_ref_rows(b, m)))
            z_b = (jnp.where(upper[li], q, k) * e).astype(BF16)
            attn = attn + keep_f[li] * _dot_nt(z_b, z_b)
        blast = b[c - 1:c, :]
        st[(bi, ci, h)] = dict(
            qe=(q * jnp.exp(b)).astype(BF16), attn=attn.astype(BF16), v_b=v.astype(BF16),
            kdec=(k * jnp.exp(blast - b)).astype(BF16), eblast=jnp.exp(blast))
    for d in st.values():
        d["av"] = _dot(d["attn"], d["v_b"])
        d["upd"] = _dot_tn(d["v_b"], d["kdec"])

    for ci in range(n_chunks):
        for bi in range(bb):
            for h in range(HG_HEADS):
                d = st[(bi, ci, h)]
                s_old = s_ref[bi, h]
                o = _dot_nt(d["qe"], s_old.astype(BF16)) + d["av"]
                s_ref[bi, h] = s_old * d["eblast"] + d["upd"]

                ms = jnp.mean(o * o, axis=-1, keepdims=True)
                zp = z_ref[bi, rows(ci), lanes(h)]
                o_ref[bi, rows(ci), lanes(h)] = o * lax.rsqrt(ms + EPS) * nw_ref[...] * (zp * _sigmoid(zp))


def _hgrn(proj, s0_t, lb_row, nw_row, ltri, *, bb, n_chunks, n_valid):
    B, L, _ = proj.shape
    tile = n_chunks * CHUNK
    has_state = s0_t is not None
    kern = functools.partial(_hgrn_kernel, n_chunks=n_chunks, n_valid=n_valid, has_state=has_state)
    col = lambda j: _batch_spec((bb, tile, HG_QK), lambda b, t: (t, j))
    state = _batch_spec((bb, HG_HEADS, HG_DK, HG_DK), lambda b, t: (0, 0, 0))
    return pl.pallas_call(
        kern,
        grid=(B // bb, L // tile),
        in_specs=[col(4), col(5), col(6), col(7)] + ([state] if has_state else [])
                 + [_const_spec((1, HG_QK)), _const_spec((1, HG_DK)), _const_spec(ltri.shape)],
        out_specs=[_batch_spec((bb, tile, HG_QK), lambda b, t: (t, 0)), state],
        out_shape=[
            jax.ShapeDtypeStruct((B, L, HG_QK), F32),
            jax.ShapeDtypeStruct((B, HG_HEADS, HG_DK, HG_DK), F32),
        ],
        compiler_params=pltpu.CompilerParams(
            dimension_semantics=("arbitrary", "arbitrary"), vmem_limit_bytes=VMEM_LIMIT),
        name="hgrn_chunks",
    )(*([proj] * 4 + ([s0_t] if has_state else []) + [lb_row, nw_row, ltri]))


def _outproj_kernel(x_ref, oa_ref, ob_ref, w_ref, fw_ref, y_ref):
    T = x_ref.shape[1]
    rows = min(T, OUT_ROWS)
    for r in range(0, T, rows):
        o = jnp.concatenate([oa_ref[0, r:r + rows, :], ob_ref[0, r:r + rows, :]], axis=1).astype(BF16)
        y = x_ref[0, r:r + rows, :] + _dot(o, w_ref[...])
        y_ref[0, r:r + rows, :] = y * lax.rsqrt(jnp.mean(y * y, axis=-1, keepdims=True) + EPS) * fw_ref[...]


def _outproj(x, oa, ob, w_out_b, fw_row, *, tile):
    B, L, _ = x.shape
    return pl.pallas_call(
        _outproj_kernel,
        grid=(B, L // tile),
        in_specs=[
            _batch_spec((1, tile, D_MODEL), lambda b, t: (t, 0)),
            _batch_spec((1, tile, GDN_QK), lambda b, t: (t, 0)),
            _batch_spec((1, tile, HG_QK), lambda b, t: (t, 0)),
            _const_spec((D_MODEL, D_MODEL)),
            _const_spec((1, D_MODEL)),
        ],
        out_specs=_batch_spec((1, tile, D_MODEL), lambda b, t: (t, 0)),
        out_shape=jax.ShapeDtypeStruct((B, L, D_MODEL), F32),
        compiler_params=pltpu.CompilerParams(
            dimension_semantics=("arbitrary", "arbitrary"), vmem_limit_bytes=VMEM_LIMIT),
        name="outproj_norm",
    )(x, oa, ob, w_out_b, fw_row)


def _layer(x, conv_state, s_gdn, s_hg, params, consts, *, n_valid, bb, in_tile, n_chunks, out_tile):
    (norm_w, w_in_r, conv_w, alog_row, dtb_row, gdn_nw_row, lb_row, hg_nw_row, w_out_b, fw_row) = params
    gdn_consts, hg_ltri = consts
    B, L, _ = x.shape
    if conv_state is None:
        cs8 = jnp.zeros((B, SUBLANES, GDN_CONV_CH), F32)
    else:
        cs8 = jnp.pad(conv_state, ((0, 0), (SUBLANES - (CONV_W - 1), 0), (0, 0)))
    proj, nc8 = _inproj(x, norm_w, w_in_r, conv_w, cs8, bb=bb, tile=in_tile,
                        n_valid=in_tile if n_valid is None else n_valid)
    new_conv = nc8[:, SUBLANES - (CONV_W - 1):, :]

    hph = GDN_HALF // GDN_DK
    eye_h = jnp.eye(hph, dtype=F32)
    s0_bd = None
    if s_gdn is not None:
        sg = s_gdn.reshape(B, GDN_HALVES, hph, GDN_DK, GDN_DK)
        s0_bd = jnp.einsum("bahde,hg->bahegd", sg, eye_h).reshape(B, GDN_HALVES, GDN_HALF, GDN_HALF)
    oa, s_bd = _gdn(proj, s0_bd, alog_row, dtb_row, gdn_nw_row, gdn_consts,
                    bb=bb, n_chunks=n_chunks, n_valid=n_valid)
    s_bd = s_bd.reshape(B, GDN_HALVES, hph, GDN_DK, hph, GDN_DK)
    new_gdn = jnp.einsum("bahegd,hg->bahde", s_bd, eye_h).reshape(B, GDN_HEADS, GDN_DK, GDN_DK)

    ob, s_t = _hgrn(proj, None if s_hg is None else jnp.swapaxes(s_hg, -1, -2), lb_row, hg_nw_row, hg_ltri,
                    bb=bb, n_chunks=n_chunks, n_valid=n_valid)
    new_hg = jnp.swapaxes(s_t, -1, -2)

    rows = B * L
    y = _outproj(x.reshape(1, rows, D_MODEL), oa.reshape(1, rows, GDN_QK), ob.reshape(1, rows, HG_QK),
                 w_out_b, fw_row, tile=out_tile).reshape(B, L, D_MODEL)
    return y, new_conv, new_gdn, new_hg


def _prep(norm_w, w_in, conv_w, gdn_A_log, gdn_dt_bias, gdn_norm_w, hgrn_lb_logits, hgrn_norm_w, w_out,
          final_norm_w):
    w = w_in[0]
    o_qkv, o_za = 0, GDN_CONV_CH
    o_b = o_za + GDN_QK
    o_a = o_b + GDN_HEADS
    o_hq = o_a + GDN_HEADS
    gate_cols = jnp.pad(w[:, o_b:o_hq], ((0, 0), (0, LANES - 2 * GDN_HEADS)))
    w_in_r = jnp.concatenate([w[:, o_qkv:o_b], w[:, o_hq:], gate_cols], axis=1).astype(BF16)

    pad_row = lambda v: jnp.pad(v.astype(F32), (GDN_HEADS, LANES - 2 * GDN_HEADS))[None, :]
    lb = jnp.cumsum(jax.nn.softmax(hgrn_lb_logits.astype(F32), axis=0), axis=0)[0]
    params = (norm_w[0][None, :], w_in_r, conv_w[0], pad_row(gdn_A_log[0]), pad_row(gdn_dt_bias[0]),
              jnp.tile(gdn_norm_w[0], GDN_HALF // GDN_DK)[None, :], lb[None, :], hgrn_norm_w[0][None, :],
              w_out[0].astype(BF16), final_norm_w[None, :])
    consts = (_gdn_consts(), _hgrn_consts())
    return params, consts


def kernel(x_prompt, x_sample, state_conv, state_gdn, state_hgrn, norm_w, w_in, conv_w, gdn_A_log,
           gdn_dt_bias, gdn_norm_w, hgrn_lb_logits, hgrn_norm_w, w_out, final_norm_w):
    Bs, Ls, _ = x_sample.shape
    params, consts = _prep(norm_w, w_in, conv_w, gdn_A_log, gdn_dt_bias, gdn_norm_w, hgrn_lb_logits,
                           hgrn_norm_w, w_out, final_norm_w)

    y_p, c_p, g_p, r_p = _layer(x_prompt, None, None, None, params, consts,
                                n_valid=None, bb=1, in_tile=512, n_chunks=8, out_tile=512)

    xs = jnp.pad(x_sample, ((0, 0), (0, CHUNK - Ls), (0, 0)))
    y_s, c_s, g_s, r_s = _layer(xs, state_conv[0], state_gdn[0], state_hgrn[0], params, consts,
                                n_valid=Ls, bb=8, in_tile=CHUNK, n_chunks=1, out_tile=512)
    y_s = y_s[:, :Ls]

    return (y_p, y_s, c_p[None], g_p[None], r_p[None], c_s[None], g_s[None], r_s[None])
```

```python
import functools

import numpy as np
import jax
import jax.numpy as jnp
from jax import lax
from jax.experimental import pallas as pl
from jax.experimental.pallas import tpu as pltpu

F32 = jnp.float32
BF16 = jnp.bfloat16

D_MODEL = 1024
CHUNK = 64
GDN_HEADS = 8
GDN_DK = 64
GDN_QK = GDN_HEADS * GDN_DK
GDN_CONV_CH = 3 * GDN_QK
CONV_W = 4
HG_HEADS = 4
HG_DK = 128
HG_QK = HG_HEADS * HG_DK
EPS = 1e-6

LANES = 128
SUBLANES = 8
MXU_DIM = 256
SLAB = 512
OUT_ROWS = 128
GATE_COL = GDN_CONV_CH + GDN_QK + 4 * HG_QK
N_PROJ = GATE_COL + LANES
GDN_HALF = MXU_DIM
GDN_HALVES = GDN_QK // GDN_HALF
VMEM_LIMIT = 48 * 1024 * 1024


def _sigmoid(x):
    return 1.0 / (1.0 + jnp.exp(-x))


def _split3(x):
    hi = x.astype(BF16)
    r = x - hi.astype(F32)
    mid = r.astype(BF16)
    lo = (r - mid.astype(F32)).astype(BF16)
    return hi, mid, lo


def _dot(a, b):
    return jnp.dot(a, b, preferred_element_type=F32)


def _dot_nt(a, b):
    return lax.dot_general(a, b, (((1,), (1,)), ((), ())), preferred_element_type=F32)


def _dot_tn(a, b):
    return lax.dot_general(a, b, (((0,), (0,)), ((), ())), preferred_element_type=F32)


def _sel_left(sel01, x):
    hi, mid, lo = _split3(x)
    return _dot(sel01, hi) + _dot(sel01, mid) + _dot(sel01, lo)


def _sel_right(x, sel01):
    hi, mid, lo = _split3(x)
    return _dot(hi, sel01) + _dot(mid, sel01) + _dot(lo, sel01)


def _batch_spec(shape, index):
    return pl.BlockSpec(shape, lambda b, t: (b,) + tuple(index(b, t)))


def _const_spec(shape):
    return pl.BlockSpec(shape, lambda b, t: (0,) * len(shape))


def _inproj_kernel(x_ref, nw_ref, w_ref, cw_ref, cs_ref, proj_ref, nc_ref, carry, *, n_valid):
    t = pl.program_id(1)
    bb, T = x_ref.shape[0], x_ref.shape[1]

    @pl.when(t == 0)
    def _():
        carry[...] = cs_ref[...]

    x = x_ref[...].reshape(bb * T, D_MODEL)
    h = x * lax.rsqrt(jnp.mean(x * x, axis=-1, keepdims=True) + EPS) * nw_ref[...]
    hb = h.astype(BF16)

    assert CONV_W == 4
    for s in range(0, GDN_CONV_CH, SLAB):
        sl = slice(s, s + SLAB)
        u_all = _dot(hb, w_ref[:, sl])
        for bi in range(bb):
            u = u_all[bi * T:(bi + 1) * T]
            full = jnp.concatenate([carry[bi, :, sl], u], axis=0)
            prev = pltpu.roll(full, 1, 0)
            pair = cw_ref[1:2, sl] * full + cw_ref[0:1, sl] * prev
            y = (cw_ref[3:4, sl] * full + cw_ref[2:3, sl] * prev + pltpu.roll(pair, 2, 0))[SUBLANES:]
            proj_ref[bi, :, sl] = y * _sigmoid(y)
            nc_ref[bi, :, sl] = full[n_valid:n_valid + SUBLANES]
            carry[bi, :, sl] = full[T:T + SUBLANES]
    for s in list(range(GDN_CONV_CH, GATE_COL, SLAB)) + [GATE_COL]:
        sl = slice(s, min(s + SLAB, N_PROJ))
        res = _dot(hb, w_ref[:, sl])
        for bi in range(bb):
            proj_ref[bi, :, sl] = res[bi * T:(bi + 1) * T]


def _inproj(x, norm_w, w_in_r, conv_w, conv_state8, *, bb, tile, n_valid):
    B, L, _ = x.shape
    kern = functools.partial(_inproj_kernel, n_valid=n_valid)
    return pl.pallas_call(
        kern,
        grid=(B // bb, L // tile),
        in_specs=[
            _batch_spec((bb, tile, D_MODEL), lambda b, t: (t, 0)),
            _const_spec((1, D_MODEL)),
            _const_spec((D_MODEL, N_PROJ)),
            _const_spec((CONV_W, GDN_CONV_CH)),
            _batch_spec((bb, SUBLANES, GDN_CONV_CH), lambda b, t: (0, 0)),
        ],
        out_specs=[
            _batch_spec((bb, tile, N_PROJ), lambda b, t: (t, 0)),
            _batch_spec((bb, SUBLANES, GDN_CONV_CH), lambda b, t: (0, 0)),
        ],
        out_shape=[
            jax.ShapeDtypeStruct((B, L, N_PROJ), F32),
            jax.ShapeDtypeStruct((B, SUBLANES, GDN_CONV_CH), F32),
        ],
        scratch_shapes=[pltpu.VMEM((bb, SUBLANES, GDN_CONV_CH), F32)],
        compiler_params=pltpu.CompilerParams(
            dimension_semantics=("arbitrary", "arbitrary"), vmem_limit_bytes=VMEM_LIMIT),
        name="inproj_conv",
    )(x, norm_w, w_in_r, conv_w, conv_state8)


def _gdn_consts():
    ltri = np.tril(np.ones((CHUNK, CHUNK), np.float32))
    bd = np.kron(np.eye(GDN_HALF // GDN_DK, dtype=np.float32), np.ones((GDN_DK, GDN_DK), np.float32))
    return (jnp.asarray(ltri, BF16), jnp.asarray(bd, BF16))


def _gdn_kernel(*refs, n_chunks, n_valid, has_state):
    if has_state:
        (q_ref, k_ref, v_ref, z_ref, g_ref, s0_ref, alog_ref, dtb_ref, nw_ref,
         ltri_ref, bd_ref, o_ref, sn_ref, s_ref) = refs
    else:
        (q_ref, k_ref, v_ref, z_ref, g_ref, alog_ref, dtb_ref, nw_ref,
         ltri_ref, bd_ref, o_ref, sn_ref, s_ref) = refs
    c = CHUNK
    W = GDN_HALF
    bb = q_ref.shape[0]
    t = pl.program_id(1)
    bd_mask = ((lax.broadcasted_iota(jnp.int32, (W, W), 0) & -GDN_DK)
               == (lax.broadcasted_iota(jnp.int32, (W, W), 1) & -GDN_DK))

    @pl.when(t == 0)
    def _():
        if has_state:
            for bi in range(bb):
                for a in range(GDN_HALVES):
                    wide = jnp.concatenate([s0_ref[bi, a]] * (W // GDN_DK), axis=1)
                    s_ref[bi, a] = jnp.where(bd_mask, wide, 0.0).T
        else:
            s_ref[...] = jnp.zeros(s_ref.shape, F32)

    row = lax.broadcasted_iota(jnp.int32, (c, W), 0)
    col = lax.broadcasted_iota(jnp.int32, (c, W), 1) & (GDN_DK - 1)
    eye_t = col == row
    causal_t = col <= row
    strict_t = col < row
    eye_f = eye_t.astype(F32)
    grow_ = lax.broadcasted_iota(jnp.int32, (c, LANES), 0)
    glane = lax.broadcasted_iota(jnp.int32, (c, LANES), 1)
    bd01 = bd_ref[...]
    neg_a = -jnp.exp(alog_ref[...])
    ltri = ltri_ref[...]

    def block_diag(x_b):
        return jnp.concatenate([x_b] * (W // c), axis=0) * bd01

    groups = [(bi, ci) for bi in range(bb) for ci in range(n_chunks)]
    items = [(bi, ci, a) for (bi, ci) in groups for a in range(GDN_HALVES)]
    rows = lambda ci: slice(ci * c, (ci + 1) * c)
    lanes = lambda a: slice(a * W, (a + 1) * W)

    gbs = {}
    for (bi, ci) in groups:
        gz = g_ref[bi, rows(ci), :]
        xg = gz + dtb_ref[...]
        softplus = jnp.maximum(xg, 0.0) + jnp.log(1.0 + jnp.exp(-jnp.abs(xg)))
        gb = jnp.where(glane < GDN_HEADS, _sigmoid(gz),
                       jnp.where(glane < 2 * GDN_HEADS, neg_a * softplus, 0.0))
        if n_valid is not None:
            gb = jnp.where(grow_ + ci * c < n_valid, gb, 0.0)
        gbs[(bi, ci)] = gb
    gcums = {g: _sel_left(ltri, gbs[g]) for g in groups}
    lane_w = lax.broadcasted_iota(jnp.int32, (c, LANES), 1)

    def to_head_lanes(x, first):
        slabs = []
        for p in range(GDN_QK // LANES):
            even = jnp.broadcast_to(x[:, first + 2 * p:first + 2 * p + 1], (c, LANES))
            odd = jnp.broadcast_to(x[:, first + 2 * p + 1:first + 2 * p + 2], (c, LANES))
            slabs.append(jnp.where(lane_w < GDN_DK, even, odd))
        return jnp.concatenate(slabs, axis=1)

    bxs = {g: to_head_lanes(gbs[g], 0) for g in groups}
    gws = {g: to_head_lanes(gcums[g], GDN_HEADS) for g in groups}

    st = {}
    for (bi, ci, a) in items:
        qp = q_ref[bi, rows(ci), lanes(a)]
        kp = k_ref[bi, rows(ci), lanes(a)]
        ss = _dot(jnp.concatenate([qp * qp, kp * kp], axis=0).astype(BF16), bd01)
        st[(bi, ci, a)] = dict(qp=qp, kp=kp, ss=ss)
    for (bi, ci, a) in items:
        d = st[(bi, ci, a)]
        qp, kp, ss = d["qp"], d["kp"], d["ss"]
        vp = v_ref[bi, rows(ci), lanes(a)]
        qn = qp * lax.rsqrt(ss[:c] + EPS) * (GDN_DK ** -0.5)
        kn = kp * lax.rsqrt(ss[c:] + EPS)
        bp = bxs[(bi, ci)][:, lanes(a)]
        gp = gws[(bi, ci)][:, lanes(a)]
        glast = gp[c - 1:c, :]
        e_g = jnp.exp(gp)
        kb = kn * bp
        gdiag = jnp.sum(jnp.where(eye_t, gp, 0.0), axis=0, keepdims=True)
        dec = jnp.where(causal_t, jnp.exp(jnp.minimum(gp - gdiag, 0.0)), 0.0)
        kq = _dot_nt(jnp.concatenate([kb, qn], axis=0).astype(BF16), block_diag(kn.astype(BF16)))
        a_neg = jnp.where(strict_t, -(kq[:c] * dec), 0.0)
        st[(bi, ci, a)] = dict(
            a_neg=a_neg, attn=(kq[c:] * dec).astype(BF16), vb=(vp * bp).astype(BF16),
            kbg=(kb * e_g).astype(BF16), qg=qn * e_g, kdec=(kn * jnp.exp(glast - gp)).astype(BF16),
            eglast=jnp.exp(glast))

    for it in items:
        a_b = st[it]["a_neg"].astype(BF16)
        st[it]["pw"] = _dot(a_b, block_diag(a_b))
        st[it]["xk"] = eye_f + st[it]["a_neg"]
    span = 4
    while span < c:
        for it in items:
            pw, xk = st[it]["pw"], st[it]["xk"]
            both = _dot(jnp.concatenate([pw, xk], axis=0).astype(BF16), block_diag(pw.astype(BF16)))
            st[it]["pw"] = both[:c]
            st[it]["xk"] = xk + both[c:]
        span *= 2
    for it in items:
        pw, xk = st[it]["pw"], st[it]["xk"]
        st[it]["t_b"] = (xk + _dot(xk.astype(BF16), block_diag(pw.astype(BF16)))).astype(BF16)
    for it in items:
        d = st[it]
        d["u_b"] = _dot(d["t_b"], block_diag(d["vb"])).astype(BF16)
        d["w_b"] = _dot(d["t_b"], block_diag(d["kbg"])).astype(BF16)

    for it in items:
        d = st[it]
        d["q_eff"] = (d["qg"] - _dot(d["attn"], block_diag(d["w_b"]))).astype(BF16)
        d["o_loc"] = _dot(d["attn"], block_diag(d["u_b"]))
        d["n_t"] = jnp.where(bd_mask, _dot_tn(d["u_b"], d["kdec"]), 0.0)
        d["m_t"] = jnp.where(bd_mask, -_dot_tn(d["w_b"], d["kdec"]), 0.0).astype(BF16)

    for ci in range(n_chunks):
        for bi in range(bb):
            for a in range(GDN_HALVES):
                d = st[(bi, ci, a)]
                s_old = s_ref[bi, a]
                s_b = s_old.astype(BF16)
                d["o"] = _dot_nt(d["q_eff"], s_b) + d["o_loc"]
                s_ref[bi, a] = s_old * d["eglast"] + d["n_t"] + _dot(s_b, d["m_t"])

    for it in items:
        st[it]["ms"] = _dot((st[it]["o"] * st[it]["o"]).astype(BF16), bd01) * (1.0 / GDN_DK)
    for (bi, ci, a) in items:
        d = st[(bi, ci, a)]
        zp = z_ref[bi, rows(ci), lanes(a)]
        o_ref[bi, rows(ci), lanes(a)] = d["o"] * lax.rsqrt(d["ms"] + EPS) * nw_ref[...] * (zp * _sigmoid(zp))

    @pl.when(t == pl.num_programs(1) - 1)
    def _():
        for bi in range(bb):
            for a in range(GDN_HALVES):
                s_bd = s_ref[bi, a].T
                f = s_bd[:, :LANES] + s_bd[:, LANES:]
                sn_ref[bi, a] = f[:, :GDN_DK] + f[:, GDN_DK:]


def _gdn(proj, s0, alog_row, dtb_row, nw_row, consts, *, bb, n_chunks, n_valid):
    B, L, _ = proj.shape
    tile = n_chunks * CHUNK
    ltri, bd = consts
    has_state = s0 is not None
    kern = functools.partial(_gdn_kernel, n_chunks=n_chunks, n_valid=n_valid, has_state=has_state)
    col = lambda j: _batch_spec((bb, tile, GDN_QK), lambda b, t: (t, j))
    state_shape = (B, GDN_HALVES, GDN_HALF, GDN_DK)
    state = _batch_spec((bb,) + state_shape[1:], lambda b, t: (0, 0, 0))
    consts_specs = [_const_spec((1, LANES)), _const_spec((1, LANES)), _const_spec((1, GDN_HALF)),
                    _const_spec(ltri.shape), _const_spec(bd.shape)]
    oa, sn = pl.pallas_call(
        kern,
        grid=(B // bb, L // tile),
        in_specs=[col(0), col(1), col(2), col(3),
                  _batch_spec((bb, tile, LANES), lambda b, t: (t, GATE_COL // LANES))]
                 + ([state] if has_state else []) + consts_specs,
        out_specs=[_batch_spec((bb, tile, GDN_QK), lambda b, t: (t, 0)), state],
        out_shape=[
            jax.ShapeDtypeStruct((B, L, GDN_QK), F32),
            jax.ShapeDtypeStruct(state_shape, F32),
        ],
        scratch_shapes=[pltpu.VMEM((bb, GDN_HALVES, GDN_HALF, GDN_HALF), F32)],
        compiler_params=pltpu.CompilerParams(
            dimension_semantics=("arbitrary", "arbitrary"), vmem_limit_bytes=VMEM_LIMIT),
        name="gdn_chunks",
    )(*([proj] * 5 + ([s0.reshape(state_shape)] if has_state else []) + [alog_row, dtb_row, nw_row, ltri, bd]))
    return oa, sn.reshape(B, GDN_HEADS, GDN_DK, GDN_DK)


def _hgrn_levels():
    out = []
    m = 1
    while m < CHUNK:
        out.append(m)
        m *= 2
    return out


def _hgrn_consts():
    return jnp.asarray(np.tril(np.ones((CHUNK, CHUNK), np.float32)), BF16)


def _level_ref_rows(b, m):
    c = b.shape[0]
    bcast = lambda r, n: jnp.broadcast_to(b[r:r + 1, :], (n, LANES))
    if 2 * m >= SUBLANES:
        parts = [bcast(blk * 2 * m + m, 2 * m) for blk in range(c // (2 * m))]
    else:
        sub = lax.broadcasted_iota(jnp.int32, (SUBLANES, LANES), 0)
        parts = []
        for g in range(c // SUBLANES):
            acc = bcast(g * SUBLANES + m, SUBLANES)
            for blk in range(1, SUBLANES // (2 * m)):
                acc = jnp.where(sub >= blk * 2 * m, bcast(g * SUBLANES + blk * 2 * m + m, SUBLANES), acc)
            parts.append(acc)
    return parts[0] if len(parts) == 1 else jnp.concatenate(parts, axis=0)


def _hgrn_kernel(*refs, n_chunks, n_valid, has_state):
    if has_state:
        q_ref, f_ref, i_ref, z_ref, s0_ref, lb_ref, nw_ref, ltri_ref, o_ref, sn_ref, s_ref = refs
    else:
        q_ref, f_ref, i_ref, z_ref, lb_ref, nw_ref, ltri_ref, o_ref, sn_ref, s_ref = refs
    c = CHUNK
    bb = q_ref.shape[0]
    t = pl.program_id(1)

    @pl.when(t == 0)
    def _():
        if has_state:
            for bi in range(bb):
                for h in range(HG_HEADS):
                    s_ref[bi, h] = s0_ref[bi, h].T
        else:
            s_ref[...] = jnp.zeros(s_ref.shape, F32)

    row = lax.broadcasted_iota(jnp.int32, (c, LANES), 0)
    ri = lax.broadcasted_iota(jnp.int32, (c, c), 0)
    ci_ = lax.broadcasted_iota(jnp.int32, (c, c), 1)
    levels = _hgrn_levels()
    ltri = ltri_ref[...]
    lb = lb_ref[...]
    eye_f = (ri == ci_).astype(F32)
    upper = [(row & m) != 0 for m in levels]
    keep_f = [(((ri & -(2 * m)) == (ci_ & -(2 * m))) & ((ri & m) != 0) & ((ci_ & m) == 0)).astype(F32)
              for m in levels]

    groups = [(bi, ci) for bi in range(bb) for ci in range(n_chunks)]
    items = [(bi, ci, h) for (bi, ci) in groups for h in range(HG_HEADS)]
    rows = lambda ci: slice(ci * c, (ci + 1) * c)
    lanes = lambda h: slice(h * LANES, (h + 1) * LANES)

    kks, bs = {}, {}
    for (bi, ci) in groups:
        f = lb + (1.0 - lb) * _sigmoid(f_ref[bi, rows(ci), :])
        logf = jnp.log(f)
        kk = 1.0 - f
        if n_valid is not None:
            valid = lax.broadcasted_iota(jnp.int32, (c, HG_QK), 0) + ci * c < n_valid
            logf = jnp.where(valid, logf, 0.0)
            kk = jnp.where(valid, kk, 0.0)
        kks[(bi, ci)] = kk
        bs[(bi, ci)] = _sel_left(ltri, logf)

    st = {}
    for (bi, ci, h) in items:
        hq = q_ref[bi, rows(ci), lanes(h)]
        q = hq * _sigmoid(hq)
        k = kks[(bi, ci)][:, lanes(h)]
        v = i_ref[bi, rows(ci), lanes(h)]
        if n_valid is not None:
            v = jnp.where(row + ci * c < n_valid, v, 0.0)
        b = bs[(bi, ci)][:, lanes(h)]
        attn = eye_f * _dot_nt(q.astype(BF16), k.astype(BF16))
        for li, m in enumerate(levels):
            e = jnp.exp(-jnp.abs(b - _level_ref_rows(b, m)))
            z_b = (jnp.where(upper[li], q, k) * e).astype(BF16)
            attn = attn + keep_f[li] * _dot_nt(z_b, z_b)
        blast = b[c - 1:c, :]
        st[(bi, ci, h)] = dict(
            qe=(q * jnp.exp(b)).astype(BF16), attn=attn.astype(BF16), v_b=v.astype(BF16),
            kdec=(k * jnp.exp(blast - b)).astype(BF16), eblast=jnp.exp(blast))
    for d in st.values():
        d["av"] = _dot(d["attn"], d["v_b"])
        d["upd"] = _dot_tn(d["v_b"], d["kdec"])

    for ci in range(n_chunks):
        for bi in range(bb):
            for h in range(HG_HEADS):
                d = st[(bi, ci, h)]
                s_old = s_ref[bi, h]
                o = _dot_nt(d["qe"], s_old.astype(BF16)) + d["av"]
                s_ref[bi, h] = s_old * d["eblast"] + d["upd"]

                ms = jnp.mean(o * o, axis=-1, keepdims=True)
                zp = z_ref[bi, rows(ci), lanes(h)]
                o_ref[bi, rows(ci), lanes(h)] = o * lax.rsqrt(ms + EPS) * nw_ref[...] * (zp * _sigmoid(zp))

    @pl.when(t == pl.num_programs(1) - 1)
    def _():
        for bi in range(bb):
            for h in range(HG_HEADS):
                sn_ref[bi, h] = s_ref[bi, h].T


def _hgrn(proj, s0, lb_row, nw_row, ltri, *, bb, n_chunks, n_valid):
    B, L, _ = proj.shape
    tile = n_chunks * CHUNK
    has_state = s0 is not None
    kern = functools.partial(_hgrn_kernel, n_chunks=n_chunks, n_valid=n_valid, has_state=has_state)
    col = lambda j: _batch_spec((bb, tile, HG_QK), lambda b, t: (t, j))
    state = _batch_spec((bb, HG_HEADS, HG_DK, HG_DK), lambda b, t: (0, 0, 0))
    return pl.pallas_call(
        kern,
        grid=(B // bb, L // tile),
        in_specs=[col(4), col(5), col(6), col(7)] + ([state] if has_state else [])
                 + [_const_spec((1, HG_QK)), _const_spec((1, HG_DK)), _const_spec(ltri.shape)],
        out_specs=[_batch_spec((bb, tile, HG_QK), lambda b, t: (t, 0)), state],
        out_shape=[
            jax.ShapeDtypeStruct((B, L, HG_QK), F32),
            jax.ShapeDtypeStruct((B, HG_HEADS, HG_DK, HG_DK), F32),
        ],
        scratch_shapes=[pltpu.VMEM((bb, HG_HEADS, HG_DK, HG_DK), F32)],
        compiler_params=pltpu.CompilerParams(
            dimension_semantics=("arbitrary", "arbitrary"), vmem_limit_bytes=VMEM_LIMIT),
        name="hgrn_chunks",
    )(*([proj] * 4 + ([s0] if has_state else []) + [lb_row, nw_row, ltri]))


def _outproj_kernel(x_ref, oa_ref, ob_ref, w_ref, fw_ref, y_ref):
    T = x_ref.shape[1]
    rows = min(T, OUT_ROWS)
    for r in range(0, T, rows):
        o = jnp.concatenate([oa_ref[0, r:r + rows, :], ob_ref[0, r:r + rows, :]], axis=1).astype(BF16)
        y = x_ref[0, r:r + rows, :] + _dot(o, w_ref[...])
        y_ref[0, r:r + rows, :] = y * lax.rsqrt(jnp.mean(y * y, axis=-1, keepdims=True) + EPS) * fw_ref[...]


def _outproj(x, oa, ob, w_out_b, fw_row, *, tile):
    B, L, _ = x.shape
    return pl.pallas_call(
        _outproj_kernel,
        grid=(B, L // tile),
        in_specs=[
            _batch_spec((1, tile, D_MODEL), lambda b, t: (t, 0)),
            _batch_spec((1, tile, GDN_QK), lambda b, t: (t, 0)),
            _batch_spec((1, tile, HG_QK), lambda b, t: (t, 0)),
            _const_spec((D_MODEL, D_MODEL)),
            _const_spec((1, D_MODEL)),
        ],
        out_specs=_batch_spec((1, tile, D_MODEL), lambda b, t: (t, 0)),
        out_shape=jax.ShapeDtypeStruct((B, L, D_MODEL), F32),
        compiler_params=pltpu.CompilerParams(
            dimension_semantics=("arbitrary", "arbitrary"), vmem_limit_bytes=VMEM_LIMIT),
        name="outproj_norm",
    )(x, oa, ob, w_out_b, fw_row)


def _layer(x, conv_state, s_gdn, s_hg, params, consts, *, n_valid, bb, in_tile, n_chunks, out_tile):
    (norm_w, w_in_r, conv_w, alog_row, dtb_row, gdn_nw_row, lb_row, hg_nw_row, w_out_b, fw_row) = params
    gdn_consts, hg_ltri = consts
    B, L, _ = x.shape
    if conv_state is None:
        cs8 = jnp.zeros((B, SUBLANES, GDN_CONV_CH), F32)
    else:
        cs8 = jnp.pad(conv_state, ((0, 0), (SUBLANES - (CONV_W - 1), 0), (0, 0)))
    proj, nc8 = _inproj(x, norm_w, w_in_r, conv_w, cs8, bb=bb, tile=in_tile,
                        n_valid=in_tile if n_valid is None else n_valid)
    new_conv = nc8[:, SUBLANES - (CONV_W - 1):, :]

    oa, new_gdn = _gdn(proj, s_gdn, alog_row, dtb_row, gdn_nw_row, gdn_consts,
                       bb=bb, n_chunks=n_chunks, n_valid=n_valid)
    ob, new_hg = _hgrn(proj, s_hg, lb_row, hg_nw_row, hg_ltri, bb=bb, n_chunks=n_chunks, n_valid=n_valid)

    rows = B * L
    y = _outproj(x.reshape(1, rows, D_MODEL), oa.reshape(1, rows, GDN_QK), ob.reshape(1, rows, HG_QK),
                 w_out_b, fw_row, tile=out_tile).reshape(B, L, D_MODEL)
    return y, new_conv, new_gdn, new_hg


def _prep(norm_w, w_in, conv_w, gdn_A_log, gdn_dt_bias, gdn_norm_w, hgrn_lb_logits, hgrn_norm_w, w_out,
          final_norm_w):
    w = w_in[0]
    o_qkv, o_za = 0, GDN_CONV_CH
    o_b = o_za + GDN_QK
    o_a = o_b + GDN_HEADS
    o_hq = o_a + GDN_HEADS
    gate_cols = jnp.pad(w[:, o_b:o_hq], ((0, 0), (0, LANES - 2 * GDN_HEADS)))
    w_in_r = jnp.concatenate([w[:, o_qkv:o_b], w[:, o_hq:], gate_cols], axis=1).astype(BF16)

    pad_row = lambda v: jnp.pad(v.astype(F32), (GDN_HEADS, LANES - 2 * GDN_HEADS))[None, :]
    lb = jnp.cumsum(jax.nn.softmax(hgrn_lb_logits.astype(F32), axis=0), axis=0)[0]
    params = (norm_w[0][None, :], w_in_r, conv_w[0], pad_row(gdn_A_log[0]), pad_row(gdn_dt_bias[0]),
              jnp.tile(gdn_norm_w[0], GDN_HALF // GDN_DK)[None, :], lb[None, :], hgrn_norm_w[0][None, :],
              w_out[0].astype(BF16), final_norm_w[None, :])
    consts = (_gdn_consts(), _hgrn_consts())
    return params, consts


def kernel(x_prompt, x_sample, state_conv, state_gdn, state_hgrn, norm_w, w_in, conv_w, gdn_A_log,
           gdn_dt_bias, gdn_norm_w, hgrn_lb_logits, hgrn_norm_w, w_out, final_norm_w):
    Bs, Ls, _ = x_sample.shape
    params, consts = _prep(norm_w, w_in, conv_w, gdn_A_log, gdn_dt_bias, gdn_norm_w, hgrn_lb_logits,
                           hgrn_norm_w, w_out, final_norm_w)

    y_p, c_p, g_p, r_p = _layer(x_prompt, None, None, None, params, consts,
                                n_valid=None, bb=1, in_tile=512, n_chunks=8, out_tile=512)

    xs = jnp.pad(x_sample, ((0, 0), (0, CHUNK - Ls), (0, 0)))
    y_s, c_s, g_s, r_s = _layer(xs, state_conv[0], state_gdn[0], state_hgrn[0], params, consts,
                                n_valid=Ls, bb=8, in_tile=CHUNK, n_chunks=1, out_tile=512)
    y_s = y_s[:, :Ls]

    return (y_p, y_s, c_p[None], g_p[None], r_p[None], c_s[None], g_s[None], r_s[None])
```

```python
import functools

import numpy as np
import jax
import jax.numpy as jnp
from jax import lax
from jax.experimental import pallas as pl
from jax.experimental.pallas import tpu as pltpu

F32 = jnp.float32
BF16 = jnp.bfloat16

D_MODEL = 1024
CHUNK = 64
GDN_HEADS = 8
GDN_DK = 64
GDN_QK = GDN_HEADS * GDN_DK
GDN_CONV_CH = 3 * GDN_QK
CONV_W = 4
HG_HEADS = 4
HG_DK = 128
HG_QK = HG_HEADS * HG_DK
EPS = 1e-6

LANES = 128
SUBLANES = 8
MXU_DIM = 256
SLAB = 512
OUT_ROWS = 128
GATE_COL = GDN_CONV_CH + GDN_QK + 4 * HG_QK
N_PROJ = GATE_COL + LANES
GDN_HALF = MXU_DIM
GDN_HALVES = GDN_QK // GDN_HALF
VMEM_LIMIT = 48 * 1024 * 1024


def _sigmoid(x):
    return 1.0 / (1.0 + jnp.exp(-x))


def _split3(x):
    hi = x.astype(BF16)
    r = x - hi.astype(F32)
    mid = r.astype(BF16)
    lo = (r - mid.astype(F32)).astype(BF16)
    return hi, mid, lo


def _dot(a, b):
    return jnp.dot(a, b, preferred_element_type=F32)


def _dot_nt(a, b):
    return lax.dot_general(a, b, (((1,), (1,)), ((), ())), preferred_element_type=F32)


def _dot_tn(a, b):
    return lax.dot_general(a, b, (((0,), (0,)), ((), ())), preferred_element_type=F32)


def _sel_left(sel01, x):
    hi, mid, lo = _split3(x)
    return _dot(sel01, hi) + _dot(sel01, mid) + _dot(sel01, lo)


def _sel_right(x, sel01):
    hi, mid, lo = _split3(x)
    return _dot(hi, sel01) + _dot(mid, sel01) + _dot(lo, sel01)


def _batch_spec(shape, index):
    return pl.BlockSpec(shape, lambda b, t: (b,) + tuple(index(b, t)))


def _const_spec(shape):
    return pl.BlockSpec(shape, lambda b, t: (0,) * len(shape))


def _inproj_kernel(x_ref, nw_ref, w_ref, cw_ref, cs_ref, proj_ref, nc_ref, carry, *, n_valid):
    t = pl.program_id(1)
    bb, T = x_ref.shape[0], x_ref.shape[1]

    @pl.when(t == 0)
    def _():
        carry[...] = cs_ref[...]

    x = x_ref[...].reshape(bb * T, D_MODEL)
    h = x * lax.rsqrt(jnp.mean(x * x, axis=-1, keepdims=True) + EPS) * nw_ref[...]
    hb = h.astype(BF16)

    assert CONV_W == 4
    for s in range(0, GDN_CONV_CH, SLAB):
        sl = slice(s, s + SLAB)
        u_all = _dot(hb, w_ref[:, sl])
        for bi in range(bb):
            u = u_all[bi * T:(bi + 1) * T]
            full = jnp.concatenate([carry[bi, :, sl], u], axis=0)
            prev = pltpu.roll(full, 1, 0)
            pair = cw_ref[1:2, sl] * full + cw_ref[0:1, sl] * prev
            y = (cw_ref[3:4, sl] * full + cw_ref[2:3, sl] * prev + pltpu.roll(pair, 2, 0))[SUBLANES:]
            proj_ref[bi, :, sl] = y * _sigmoid(y)
            nc_ref[bi, :, sl] = full[n_valid:n_valid + SUBLANES]
            carry[bi, :, sl] = full[T:T + SUBLANES]
    for s in list(range(GDN_CONV_CH, GATE_COL, SLAB)) + [GATE_COL]:
        sl = slice(s, min(s + SLAB, N_PROJ))
        res = _dot(hb, w_ref[:, sl])
        for bi in range(bb):
            proj_ref[bi, :, sl] = res[bi * T:(bi + 1) * T]


def _inproj(x, norm_w, w_in_r, conv_w, conv_state8, *, bb, tile, n_valid):
    B, L, _ = x.shape
    kern = functools.partial(_inproj_kernel, n_valid=n_valid)
    return pl.pallas_call(
        kern,
        grid=(B // bb, L // tile),
        in_specs=[
            _batch_spec((bb, tile, D_MODEL), lambda b, t: (t, 0)),
            _const_spec((1, D_MODEL)),
            _const_spec((D_MODEL, N_PROJ)),
            _const_spec((CONV_W, GDN_CONV_CH)),
            _batch_spec((bb, SUBLANES, GDN_CONV_CH), lambda b, t: (0, 0)),
        ],
        out_specs=[
            _batch_spec((bb, tile, N_PROJ), lambda b, t: (t, 0)),
            _batch_spec((bb, SUBLANES, GDN_CONV_CH), lambda b, t: (0, 0)),
        ],
        out_shape=[
            jax.ShapeDtypeStruct((B, L, N_PROJ), F32),
            jax.ShapeDtypeStruct((B, SUBLANES, GDN_CONV_CH), F32),
        ],
        scratch_shapes=[pltpu.VMEM((bb, SUBLANES, GDN_CONV_CH), F32)],
        compiler_params=pltpu.CompilerParams(
            dimension_semantics=("arbitrary", "arbitrary"), vmem_limit_bytes=VMEM_LIMIT),
        name="inproj_conv",
    )(x, norm_w, w_in_r, conv_w, conv_state8)


def _consts():
    ltri = np.tril(np.ones((CHUNK, CHUNK), np.float32))
    bd = np.kron(np.eye(GDN_HALF // GDN_DK, dtype=np.float32), np.ones((GDN_DK, GDN_DK), np.float32))
    return (jnp.asarray(ltri, BF16), jnp.asarray(bd, BF16))


def _gdn_body(q_ref, k_ref, v_ref, z_ref, g_ref, s0_ref, alog_ref, dtb_ref, nw_ref, ltri_ref, bd_ref,
              o_ref, sn_ref, s_ref, *, n_chunks, n_valid):
    has_state = s0_ref is not None
    c = CHUNK
    W = GDN_HALF
    bb = q_ref.shape[0]
    t = pl.program_id(1)
    bd_mask = ((lax.broadcasted_iota(jnp.int32, (W, W), 0) & -GDN_DK)
               == (lax.broadcasted_iota(jnp.int32, (W, W), 1) & -GDN_DK))

    @pl.when(t == 0)
    def _():
        if has_state:
            for bi in range(bb):
                for a in range(GDN_HALVES):
                    wide = jnp.concatenate([s0_ref[bi, a]] * (W // GDN_DK), axis=1)
                    s_ref[bi, a] = jnp.where(bd_mask, wide, 0.0).T
        else:
            s_ref[...] = jnp.zeros(s_ref.shape, F32)

    row = lax.broadcasted_iota(jnp.int32, (c, W), 0)
    col = lax.broadcasted_iota(jnp.int32, (c, W), 1) & (GDN_DK - 1)
    eye_t = col == row
    causal_t = col <= row
    strict_t = col < row
    eye_f = eye_t.astype(F32)
    grow_ = lax.broadcasted_iota(jnp.int32, (c, LANES), 0)
    glane = lax.broadcasted_iota(jnp.int32, (c, LANES), 1)
    bd01 = bd_ref[...]
    neg_a = -jnp.exp(alog_ref[...])
    ltri = ltri_ref[...]

    def block_diag(x_b):
        return jnp.concatenate([x_b] * (W // c), axis=0) * bd01

    groups = [(bi, ci) for bi in range(bb) for ci in range(n_chunks)]
    items = [(bi, ci, a) for (bi, ci) in groups for a in range(GDN_HALVES)]
    rows = lambda ci: slice(ci * c, (ci + 1) * c)
    lanes = lambda a: slice(a * W, (a + 1) * W)

    gbs = {}
    for (bi, ci) in groups:
        gz = g_ref[bi, rows(ci), :]
        xg = gz + dtb_ref[...]
        softplus = jnp.maximum(xg, 0.0) + jnp.log(1.0 + jnp.exp(-jnp.abs(xg)))
        gb = jnp.where(glane < GDN_HEADS, _sigmoid(gz),
                       jnp.where(glane < 2 * GDN_HEADS, neg_a * softplus, 0.0))
        if n_valid is not None:
            gb = jnp.where(grow_ + ci * c < n_valid, gb, 0.0)
        gbs[(bi, ci)] = gb
    gcums = {g: _sel_left(ltri, gbs[g]) for g in groups}
    lane_w = lax.broadcasted_iota(jnp.int32, (c, LANES), 1)

    def to_head_lanes(x, first):
        slabs = []
        for p in range(GDN_QK // LANES):
            even = jnp.broadcast_to(x[:, first + 2 * p:first + 2 * p + 1], (c, LANES))
            odd = jnp.broadcast_to(x[:, first + 2 * p + 1:first + 2 * p + 2], (c, LANES))
            slabs.append(jnp.where(lane_w < GDN_DK, even, odd))
        return jnp.concatenate(slabs, axis=1)

    bxs = {g: to_head_lanes(gbs[g], 0) for g in groups}
    gws = {g: to_head_lanes(gcums[g], GDN_HEADS) for g in groups}

    st = {}
    for (bi, ci, a) in items:
        qp = q_ref[bi, rows(ci), lanes(a)]
        kp = k_ref[bi, rows(ci), lanes(a)]
        ss = _dot(jnp.concatenate([qp * qp, kp * kp], axis=0).astype(BF16), bd01)
        st[(bi, ci, a)] = dict(qp=qp, kp=kp, ss=ss)
    for (bi, ci, a) in items:
        d = st[(bi, ci, a)]
        qp, kp, ss = d["qp"], d["kp"], d["ss"]
        vp = v_ref[bi, rows(ci), lanes(a)]
        qn = qp * lax.rsqrt(ss[:c] + EPS) * (GDN_DK ** -0.5)
        kn = kp * lax.rsqrt(ss[c:] + EPS)
        bp = bxs[(bi, ci)][:, lanes(a)]
        gp = gws[(bi, ci)][:, lanes(a)]
        glast = gp[c - 1:c, :]
        e_g = jnp.exp(gp)
        kb = kn * bp
        gdiag = jnp.sum(jnp.where(eye_t, gp, 0.0), axis=0, keepdims=True)
        dec = jnp.where(causal_t, jnp.exp(jnp.minimum(gp - gdiag, 0.0)), 0.0)
        kq = _dot_nt(jnp.concatenate([kb, qn], axis=0).astype(BF16), block_diag(kn.astype(BF16)))
        a_neg = jnp.where(strict_t, -(kq[:c] * dec), 0.0)
        st[(bi, ci, a)] = dict(
            a_neg=a_neg, attn=(kq[c:] * dec).astype(BF16), vb=(vp * bp).astype(BF16),
            kbg=(kb * e_g).astype(BF16), qg=qn * e_g, kdec=(kn * jnp.exp(glast - gp)).astype(BF16),
            eglast=jnp.exp(glast))

    for it in items:
        a_b = st[it]["a_neg"].astype(BF16)
        st[it]["pw"] = _dot(a_b, block_diag(a_b))
        st[it]["xk"] = eye_f + st[it]["a_neg"]
    span = 4
    while span < c:
        for it in items:
            pw, xk = st[it]["pw"], st[it]["xk"]
            both = _dot(jnp.concatenate([pw, xk], axis=0).astype(BF16), block_diag(pw.astype(BF16)))
            st[it]["pw"] = both[:c]
            st[it]["xk"] = xk + both[c:]
        span *= 2
    for it in items:
        pw, xk = st[it]["pw"], st[it]["xk"]
        st[it]["t_b"] = (xk + _dot(xk.astype(BF16), block_diag(pw.astype(BF16)))).astype(BF16)
    for it in items:
        d = st[it]
        d["u_b"] = _dot(d["t_b"], block_diag(d["vb"])).astype(BF16)
        d["w_b"] = _dot(d["t_b"], block_diag(d["kbg"])).astype(BF16)

    for it in items:
        d = st[it]
        d["q_eff"] = (d["qg"] - _dot(d["attn"], block_diag(d["w_b"]))).astype(BF16)
        d["o_loc"] = _dot(d["attn"], block_diag(d["u_b"]))
        d["n_t"] = jnp.where(bd_mask, _dot_tn(d["u_b"], d["kdec"]), 0.0)
        d["m_t"] = jnp.where(bd_mask, -_dot_tn(d["w_b"], d["kdec"]), 0.0).astype(BF16)

    for ci in range(n_chunks):
        for bi in range(bb):
            for a in range(GDN_HALVES):
                d = st[(bi, ci, a)]
                s_old = s_ref[bi, a]
                s_b = s_old.astype(BF16)
                d["o"] = _dot_nt(d["q_eff"], s_b) + d["o_loc"]
                s_ref[bi, a] = s_old * d["eglast"] + d["n_t"] + _dot(s_b, d["m_t"])

    for it in items:
        st[it]["ms"] = _dot((st[it]["o"] * st[it]["o"]).astype(BF16), bd01) * (1.0 / GDN_DK)
    for (bi, ci, a) in items:
        d = st[(bi, ci, a)]
        zp = z_ref[bi, rows(ci), lanes(a)]
        o_ref[bi, rows(ci), lanes(a)] = (d["o"] * lax.rsqrt(d["ms"] + EPS) * nw_ref[...]
                                         * (zp * _sigmoid(zp))).astype(o_ref.dtype)

    @pl.when(t == pl.num_programs(1) - 1)
    def _():
        for bi in range(bb):
            for a in range(GDN_HALVES):
                s_bd = s_ref[bi, a].T
                f = s_bd[:, :LANES] + s_bd[:, LANES:]
                sn_ref[bi, a] = f[:, :GDN_DK] + f[:, GDN_DK:]


def _hgrn_levels():
    out = []
    m = 1
    while m < CHUNK:
        out.append(m)
        m *= 2
    return out


def _level_ref_rows(b, m):
    c = b.shape[0]
    bcast = lambda r, n: jnp.broadcast_to(b[r:r + 1, :], (n, LANES))
    if 2 * m >= SUBLANES:
        parts = [bcast(blk * 2 * m + m, 2 * m) for blk in range(c // (2 * m))]
    else:
        sub = lax.broadcasted_iota(jnp.int32, (SUBLANES, LANES), 0)
        parts = []
        for g in range(c // SUBLANES):
            acc = bcast(g * SUBLANES + m, SUBLANES)
            for blk in range(1, SUBLANES // (2 * m)):
                acc = jnp.where(sub >= blk * 2 * m, bcast(g * SUBLANES + blk * 2 * m + m, SUBLANES), acc)
            parts.append(acc)
    return parts[0] if len(parts) == 1 else jnp.concatenate(parts, axis=0)


def _hgrn_body(q_ref, f_ref, i_ref, z_ref, s0_ref, lb_ref, nw_ref, ltri_ref, o_ref, sn_ref, s_ref,
               *, n_chunks, n_valid):
    has_state = s0_ref is not None
    c = CHUNK
    bb = q_ref.shape[0]
    t = pl.program_id(1)

    @pl.when(t == 0)
    def _():
        if has_state:
            for bi in range(bb):
                for h in range(HG_HEADS):
                    s_ref[bi, h] = s0_ref[bi, h].T
        else:
            s_ref[...] = jnp.zeros(s_ref.shape, F32)

    row = lax.broadcasted_iota(jnp.int32, (c, LANES), 0)
    ri = lax.broadcasted_iota(jnp.int32, (c, c), 0)
    ci_ = lax.broadcasted_iota(jnp.int32, (c, c), 1)
    levels = _hgrn_levels()
    ltri = ltri_ref[...]
    lb = lb_ref[...]
    eye_f = (ri == ci_).astype(F32)
    upper = [(row & m) != 0 for m in levels]
    keep_f = [(((ri & -(2 * m)) == (ci_ & -(2 * m))) & ((ri & m) != 0) & ((ci_ & m) == 0)).astype(F32)
              for m in levels]

    groups = [(bi, ci) for bi in range(bb) for ci in range(n_chunks)]
    items = [(bi, ci, h) for (bi, ci) in groups for h in range(HG_HEADS)]
    rows = lambda ci: slice(ci * c, (ci + 1) * c)
    lanes = lambda h: slice(h * LANES, (h + 1) * LANES)

    kks, bs = {}, {}
    for (bi, ci) in groups:
        f = lb + (1.0 - lb) * _sigmoid(f_ref[bi, rows(ci), :])
        logf = jnp.log(f)
        kk = 1.0 - f
        if n_valid is not None:
            valid = lax.broadcasted_iota(jnp.int32, (c, HG_QK), 0) + ci * c < n_valid
            logf = jnp.where(valid, logf, 0.0)
            kk = jnp.where(valid, kk, 0.0)
        kks[(bi, ci)] = kk
        bs[(bi, ci)] = _sel_left(ltri, logf)

    st = {}
    for (bi, ci, h) in items:
        hq = q_ref[bi, rows(ci), lanes(h)]
        q = hq * _sigmoid(hq)
        k = kks[(bi, ci)][:, lanes(h)]
        v = i_ref[bi, rows(ci), lanes(h)]
        if n_valid is not None:
            v = jnp.where(row + ci * c < n_valid, v, 0.0)
        b = bs[(bi, ci)][:, lanes(h)]
        attn = eye_f * _dot_nt(q.astype(BF16), k.astype(BF16))
        for li, m in enumerate(levels):
            e = jnp.exp(-jnp.abs(b - _level_ref_rows(b, m)))
            z_b = (jnp.where(upper[li], q, k) * e).astype(BF16)
            attn = attn + keep_f[li] * _dot_nt(z_b, z_b)
        blast = b[c - 1:c, :]
        st[(bi, ci, h)] = dict(
            qe=(q * jnp.exp(b)).astype(BF16), attn=attn.astype(BF16), v_b=v.astype(BF16),
            kdec=(k * jnp.exp(blast - b)).astype(BF16), eblast=jnp.exp(blast))
    for d in st.values():
        d["av"] = _dot(d["attn"], d["v_b"])
        d["upd"] = _dot_tn(d["v_b"], d["kdec"])

    for ci in range(n_chunks):
        for bi in range(bb):
            for h in range(HG_HEADS):
                d = st[(bi, ci, h)]
                s_old = s_ref[bi, h]
                o = _dot_nt(d["qe"], s_old.astype(BF16)) + d["av"]
                s_ref[bi, h] = s_old * d["eblast"] + d["upd"]

                ms = jnp.mean(o * o, axis=-1, keepdims=True)
                zp = z_ref[bi, rows(ci), lanes(h)]
                o_ref[bi, rows(ci), lanes(h)] = (o * lax.rsqrt(ms + EPS) * nw_ref[...]
                                                 * (zp * _sigmoid(zp))).astype(o_ref.dtype)

    @pl.when(t == pl.num_programs(1) - 1)
    def _():
        for bi in range(bb):
            for h in range(HG_HEADS):
                sn_ref[bi, h] = s_ref[bi, h].T


def _outproj_body(x_ref, oa_ref, ob_ref, w_ref, fw_ref, y_ref):
    bb, T = x_ref.shape[0], x_ref.shape[1]
    rows = min(T, OUT_ROWS)
    for bi in range(bb):
        for r in range(0, T, rows):
            o = jnp.concatenate([oa_ref[bi, r:r + rows, :], ob_ref[bi, r:r + rows, :]], axis=1)
            y = x_ref[bi, r:r + rows, :] + _dot(o, w_ref[...])
            y_ref[bi, r:r + rows, :] = (y * lax.rsqrt(jnp.mean(y * y, axis=-1, keepdims=True) + EPS)
                                        * fw_ref[...])


def _mixer_kernel(*refs, n_chunks, n_valid, has_state):
    refs = list(refs)
    q_ref, k_ref, v_ref, za_ref, hq_ref, hf_ref, hi_ref, zb_ref, g_ref, x_ref = refs[:10]
    del refs[:10]
    sg0_ref = sh0_ref = None
    if has_state:
        sg0_ref, sh0_ref = refs[:2]
        del refs[:2]
    (alog_ref, dtb_ref, gnw_ref, lb_ref, hnw_ref, ltri_ref, bd_ref, w_ref, fw_ref,
     y_ref, sgn_ref, shn_ref, sg_ref, sh_ref, oa_ref, ob_ref) = refs
    _gdn_body(q_ref, k_ref, v_ref, za_ref, g_ref, sg0_ref, alog_ref, dtb_ref, gnw_ref, ltri_ref, bd_ref,
              oa_ref, sgn_ref, sg_ref, n_chunks=n_chunks, n_valid=n_valid)
    _hgrn_body(hq_ref, hf_ref, hi_ref, zb_ref, sh0_ref, lb_ref, hnw_ref, ltri_ref,
               ob_ref, shn_ref, sh_ref, n_chunks=n_chunks, n_valid=n_valid)
    _outproj_body(x_ref, oa_ref, ob_ref, w_ref, fw_ref, y_ref)


def _mixer(proj, x, s_gdn, s_hg, params, consts, *, bb, n_chunks, n_valid):
    (_, _, _, alog_row, dtb_row, gdn_nw_row, lb_row, hg_nw_row, w_out_b, fw_row) = params
    ltri, bd = consts
    B, L, _ = proj.shape
    tile = n_chunks * CHUNK
    has_state = s_gdn is not None
    kern = functools.partial(_mixer_kernel, n_chunks=n_chunks, n_valid=n_valid, has_state=has_state)
    col = lambda j: _batch_spec((bb, tile, SLAB), lambda b, t: (t, j))
    gdn_state_shape = (B, GDN_HALVES, GDN_HALF, GDN_DK)
    gdn_state = _batch_spec((bb,) + gdn_state_shape[1:], lambda b, t: (0, 0, 0))
    hg_state = _batch_spec((bb, HG_HEADS, HG_DK, HG_DK), lambda b, t: (0, 0, 0))
    const_args = [alog_row, dtb_row, gdn_nw_row, lb_row, hg_nw_row, ltri, bd, w_out_b, fw_row]
    y, sgn, shn = pl.pallas_call(
        kern,
        grid=(B // bb, L // tile),
        in_specs=[col(j) for j in range(8)]
                 + [_batch_spec((bb, tile, LANES), lambda b, t: (t, GATE_COL // LANES)),
                    _batch_spec((bb, tile, D_MODEL), lambda b, t: (t, 0))]
                 + ([gdn_state, hg_state] if has_state else [])
                 + [_const_spec(a.shape) for a in const_args],
        out_specs=[_batch_spec((bb, tile, D_MODEL), lambda b, t: (t, 0)), gdn_state, hg_state],
        out_shape=[
            jax.ShapeDtypeStruct((B, L, D_MODEL), F32),
            jax.ShapeDtypeStruct(gdn_state_shape, F32),
            jax.ShapeDtypeStruct((B, HG_HEADS, HG_DK, HG_DK), F32),
        ],
        scratch_shapes=[
            pltpu.VMEM((bb, GDN_HALVES, GDN_HALF, GDN_HALF), F32),
            pltpu.VMEM((bb, HG_HEADS, HG_DK, HG_DK), F32),
            pltpu.VMEM((bb, tile, GDN_QK), BF16),
            pltpu.VMEM((bb, tile, HG_QK), BF16),
        ],
        compiler_params=pltpu.CompilerParams(
            dimension_semantics=("arbitrary", "arbitrary"), vmem_limit_bytes=VMEM_LIMIT),
        name="mixers_outproj",
    )(*([proj] * 9 + [x] + ([s_gdn.reshape(gdn_state_shape), s_hg] if has_state else []) + const_args))
    return y, sgn.reshape(B, GDN_HEADS, GDN_DK, GDN_DK), shn


def _layer(x, conv_state, s_gdn, s_hg, params, consts, *, n_valid, bb, in_tile, n_chunks):
    norm_w, w_in_r, conv_w = params[:3]
    B, L, _ = x.shape
    if conv_state is None:
        cs8 = jnp.zeros((B, SUBLANES, GDN_CONV_CH), F32)
    else:
        cs8 = jnp.pad(conv_state, ((0, 0), (SUBLANES - (CONV_W - 1), 0), (0, 0)))
    proj, nc8 = _inproj(x, norm_w, w_in_r, conv_w, cs8, bb=bb, tile=in_tile,
                        n_valid=in_tile if n_valid is None else n_valid)
    new_conv = nc8[:, SUBLANES - (CONV_W - 1):, :]

    y, new_gdn, new_hg = _mixer(proj, x, s_gdn, s_hg, params, consts, bb=bb, n_chunks=n_chunks, n_valid=n_valid)
    return y, new_conv, new_gdn, new_hg


def _prep(norm_w, w_in, conv_w, gdn_A_log, gdn_dt_bias, gdn_norm_w, hgrn_lb_logits, hgrn_norm_w, w_out,
          final_norm_w):
    w = w_in[0]
    o_qkv, o_za = 0, GDN_CONV_CH
    o_b = o_za + GDN_QK
    o_a = o_b + GDN_HEADS
    o_hq = o_a + GDN_HEADS
    gate_cols = jnp.pad(w[:, o_b:o_hq], ((0, 0), (0, LANES - 2 * GDN_HEADS)))
    w_in_r = jnp.concatenate([w[:, o_qkv:o_b], w[:, o_hq:], gate_cols], axis=1).astype(BF16)

    pad_row = lambda v: jnp.pad(v.astype(F32), (GDN_HEADS, LANES - 2 * GDN_HEADS))[None, :]
    lb = jnp.cumsum(jax.nn.softmax(hgrn_lb_logits.astype(F32), axis=0), axis=0)[0]
    params = (norm_w[0][None, :], w_in_r, conv_w[0], pad_row(gdn_A_log[0]), pad_row(gdn_dt_bias[0]),
              jnp.tile(gdn_norm_w[0], GDN_HALF // GDN_DK)[None, :], lb[None, :], hgrn_norm_w[0][None, :],
              w_out[0].astype(BF16), final_norm_w[None, :])
    return params, _consts()


def kernel(x_prompt, x_sample, state_conv, state_gdn, state_hgrn, norm_w, w_in, conv_w, gdn_A_log,
           gdn_dt_bias, gdn_norm_w, hgrn_lb_logits, hgrn_norm_w, w_out, final_norm_w):
    Bs, Ls, _ = x_sample.shape
    params, consts = _prep(norm_w, w_in, conv_w, gdn_A_log, gdn_dt_bias, gdn_norm_w, hgrn_lb_logits,
                           hgrn_norm_w, w_out, final_norm_w)

    y_p, c_p, g_p, r_p = _layer(x_prompt, None, None, None, params, consts,
                                n_valid=None, bb=1, in_tile=512, n_chunks=8)

    xs = jnp.pad(x_sample, ((0, 0), (0, CHUNK - Ls), (0, 0)))
    y_s, c_s, g_s, r_s = _layer(xs, state_conv[0], state_gdn[0], state_hgrn[0], params, consts,
                                n_valid=Ls, bb=4, in_tile=CHUNK, n_chunks=1)
    y_s = y_s[:, :Ls]

    return (y_p, y_s, c_p[None], g_p[None], r_p[None], c_s[None], g_s[None], r_s[None])
```

```python
import functools

import numpy as np
import jax
import jax.numpy as jnp
from jax import lax
from jax.experimental import pallas as pl
from jax.experimental.pallas import tpu as pltpu

F32 = jnp.float32
BF16 = jnp.bfloat16

D_MODEL = 1024
CHUNK = 64
GDN_HEADS = 8
GDN_DK = 64
GDN_QK = GDN_HEADS * GDN_DK
GDN_CONV_CH = 3 * GDN_QK
CONV_W = 4
HG_HEADS = 4
HG_DK = 128
HG_QK = HG_HEADS * HG_DK
EPS = 1e-6
LOG2E = 1.4426950408889634

LANES = 128
SUBLANES = 8
MXU_DIM = 256
SLAB = 512
OUT_ROWS = 128
GATE_COL = GDN_CONV_CH + GDN_QK + 4 * HG_QK
N_PROJ = GATE_COL + LANES
GDN_HALF = MXU_DIM
GDN_HALVES = GDN_QK // GDN_HALF
VMEM_LIMIT = 48 * 1024 * 1024


def _sigmoid(x):
    return 1.0 / (1.0 + jnp.exp(-x))


def _split3(x):
    hi = x.astype(BF16)
    r = x - hi.astype(F32)
    mid = r.astype(BF16)
    lo = (r - mid.astype(F32)).astype(BF16)
    return hi, mid, lo


def _dot(a, b):
    return jnp.dot(a, b, preferred_element_type=F32)


def _dot_nt(a, b):
    return lax.dot_general(a, b, (((1,), (1,)), ((), ())), preferred_element_type=F32)


def _dot_tn(a, b):
    return lax.dot_general(a, b, (((0,), (0,)), ((), ())), preferred_element_type=F32)


def _sel_left(sel01, x):
    hi, mid, lo = _split3(x)
    return _dot(sel01, hi) + _dot(sel01, mid) + _dot(sel01, lo)


def _sel_right(x, sel01):
    hi, mid, lo = _split3(x)
    return _dot(hi, sel01) + _dot(mid, sel01) + _dot(lo, sel01)


def _batch_spec(shape, index):
    return pl.BlockSpec(shape, lambda b, t: (b,) + tuple(index(b, t)))


def _const_spec(shape):
    return pl.BlockSpec(shape, lambda b, t: (0,) * len(shape))


def _inproj_kernel(x_ref, nw_ref, w_ref, cw_ref, cs_ref, proj_ref, nc_ref, carry, *stages, n_valid):
    t = pl.program_id(1)
    bb, T = x_ref.shape[0], x_ref.shape[1]

    @pl.when(t == 0)
    def _():
        carry[...] = cs_ref[...]

    x = x_ref[...].reshape(bb * T, D_MODEL)
    h = x * lax.rsqrt(jnp.mean(x * x, axis=-1, keepdims=True) + EPS) * nw_ref[...]
    hb = h.astype(BF16)

    def conv(i):
        sl = slice(i * SLAB, (i + 1) * SLAB)
        for bi in range(bb):
            u = stages[i][bi * T:(bi + 1) * T, :]
            full = jnp.concatenate([carry[bi, :, sl], u], axis=0)
            prev = pltpu.roll(full, 1, 0)
            pair = cw_ref[1:2, sl] * full + cw_ref[0:1, sl] * prev
            y = (cw_ref[3:4, sl] * full + cw_ref[2:3, sl] * prev + pltpu.roll(pair, 2, 0))[SUBLANES:]
            proj_ref[bi, :, sl] = y * _sigmoid(y)
            nc_ref[bi, :, sl] = full[n_valid:n_valid + SUBLANES]
            carry[bi, :, sl] = full[T:T + SUBLANES]

    assert CONV_W == 4
    n_conv = GDN_CONV_CH // SLAB
    starts = list(range(0, GATE_COL, SLAB)) + [GATE_COL]
    for j, s in enumerate(starts):
        sl = slice(s, min(s + SLAB, N_PROJ))
        res = _dot(hb, w_ref[:, sl])
        if j < n_conv:
            stages[j][...] = res
        else:
            for bi in range(bb):
                proj_ref[bi, :, sl] = res[bi * T:(bi + 1) * T]
        if 1 <= j <= n_conv:
            conv(j - 1)


def _inproj(x, norm_w, w_in_r, conv_w, conv_state8, *, bb, tile, n_valid):
    B, L, _ = x.shape
    kern = functools.partial(_inproj_kernel, n_valid=n_valid)
    return pl.pallas_call(
        kern,
        grid=(B // bb, L // tile),
        in_specs=[
            _batch_spec((bb, tile, D_MODEL), lambda b, t: (t, 0)),
            _const_spec((1, D_MODEL)),
            _const_spec((D_MODEL, N_PROJ)),
            _const_spec((CONV_W, GDN_CONV_CH)),
            _batch_spec((bb, SUBLANES, GDN_CONV_CH), lambda b, t: (0, 0)),
        ],
        out_specs=[
            _batch_spec((bb, tile, N_PROJ), lambda b, t: (t, 0)),
            _batch_spec((bb, SUBLANES, GDN_CONV_CH), lambda b, t: (0, 0)),
        ],
        out_shape=[
            jax.ShapeDtypeStruct((B, L, N_PROJ), F32),
            jax.ShapeDtypeStruct((B, SUBLANES, GDN_CONV_CH), F32),
        ],
        scratch_shapes=[pltpu.VMEM((bb, SUBLANES, GDN_CONV_CH), F32)]
                       + [pltpu.VMEM((bb * tile, SLAB), F32) for _ in range(GDN_CONV_CH // SLAB)],
        compiler_params=pltpu.CompilerParams(
            dimension_semantics=("arbitrary", "arbitrary"), vmem_limit_bytes=VMEM_LIMIT),
        name="inproj_conv",
    )(x, norm_w, w_in_r, conv_w, conv_state8)


def _consts():
    ltri = np.tril(np.ones((CHUNK, CHUNK), np.float32))
    bd = np.kron(np.eye(GDN_HALF // GDN_DK, dtype=np.float32), np.ones((GDN_DK, GDN_DK), np.float32))
    return (jnp.asarray(ltri, BF16), jnp.asarray(bd, BF16))


def _gdn_body(q_ref, k_ref, v_ref, z_ref, g_ref, s0_ref, alog_ref, dtb_ref, nw_ref, ltri_ref, bd_ref,
              o_ref, sn_ref, s_ref, *, n_chunks, n_valid):
    has_state = s0_ref is not None
    c = CHUNK
    W = GDN_HALF
    bb = q_ref.shape[0]
    t = pl.program_id(1)
    bd_mask = ((lax.broadcasted_iota(jnp.int32, (W, W), 0) & -GDN_DK)
               == (lax.broadcasted_iota(jnp.int32, (W, W), 1) & -GDN_DK))

    @pl.when(t == 0)
    def _():
        if has_state:
            for bi in range(bb):
                for a in range(GDN_HALVES):
                    wide = jnp.concatenate([s0_ref[bi, a]] * (W // GDN_DK), axis=1)
                    s_ref[bi, a] = jnp.where(bd_mask, wide, 0.0).T
        else:
            s_ref[...] = jnp.zeros(s_ref.shape, F32)

    row = lax.broadcasted_iota(jnp.int32, (c, W), 0)
    col = lax.broadcasted_iota(jnp.int32, (c, W), 1) & (GDN_DK - 1)
    eye_t = col == row
    causal_t = col <= row
    strict_t = col < row
    eye_f = eye_t.astype(F32)
    grow_ = lax.broadcasted_iota(jnp.int32, (c, LANES), 0)
    glane = lax.broadcasted_iota(jnp.int32, (c, LANES), 1)
    bd01 = bd_ref[...]
    neg_a = -jnp.exp(alog_ref[...]) * LOG2E
    ltri = ltri_ref[...]

    def block_diag(x_b):
        return jnp.concatenate([x_b] * (W // c), axis=0) * bd01

    groups = [(bi, ci) for bi in range(bb) for ci in range(n_chunks)]
    items = [(bi, ci, a) for (bi, ci) in groups for a in range(GDN_HALVES)]
    rows = lambda ci: slice(ci * c, (ci + 1) * c)
    lanes = lambda a: slice(a * W, (a + 1) * W)

    gbs = {}
    for (bi, ci) in groups:
        gz = g_ref[bi, rows(ci), :]
        xg = gz + dtb_ref[...]
        softplus = jnp.maximum(xg, 0.0) + jnp.log(1.0 + jnp.exp(-jnp.abs(xg)))
        gb = jnp.where(glane < GDN_HEADS, _sigmoid(gz),
                       jnp.where(glane < 2 * GDN_HEADS, neg_a * softplus, 0.0))
        if n_valid is not None:
            gb = jnp.where(grow_ + ci * c < n_valid, gb, 0.0)
        gbs[(bi, ci)] = gb
    gcums = {g: _sel_left(ltri, gbs[g]) for g in groups}
    lane_w = lax.broadcasted_iota(jnp.int32, (c, LANES), 1)

    def to_head_lanes(x, first):
        slabs = []
        for p in range(GDN_QK // LANES):
            even = jnp.broadcast_to(x[:, first + 2 * p:first + 2 * p + 1], (c, LANES))
            odd = jnp.broadcast_to(x[:, first + 2 * p + 1:first + 2 * p + 2], (c, LANES))
            slabs.append(jnp.where(lane_w < GDN_DK, even, odd))
        return jnp.concatenate(slabs, axis=1)

    bxs = {g: to_head_lanes(gbs[g], 0) for g in groups}
    gws = {g: to_head_lanes(gcums[g], GDN_HEADS) for g in groups}

    st = {}
    for (bi, ci, a) in items:
        qp = q_ref[bi, rows(ci), lanes(a)]
        kp = k_ref[bi, rows(ci), lanes(a)]
        ss = _dot(jnp.concatenate([qp * qp, kp * kp], axis=0).astype(BF16), bd01)
        st[(bi, ci, a)] = dict(qp=qp, kp=kp, ss=ss)
    for (bi, ci, a) in items:
        d = st[(bi, ci, a)]
        qp, kp, ss = d["qp"], d["kp"], d["ss"]
        vp = v_ref[bi, rows(ci), lanes(a)]
        qn = qp * lax.rsqrt(ss[:c] + EPS) * (GDN_DK ** -0.5)
        kn = kp * lax.rsqrt(ss[c:] + EPS)
        bp = bxs[(bi, ci)][:, lanes(a)]
        gp = gws[(bi, ci)][:, lanes(a)]
        glast = gp[c - 1:c, :]
        e_g = jnp.exp2(gp)
        kb = kn * bp
        gdiag = jnp.sum(jnp.where(eye_t, gp, 0.0), axis=0, keepdims=True)
        dec = jnp.where(causal_t, jnp.exp2(jnp.minimum(gp - gdiag, 0.0)), 0.0)
        kq = _dot_nt(jnp.concatenate([kb, qn], axis=0).astype(BF16), block_diag(kn.astype(BF16)))
        a_neg = jnp.where(strict_t, -(kq[:c] * dec), 0.0)
        st[(bi, ci, a)] = dict(
            a_neg=a_neg, attn=(kq[c:] * dec).astype(BF16), vb=(vp * bp).astype(BF16),
            kbg=(kb * e_g).astype(BF16), qg=qn * e_g, kdec=(kn * jnp.exp2(glast - gp)).astype(BF16),
            eglast=jnp.exp2(glast))

    for it in items:
        a_b = st[it]["a_neg"].astype(BF16)
        st[it]["pw"] = _dot(a_b, block_diag(a_b))
        st[it]["xk"] = eye_f + st[it]["a_neg"]
    span = 4
    while span < c:
        for it in items:
            pw, xk = st[it]["pw"], st[it]["xk"]
            both = _dot(jnp.concatenate([pw, xk], axis=0).astype(BF16), block_diag(pw.astype(BF16)))
            st[it]["pw"] = both[:c]
            st[it]["xk"] = xk + both[c:]
        span *= 2
    for it in items:
        pw, xk = st[it]["pw"], st[it]["xk"]
        st[it]["t_b"] = (xk + _dot(xk.astype(BF16), block_diag(pw.astype(BF16)))).astype(BF16)
    for it in items:
        d = st[it]
        d["u_b"] = _dot(d["t_b"], block_diag(d["vb"])).astype(BF16)
        d["w_b"] = _dot(d["t_b"], block_diag(d["kbg"])).astype(BF16)

    for it in items:
        d = st[it]
        d["q_eff"] = (d["qg"] - _dot(d["attn"], block_diag(d["w_b"]))).astype(BF16)
        d["o_loc"] = _dot(d["attn"], block_diag(d["u_b"]))
        d["n_t"] = jnp.where(bd_mask, _dot_tn(d["u_b"], d["kdec"]), 0.0)
        d["m_t"] = jnp.where(bd_mask, -_dot_tn(d["w_b"], d["kdec"]), 0.0).astype(BF16)

    for ci in range(n_chunks):
        for bi in range(bb):
            for a in range(GDN_HALVES):
                d = st[(bi, ci, a)]
                s_old = s_ref[bi, a]
                s_b = s_old.astype(BF16)
                d["o"] = _dot_nt(d["q_eff"], s_b) + d["o_loc"]
                s_ref[bi, a] = s_old * d["eglast"] + d["n_t"] + _dot(s_b, d["m_t"])

    for it in items:
        st[it]["ms"] = _dot((st[it]["o"] * st[it]["o"]).astype(BF16), bd01) * (1.0 / GDN_DK)
    for (bi, ci, a) in items:
        d = st[(bi, ci, a)]
        zp = z_ref[bi, rows(ci), lanes(a)]
        o_ref[bi, rows(ci), lanes(a)] = (d["o"] * lax.rsqrt(d["ms"] + EPS) * nw_ref[...]
                                         * (zp * _sigmoid(zp))).astype(o_ref.dtype)

    @pl.when(t == pl.num_programs(1) - 1)
    def _():
        for bi in range(bb):
            for a in range(GDN_HALVES):
                s_bd = s_ref[bi, a].T
                f = s_bd[:, :LANES] + s_bd[:, LANES:]
                sn_ref[bi, a] = f[:, :GDN_DK] + f[:, GDN_DK:]


def _hgrn_levels():
    out = []
    m = 1
    while m < CHUNK:
        out.append(m)
        m *= 2
    return out


def _level_ref_rows(b, m):
    c = b.shape[0]
    bcast = lambda r, n: jnp.broadcast_to(b[r:r + 1, :], (n, LANES))
    if 2 * m >= SUBLANES:
        parts = [bcast(blk * 2 * m + m, 2 * m) for blk in range(c // (2 * m))]
    else:
        sub = lax.broadcasted_iota(jnp.int32, (SUBLANES, LANES), 0)
        parts = []
        for g in range(c // SUBLANES):
            acc = bcast(g * SUBLANES + m, SUBLANES)
            for blk in range(1, SUBLANES // (2 * m)):
                acc = jnp.where(sub >= blk * 2 * m, bcast(g * SUBLANES + blk * 2 * m + m, SUBLANES), acc)
            parts.append(acc)
    return parts[0] if len(parts) == 1 else jnp.concatenate(parts, axis=0)


def _hgrn_body(q_ref, f_ref, i_ref, z_ref, s0_ref, lb_ref, nw_ref, ltri_ref, o_ref, sn_ref, s_ref,
               *, n_chunks, n_valid):
    has_state = s0_ref is not None
    c = CHUNK
    bb = q_ref.shape[0]
    t = pl.program_id(1)

    @pl.when(t == 0)
    def _():
        if has_state:
            for bi in range(bb):
                for h in range(HG_HEADS):
                    s_ref[bi, h] = s0_ref[bi, h].T
        else:
            s_ref[...] = jnp.zeros(s_ref.shape, F32)

    row = lax.broadcasted_iota(jnp.int32, (c, LANES), 0)
    ri = lax.broadcasted_iota(jnp.int32, (c, c), 0)
    ci_ = lax.broadcasted_iota(jnp.int32, (c, c), 1)
    levels = _hgrn_levels()
    ltri = ltri_ref[...]
    lb = lb_ref[...]
    eye_f = (ri == ci_).astype(F32)
    upper = [(row & m) != 0 for m in levels]
    sign_f = [jnp.where(u, 1.0, -1.0) for u in upper]
    keep = [((ri & -(2 * m)) == (ci_ & -(2 * m))) & ((ri & m) != 0) & ((ci_ & m) == 0) for m in levels]

    groups = [(bi, ci) for bi in range(bb) for ci in range(n_chunks)]
    items = [(bi, ci, h) for (bi, ci) in groups for h in range(HG_HEADS)]
    rows = lambda ci: slice(ci * c, (ci + 1) * c)
    lanes = lambda h: slice(h * LANES, (h + 1) * LANES)

    kks, bs = {}, {}
    for (bi, ci) in groups:
        f = lb + (1.0 - lb) * _sigmoid(f_ref[bi, rows(ci), :])
        logf = jnp.log2(f)
        kk = 1.0 - f
        if n_valid is not None:
            valid = lax.broadcasted_iota(jnp.int32, (c, HG_QK), 0) + ci * c < n_valid
            logf = jnp.where(valid, logf, 0.0)
            kk = jnp.where(valid, kk, 0.0)
        kks[(bi, ci)] = kk
        bs[(bi, ci)] = _sel_left(ltri, logf)

    st = {}
    for (bi, ci, h) in items:
        hq = q_ref[bi, rows(ci), lanes(h)]
        q = hq * _sigmoid(hq)
        k = kks[(bi, ci)][:, lanes(h)]
        v = i_ref[bi, rows(ci), lanes(h)]
        if n_valid is not None:
            v = jnp.where(row + ci * c < n_valid, v, 0.0)
        b = bs[(bi, ci)][:, lanes(h)]
        attn = eye_f * _dot_nt(q.astype(BF16), k.astype(BF16))
        for li, m in enumerate(levels):
            e = jnp.exp2((b - _level_ref_rows(b, m)) * sign_f[li])
            z_b = (jnp.where(upper[li], q, k) * e).astype(BF16)
            attn = jnp.where(keep[li], _dot_nt(z_b, z_b), attn)
        blast = b[c - 1:c, :]
        st[(bi, ci, h)] = dict(
            qe=(q * jnp.exp2(b)).astype(BF16), attn=attn.astype(BF16), v_b=v.astype(BF16),
            kdec=(k * jnp.exp2(blast - b)).astype(BF16), eblast=jnp.exp2(blast))
    for d in st.values():
        d["av"] = _dot(d["attn"], d["v_b"])
        d["upd"] = _dot_tn(d["v_b"], d["kdec"])

    for ci in range(n_chunks):
        for bi in range(bb):
            for h in range(HG_HEADS):
                d = st[(bi, ci, h)]
                s_old = s_ref[bi, h]
                o = _dot_nt(d["qe"], s_old.astype(BF16)) + d["av"]
                s_ref[bi, h] = s_old * d["eblast"] + d["upd"]

                ms = jnp.mean(o * o, axis=-1, keepdims=True)
                zp = z_ref[bi, rows(ci), lanes(h)]
                o_ref[bi, rows(ci), lanes(h)] = (o * lax.rsqrt(ms + EPS) * nw_ref[...]
                                                 * (zp * _sigmoid(zp))).astype(o_ref.dtype)

    @pl.when(t == pl.num_programs(1) - 1)
    def _():
        for bi in range(bb):
            for h in range(HG_HEADS):
                sn_ref[bi, h] = s_ref[bi, h].T


def _outproj_body(x_ref, oa_ref, ob_ref, w_ref, fw_ref, y_ref):
    bb, T = x_ref.shape[0], x_ref.shape[1]
    rows = min(T, OUT_ROWS)
    for bi in range(bb):
        for r in range(0, T, rows):
            o = jnp.concatenate([oa_ref[bi, r:r + rows, :], ob_ref[bi, r:r + rows, :]], axis=1)
            y = x_ref[bi, r:r + rows, :] + _dot(o, w_ref[...])
            y_ref[bi, r:r + rows, :] = (y * lax.rsqrt(jnp.mean(y * y, axis=-1, keepdims=True) + EPS)
                                        * fw_ref[...])


def _mixer_kernel(*refs, n_chunks, n_valid, has_state):
    refs = list(refs)
    q_ref, k_ref, v_ref, za_ref, hq_ref, hf_ref, hi_ref, zb_ref, g_ref, x_ref = refs[:10]
    del refs[:10]
    sg0_ref = sh0_ref = None
    if has_state:
        sg0_ref, sh0_ref = refs[:2]
        del refs[:2]
    (alog_ref, dtb_ref, gnw_ref, lb_ref, hnw_ref, ltri_ref, bd_ref, w_ref, fw_ref,
     y_ref, sgn_ref, shn_ref, sg_ref, sh_ref, oa_ref, ob_ref) = refs
    _gdn_body(q_ref, k_ref, v_ref, za_ref, g_ref, sg0_ref, alog_ref, dtb_ref, gnw_ref, ltri_ref, bd_ref,
              oa_ref, sgn_ref, sg_ref, n_chunks=n_chunks, n_valid=n_valid)
    _hgrn_body(hq_ref, hf_ref, hi_ref, zb_ref, sh0_ref, lb_ref, hnw_ref, ltri_ref,
               ob_ref, shn_ref, sh_ref, n_chunks=n_chunks, n_valid=n_valid)
    _outproj_body(x_ref, oa_ref, ob_ref, w_ref, fw_ref, y_ref)


def _mixer(proj, x, s_gdn, s_hg, params, consts, *, bb, n_chunks, n_valid):
    (_, _, _, alog_row, dtb_row, gdn_nw_row, lb_row, hg_nw_row, w_out_b, fw_row) = params
    ltri, bd = consts
    B, L, _ = proj.shape
    tile = n_chunks * CHUNK
    has_state = s_gdn is not None
    kern = functools.partial(_mixer_kernel, n_chunks=n_chunks, n_valid=n_valid, has_state=has_state)
    col = lambda j: _batch_spec((bb, tile, SLAB), lambda b, t: (t, j))
    gdn_state_shape = (B, GDN_HALVES, GDN_HALF, GDN_DK)
    gdn_state = _batch_spec((bb,) + gdn_state_shape[1:], lambda b, t: (0, 0, 0))
    hg_state = _batch_spec((bb, HG_HEADS, HG_DK, HG_DK), lambda b, t: (0, 0, 0))
    const_args = [alog_row, dtb_row, gdn_nw_row, lb_row, hg_nw_row, ltri, bd, w_out_b, fw_row]
    y, sgn, shn = pl.pallas_call(
        kern,
        grid=(B // bb, L // tile),
        in_specs=[col(j) for j in range(8)]
                 + [_batch_spec((bb, tile, LANES), lambda b, t: (t, GATE_COL // LANES)),
                    _batch_spec((bb, tile, D_MODEL), lambda b, t: (t, 0))]
                 + ([gdn_state, hg_state] if has_state else [])
                 + [_const_spec(a.shape) for a in const_args],
        out_specs=[_batch_spec((bb, tile, D_MODEL), lambda b, t: (t, 0)), gdn_state, hg_state],
        out_shape=[
            jax.ShapeDtypeStruct((B, L, D_MODEL), F32),
            jax.ShapeDtypeStruct(gdn_state_shape, F32),
            jax.ShapeDtypeStruct((B, HG_HEADS, HG_DK, HG_DK), F32),
        ],
        scratch_shapes=[
            pltpu.VMEM((bb, GDN_HALVES, GDN_HALF, GDN_HALF), F32),
            pltpu.VMEM((bb, HG_HEADS, HG_DK, HG_DK), F32),
            pltpu.VMEM((bb, tile, GDN_QK), BF16),
            pltpu.VMEM((bb, tile, HG_QK), BF16),
        ],
        compiler_params=pltpu.CompilerParams(
            dimension_semantics=("arbitrary", "arbitrary"), vmem_limit_bytes=VMEM_LIMIT),
        name="mixers_outproj",
    )(*([proj] * 9 + [x] + ([s_gdn.reshape(gdn_state_shape), s_hg] if has_state else []) + const_args))
    return y, sgn.reshape(B, GDN_HEADS, GDN_DK, GDN_DK), shn


def _layer(x, conv_state, s_gdn, s_hg, params, consts, *, n_valid, bb, in_tile, n_chunks):
    norm_w, w_in_r, conv_w = params[:3]
    B, L, _ = x.shape
    if conv_state is None:
        cs8 = jnp.zeros((B, SUBLANES, GDN_CONV_CH), F32)
    else:
        cs8 = jnp.pad(conv_state, ((0, 0), (SUBLANES - (CONV_W - 1), 0), (0, 0)))
    proj, nc8 = _inproj(x, norm_w, w_in_r, conv_w, cs8, bb=bb, tile=in_tile,
                        n_valid=in_tile if n_valid is None else n_valid)
    new_conv = nc8[:, SUBLANES - (CONV_W - 1):, :]

    y, new_gdn, new_hg = _mixer(proj, x, s_gdn, s_hg, params, consts, bb=bb, n_chunks=n_chunks, n_valid=n_valid)
    return y, new_conv, new_gdn, new_hg


def _prep(norm_w, w_in, conv_w, gdn_A_log, gdn_dt_bias, gdn_norm_w, hgrn_lb_logits, hgrn_norm_w, w_out,
          final_norm_w):
    w = w_in[0]
    o_qkv, o_za = 0, GDN_CONV_CH
    o_b = o_za + GDN_QK
    o_a = o_b + GDN_HEADS
    o_hq = o_a + GDN_HEADS
    gate_cols = jnp.pad(w[:, o_b:o_hq], ((0, 0), (0, LANES - 2 * GDN_HEADS)))
    w_in_r = jnp.concatenate([w[:, o_qkv:o_b], w[:, o_hq:], gate_cols], axis=1).astype(BF16)

    pad_row = lambda v: jnp.pad(v.astype(F32), (GDN_HEADS, LANES - 2 * GDN_HEADS))[None, :]
    lb = jnp.cumsum(jax.nn.softmax(hgrn_lb_logits.astype(F32), axis=0), axis=0)[0]
    params = (norm_w[0][None, :], w_in_r, conv_w[0], pad_row(gdn_A_log[0]), pad_row(gdn_dt_bias[0]),
              jnp.tile(gdn_norm_w[0], GDN_HALF // GDN_DK)[None, :], lb[None, :], hgrn_norm_w[0][None, :],
              w_out[0].astype(BF16), final_norm_w[None, :])
    return params, _consts()


def kernel(x_prompt, x_sample, state_conv, state_gdn, state_hgrn, norm_w, w_in, conv_w, gdn_A_log,
           gdn_dt_bias, gdn_norm_w, hgrn_lb_logits, hgrn_norm_w, w_out, final_norm_w):
    Bs, Ls, _ = x_sample.shape
    params, consts = _prep(norm_w, w_in, conv_w, gdn_A_log, gdn_dt_bias, gdn_norm_w, hgrn_lb_logits,
                           hgrn_norm_w, w_out, final_norm_w)

    y_p, c_p, g_p, r_p = _layer(x_prompt, None, None, None, params, consts,
                                n_valid=None, bb=1, in_tile=512, n_chunks=8)

    xs = jnp.pad(x_sample, ((0, 0), (0, CHUNK - Ls), (0, 0)))
    y_s, c_s, g_s, r_s = _layer(xs, state_conv[0], state_gdn[0], state_hgrn[0], params, consts,
                                n_valid=Ls, bb=4, in_tile=CHUNK, n_chunks=1)
    y_s = y_s[:, :Ls]

    return (y_p, y_s, c_p[None], g_p[None], r_p[None], c_s[None], g_s[None], r_s[None])
```

```python
import functools

import numpy as np
import jax
import jax.numpy as jnp
from jax import lax
from jax.experimental import pallas as pl
from jax.experimental.pallas import tpu as pltpu

F32 = jnp.float32
BF16 = jnp.bfloat16

D_MODEL = 1024
CHUNK = 64
GDN_HEADS = 8
GDN_DK = 64
GDN_QK = GDN_HEADS * GDN_DK
GDN_CONV_CH = 3 * GDN_QK
CONV_W = 4
HG_HEADS = 4
HG_DK = 128
HG_QK = HG_HEADS * HG_DK
EPS = 1e-6
LOG2E = 1.4426950408889634

LANES = 128
SUBLANES = 8
MXU_DIM = 256
SLAB = 512
OUT_ROWS = 128
GATE_COL = GDN_CONV_CH + GDN_QK + 4 * HG_QK
N_PROJ = GATE_COL + LANES
GDN_HALF = MXU_DIM
GDN_HALVES = GDN_QK // GDN_HALF
VMEM_LIMIT = 48 * 1024 * 1024


def _sigmoid(x):
    return 1.0 / (1.0 + jnp.exp(-x))


def _split3(x):
    hi = x.astype(BF16)
    r = x - hi.astype(F32)
    mid = r.astype(BF16)
    lo = (r - mid.astype(F32)).astype(BF16)
    return hi, mid, lo


def _dot(a, b):
    return jnp.dot(a, b, preferred_element_type=F32)


def _dot_nt(a, b):
    return lax.dot_general(a, b, (((1,), (1,)), ((), ())), preferred_element_type=F32)


def _dot_tn(a, b):
    return lax.dot_general(a, b, (((0,), (0,)), ((), ())), preferred_element_type=F32)


def _sel_left(sel01, x):
    hi, mid, lo = _split3(x)
    return _dot(sel01, hi) + _dot(sel01, mid) + _dot(sel01, lo)


def _sel_right(x, sel01):
    hi, mid, lo = _split3(x)
    return _dot(hi, sel01) + _dot(mid, sel01) + _dot(lo, sel01)


def _batch_spec(shape, index):
    return pl.BlockSpec(shape, lambda b, t: (b,) + tuple(index(b, t)))


def _const_spec(shape):
    return pl.BlockSpec(shape, lambda b, t: (0,) * len(shape))


def _inproj_body(x_ref, nw_ref, w_ref, cw_ref, cs_ref, proj_ref, nc_ref, carry, *, n_valid):
    t = pl.program_id(1)
    bb, T = x_ref.shape[0], x_ref.shape[1]

    @pl.when(t == 0)
    def _():
        carry[...] = jnp.zeros(carry.shape, F32) if cs_ref is None else cs_ref[...]

    x = x_ref[...].reshape(bb * T, D_MODEL)
    h = x * lax.rsqrt(jnp.mean(x * x, axis=-1, keepdims=True) + EPS) * nw_ref[...]
    hb = h.astype(BF16)

    assert CONV_W == 4
    for s in range(0, GDN_CONV_CH, SLAB):
        sl = slice(s, s + SLAB)
        u_all = _dot(hb, w_ref[:, sl])
        for bi in range(bb):
            u = u_all[bi * T:(bi + 1) * T]
            full = jnp.concatenate([carry[bi, :, sl], u], axis=0)
            prev = pltpu.roll(full, 1, 0)
            pair = cw_ref[1:2, sl] * full + cw_ref[0:1, sl] * prev
            y = (cw_ref[3:4, sl] * full + cw_ref[2:3, sl] * prev + pltpu.roll(pair, 2, 0))[SUBLANES:]
            proj_ref[bi, :, sl] = y * _sigmoid(y)
            nc_ref[bi, :, sl] = full[n_valid:n_valid + SUBLANES]
            carry[bi, :, sl] = full[T:T + SUBLANES]
    for s in list(range(GDN_CONV_CH, GATE_COL, SLAB)) + [GATE_COL]:
        sl = slice(s, min(s + SLAB, N_PROJ))
        res = _dot(hb, w_ref[:, sl])
        for bi in range(bb):
            proj_ref[bi, :, sl] = res[bi * T:(bi + 1) * T]


def _consts():
    ltri = np.tril(np.ones((CHUNK, CHUNK), np.float32))
    bd = np.kron(np.eye(GDN_HALF // GDN_DK, dtype=np.float32), np.ones((GDN_DK, GDN_DK), np.float32))
    return (jnp.asarray(ltri, BF16), jnp.asarray(bd, BF16))


def _gdn_body(q_ref, k_ref, v_ref, z_ref, g_ref, s0_ref, alog_ref, dtb_ref, nw_ref, ltri_ref, bd_ref,
              o_ref, sn_ref, s_ref, *, n_chunks, n_valid):
    has_state = s0_ref is not None
    c = CHUNK
    W = GDN_HALF
    bb = q_ref.shape[0]
    t = pl.program_id(1)
    bd_mask = ((lax.broadcasted_iota(jnp.int32, (W, W), 0) & -GDN_DK)
               == (lax.broadcasted_iota(jnp.int32, (W, W), 1) & -GDN_DK))

    @pl.when(t == 0)
    def _():
        if has_state:
            for bi in range(bb):
                for a in range(GDN_HALVES):
                    wide = jnp.concatenate([s0_ref[bi, a]] * (W // GDN_DK), axis=1)
                    s_ref[bi, a] = jnp.where(bd_mask, wide, 0.0).T
        else:
            s_ref[...] = jnp.zeros(s_ref.shape, F32)

    row = lax.broadcasted_iota(jnp.int32, (c, W), 0)
    col = lax.broadcasted_iota(jnp.int32, (c, W), 1) & (GDN_DK - 1)
    eye_t = col == row
    causal_t = col <= row
    strict_t = col < row
    eye_f = eye_t.astype(F32)
    grow_ = lax.broadcasted_iota(jnp.int32, (c, LANES), 0)
    glane = lax.broadcasted_iota(jnp.int32, (c, LANES), 1)
    bd01 = bd_ref[...]
    neg_a = -jnp.exp(alog_ref[...]) * LOG2E
    ltri = ltri_ref[...]

    def block_diag(x_b):
        return jnp.concatenate([x_b] * (W // c), axis=0) * bd01

    groups = [(bi, ci) for bi in range(bb) for ci in range(n_chunks)]
    items = [(bi, ci, a) for (bi, ci) in groups for a in range(GDN_HALVES)]
    rows = lambda ci: slice(ci * c, (ci + 1) * c)
    lanes = lambda a: slice(a * W, (a + 1) * W)

    gbs = {}
    for (bi, ci) in groups:
        gz = g_ref[bi, rows(ci), :]
        xg = gz + dtb_ref[...]
        softplus = jnp.maximum(xg, 0.0) + jnp.log(1.0 + jnp.exp(-jnp.abs(xg)))
        gb = jnp.where(glane < GDN_HEADS, _sigmoid(gz),
                       jnp.where(glane < 2 * GDN_HEADS, neg_a * softplus, 0.0))
        if n_valid is not None:
            gb = jnp.where(grow_ + ci * c < n_valid, gb, 0.0)
        gbs[(bi, ci)] = gb
    gcums = {g: _sel_left(ltri, gbs[g]) for g in groups}
    lane_w = lax.broadcasted_iota(jnp.int32, (c, LANES), 1)

    def to_head_lanes(x, first):
        slabs = []
        for p in range(GDN_QK // LANES):
            even = jnp.broadcast_to(x[:, first + 2 * p:first + 2 * p + 1], (c, LANES))
            odd = jnp.broadcast_to(x[:, first + 2 * p + 1:first + 2 * p + 2], (c, LANES))
            slabs.append(jnp.where(lane_w < GDN_DK, even, odd))
        return jnp.concatenate(slabs, axis=1)

    bxs = {g: to_head_lanes(gbs[g], 0) for g in groups}
    gws = {g: to_head_lanes(gcums[g], GDN_HEADS) for g in groups}

    st = {}
    for (bi, ci, a) in items:
        qp = q_ref[bi, rows(ci), lanes(a)]
        kp = k_ref[bi, rows(ci), lanes(a)]
        ss = _dot(jnp.concatenate([qp * qp, kp * kp], axis=0).astype(BF16), bd01)
        st[(bi, ci, a)] = dict(qp=qp, kp=kp, ss=ss)
    for (bi, ci, a) in items:
        d = st[(bi, ci, a)]
        qp, kp, ss = d["qp"], d["kp"], d["ss"]
        vp = v_ref[bi, rows(ci), lanes(a)]
        qn = qp * lax.rsqrt(ss[:c] + EPS) * (GDN_DK ** -0.5)
        kn = kp * lax.rsqrt(ss[c:] + EPS)
        bp = bxs[(bi, ci)][:, lanes(a)]
        gp = gws[(bi, ci)][:, lanes(a)]
        glast = gp[c - 1:c, :]
        e_g = jnp.exp2(gp)
        kb = kn * bp
        gdiag = jnp.sum(jnp.where(eye_t, gp, 0.0), axis=0, keepdims=True)
        dec = jnp.where(causal_t, jnp.exp2(jnp.minimum(gp - gdiag, 0.0)), 0.0)
        kq = _dot_nt(jnp.concatenate([kb, qn], axis=0).astype(BF16), block_diag(kn.astype(BF16)))
        a_neg = jnp.where(strict_t, -(kq[:c] * dec), 0.0)
        st[(bi, ci, a)] = dict(
            a_neg=a_neg, attn=(kq[c:] * dec).astype(BF16), vb=(vp * bp).astype(BF16),
            kbg=(kb * e_g).astype(BF16), qg=qn * e_g, kdec=(kn * jnp.exp2(glast - gp)).astype(BF16),
            eglast=jnp.exp2(glast))

    for it in items:
        a_b = st[it]["a_neg"].astype(BF16)
        st[it]["pw"] = _dot(a_b, block_diag(a_b))
        st[it]["xk"] = eye_f + st[it]["a_neg"]
    span = 4
    while span < c:
        for it in items:
            pw, xk = st[it]["pw"], st[it]["xk"]
            both = _dot(jnp.concatenate([pw, xk], axis=0).astype(BF16), block_diag(pw.astype(BF16)))
            st[it]["pw"] = both[:c]
            st[it]["xk"] = xk + both[c:]
        span *= 2
    for it in items:
        pw, xk = st[it]["pw"], st[it]["xk"]
        st[it]["t_b"] = (xk + _dot(xk.astype(BF16), block_diag(pw.astype(BF16)))).astype(BF16)
    for it in items:
        d = st[it]
        d["u_b"] = _dot(d["t_b"], block_diag(d["vb"])).astype(BF16)
        d["w_b"] = _dot(d["t_b"], block_diag(d["kbg"])).astype(BF16)

    for it in items:
        d = st[it]
        d["q_eff"] = (d["qg"] - _dot(d["attn"], block_diag(d["w_b"]))).astype(BF16)
        d["o_loc"] = _dot(d["attn"], block_diag(d["u_b"]))
        d["n_t"] = jnp.where(bd_mask, _dot_tn(d["u_b"], d["kdec"]), 0.0)
        d["m_t"] = jnp.where(bd_mask, -_dot_tn(d["w_b"], d["kdec"]), 0.0).astype(BF16)

    for ci in range(n_chunks):
        for bi in range(bb):
            for a in range(GDN_HALVES):
                d = st[(bi, ci, a)]
                s_old = s_ref[bi, a]
                s_b = s_old.astype(BF16)
                d["o"] = _dot_nt(d["q_eff"], s_b) + d["o_loc"]
                s_ref[bi, a] = s_old * d["eglast"] + d["n_t"] + _dot(s_b, d["m_t"])

    for it in items:
        st[it]["ms"] = _dot((st[it]["o"] * st[it]["o"]).astype(BF16), bd01) * (1.0 / GDN_DK)
    for (bi, ci, a) in items:
        d = st[(bi, ci, a)]
        zp = z_ref[bi, rows(ci), lanes(a)]
        o_ref[bi, rows(ci), lanes(a)] = (d["o"] * lax.rsqrt(d["ms"] + EPS) * nw_ref[...]
                                         * (zp * _sigmoid(zp))).astype(o_ref.dtype)

    @pl.when(t == pl.num_programs(1) - 1)
    def _():
        for bi in range(bb):
            for a in range(GDN_HALVES):
                s_bd = s_ref[bi, a].T
                f = s_bd[:, :LANES] + s_bd[:, LANES:]
                sn_ref[bi, a] = f[:, :GDN_DK] + f[:, GDN_DK:]


def _hgrn_levels():
    out = []
    m = 1
    while m < CHUNK:
        out.append(m)
        m *= 2
    return out


def _level_ref_rows(b, m):
    c = b.shape[0]
    bcast = lambda r, n: jnp.broadcast_to(b[r:r + 1, :], (n, LANES))
    if 2 * m >= SUBLANES:
        parts = [bcast(blk * 2 * m + m, 2 * m) for blk in range(c // (2 * m))]
    else:
        sub = lax.broadcasted_iota(jnp.int32, (SUBLANES, LANES), 0)
        parts = []
        for g in range(c // SUBLANES):
            acc = bcast(g * SUBLANES + m, SUBLANES)
            for blk in range(1, SUBLANES // (2 * m)):
                acc = jnp.where(sub >= blk * 2 * m, bcast(g * SUBLANES + blk * 2 * m + m, SUBLANES), acc)
            parts.append(acc)
    return parts[0] if len(parts) == 1 else jnp.concatenate(parts, axis=0)


def _hgrn_body(q_ref, f_ref, i_ref, z_ref, s0_ref, lb_ref, nw_ref, ltri_ref, o_ref, sn_ref, s_ref,
               *, n_chunks, n_valid):
    has_state = s0_ref is not None
    c = CHUNK
    bb = q_ref.shape[0]
    t = pl.program_id(1)

    @pl.when(t == 0)
    def _():
        if has_state:
            for bi in range(bb):
                for h in range(HG_HEADS):
                    s_ref[bi, h] = s0_ref[bi, h].T
        else:
            s_ref[...] = jnp.zeros(s_ref.shape, F32)

    row = lax.broadcasted_iota(jnp.int32, (c, LANES), 0)
    ri = lax.broadcasted_iota(jnp.int32, (c, c), 0)
    ci_ = lax.broadcasted_iota(jnp.int32, (c, c), 1)
    levels = _hgrn_levels()
    ltri = ltri_ref[...]
    lb = lb_ref[...]
    eye_f = (ri == ci_).astype(F32)
    upper = [(row & m) != 0 for m in levels]
    sign_f = [jnp.where(u, 1.0, -1.0) for u in upper]
    keep = [((ri & -(2 * m)) == (ci_ & -(2 * m))) & ((ri & m) != 0) & ((ci_ & m) == 0) for m in levels]

    groups = [(bi, ci) for bi in range(bb) for ci in range(n_chunks)]
    items = [(bi, ci, h) for (bi, ci) in groups for h in range(HG_HEADS)]
    rows = lambda ci: slice(ci * c, (ci + 1) * c)
    lanes = lambda h: slice(h * LANES, (h + 1) * LANES)

    kks, bs = {}, {}
    for (bi, ci) in groups:
        f = lb + (1.0 - lb) * _sigmoid(f_ref[bi, rows(ci), :])
        logf = jnp.log2(f)
        kk = 1.0 - f
        if n_valid is not None:
            valid = lax.broadcasted_iota(jnp.int32, (c, HG_QK), 0) + ci * c < n_valid
            logf = jnp.where(valid, logf, 0.0)
            kk = jnp.where(valid, kk, 0.0)
        kks[(bi, ci)] = kk
        bs[(bi, ci)] = _sel_left(ltri, logf)

    st = {}
    for (bi, ci, h) in items:
        hq = q_ref[bi, rows(ci), lanes(h)]
        q = hq * _sigmoid(hq)
        k = kks[(bi, ci)][:, lanes(h)]
        v = i_ref[bi, rows(ci), lanes(h)]
        if n_valid is not None:
            v = jnp.where(row + ci * c < n_valid, v, 0.0)
        b = bs[(bi, ci)][:, lanes(h)]
        attn = eye_f * _dot_nt(q.astype(BF16), k.astype(BF16))
        for li, m in enumerate(levels):
            e = jnp.exp2((b - _level_ref_rows(b, m)) * sign_f[li])
            z_b = (jnp.where(upper[li], q, k) * e).astype(BF16)
            attn = jnp.where(keep[li], _dot_nt(z_b, z_b), attn)
        blast = b[c - 1:c, :]
        st[(bi, ci, h)] = dict(
            qe=(q * jnp.exp2(b)).astype(BF16), attn=attn.astype(BF16), v_b=v.astype(BF16),
            kdec=(k * jnp.exp2(blast - b)).astype(BF16), eblast=jnp.exp2(blast))
    for d in st.values():
        d["av"] = _dot(d["attn"], d["v_b"])
        d["upd"] = _dot_tn(d["v_b"], d["kdec"])

    for ci in range(n_chunks):
        for bi in range(bb):
            for h in range(HG_HEADS):
                d = st[(bi, ci, h)]
                s_old = s_ref[bi, h]
                o = _dot_nt(d["qe"], s_old.astype(BF16)) + d["av"]
                s_ref[bi, h] = s_old * d["eblast"] + d["upd"]

                ms = jnp.mean(o * o, axis=-1, keepdims=True)
                zp = z_ref[bi, rows(ci), lanes(h)]
                o_ref[bi, rows(ci), lanes(h)] = (o * lax.rsqrt(ms + EPS) * nw_ref[...]
                                                 * (zp * _sigmoid(zp))).astype(o_ref.dtype)

    @pl.when(t == pl.num_programs(1) - 1)
    def _():
        for bi in range(bb):
            for h in range(HG_HEADS):
                sn_ref[bi, h] = s_ref[bi, h].T


def _outproj_body(x_ref, oa_ref, ob_ref, w_ref, fw_ref, y_ref):
    bb, T = x_ref.shape[0], x_ref.shape[1]
    rows = min(T, OUT_ROWS)
    for bi in range(bb):
        for r in range(0, T, rows):
            o = jnp.concatenate([oa_ref[bi, r:r + rows, :], ob_ref[bi, r:r + rows, :]], axis=1)
            y = x_ref[bi, r:r + rows, :] + _dot(o, w_ref[...])
            y_ref[bi, r:r + rows, :] = (y * lax.rsqrt(jnp.mean(y * y, axis=-1, keepdims=True) + EPS)
                                        * fw_ref[...])


def _layer_kernel(*refs, n_chunks, n_valid, has_state):
    refs = list(refs)
    x_ref = refs.pop(0)
    cs_ref = sg0_ref = sh0_ref = None
    if has_state:
        cs_ref, sg0_ref, sh0_ref = refs[:3]
        del refs[:3]
    (nw_ref, w_in_ref, cw_ref, alog_ref, dtb_ref, gnw_ref, lb_ref, hnw_ref, w_out_ref, fw_ref, ltri_ref, bd_ref,
     y_ref, nc_ref, sgn_ref, shn_ref,
     proj, carry, sg_ref, sh_ref, oa_ref, ob_ref) = refs
    T = x_ref.shape[1]
    _inproj_body(x_ref, nw_ref, w_in_ref, cw_ref, cs_ref, proj, nc_ref, carry,
                 n_valid=T if n_valid is None else n_valid)
    col = lambda j: proj.at[:, :, j * SLAB:(j + 1) * SLAB]
    gates = proj.at[:, :, GATE_COL:N_PROJ]
    _gdn_body(col(0), col(1), col(2), col(3), gates, sg0_ref, alog_ref, dtb_ref, gnw_ref, ltri_ref, bd_ref,
              oa_ref, sgn_ref, sg_ref, n_chunks=n_chunks, n_valid=n_valid)
    _hgrn_body(col(4), col(5), col(6), col(7), sh0_ref, lb_ref, hnw_ref, ltri_ref,
               ob_ref, shn_ref, sh_ref, n_chunks=n_chunks, n_valid=n_valid)
    _outproj_body(x_ref, oa_ref, ob_ref, w_out_ref, fw_ref, y_ref)


def _layer(x, conv_state, s_gdn, s_hg, params, consts, *, n_valid, bb, n_chunks):
    B, L, _ = x.shape
    tile = n_chunks * CHUNK
    has_state = conv_state is not None
    kern = functools.partial(_layer_kernel, n_chunks=n_chunks, n_valid=n_valid, has_state=has_state)
    conv_state_spec = _batch_spec((bb, SUBLANES, GDN_CONV_CH), lambda b, t: (0, 0))
    gdn_state_shape = (B, GDN_HALVES, GDN_HALF, GDN_DK)
    gdn_state = _batch_spec((bb,) + gdn_state_shape[1:], lambda b, t: (0, 0, 0))
    hg_state = _batch_spec((bb, HG_HEADS, HG_DK, HG_DK), lambda b, t: (0, 0, 0))
    state_args, state_specs = [], []
    if has_state:
        cs8 = jnp.pad(conv_state, ((0, 0), (SUBLANES - (CONV_W - 1), 0), (0, 0)))
        state_args = [cs8, s_gdn.reshape(gdn_state_shape), s_hg]
        state_specs = [conv_state_spec, gdn_state, hg_state]
    const_args = list(params) + list(consts)
    const_specs = [pl.BlockSpec(a.shape, lambda b, t, n=a.ndim: (0,) * n, pipeline_mode=pl.Buffered(1))
                   for a in const_args]
    y, nc8, sgn, shn = pl.pallas_call(
        kern,
        grid=(B // bb, L // tile),
        in_specs=[_batch_spec((bb, tile, D_MODEL), lambda b, t: (t, 0))] + state_specs + const_specs,
        out_specs=[_batch_spec((bb, tile, D_MODEL), lambda b, t: (t, 0)), conv_state_spec, gdn_state, hg_state],
        out_shape=[
            jax.ShapeDtypeStruct((B, L, D_MODEL), F32),
            jax.ShapeDtypeStruct((B, SUBLANES, GDN_CONV_CH), F32),
            jax.ShapeDtypeStruct(gdn_state_shape, F32),
            jax.ShapeDtypeStruct((B, HG_HEADS, HG_DK, HG_DK), F32),
        ],
        scratch_shapes=[
            pltpu.VMEM((bb, tile, N_PROJ), F32),
            pltpu.VMEM((bb, SUBLANES, GDN_CONV_CH), F32),
            pltpu.VMEM((bb, GDN_HALVES, GDN_HALF, GDN_HALF), F32),
            pltpu.VMEM((bb, HG_HEADS, HG_DK, HG_DK), F32),
            pltpu.VMEM((bb, tile, GDN_QK), BF16),
            pltpu.VMEM((bb, tile, HG_QK), BF16),
        ],
        compiler_params=pltpu.CompilerParams(
            dimension_semantics=("arbitrary", "arbitrary"), vmem_limit_bytes=VMEM_LIMIT),
        name="hybrid_layer",
    )(x, *state_args, *const_args)
    new_conv = nc8[:, SUBLANES - (CONV_W - 1):, :]
    return y, new_conv, sgn.reshape(B, GDN_HEADS, GDN_DK, GDN_DK), shn


def _prep(norm_w, w_in, conv_w, gdn_A_log, gdn_dt_bias, gdn_norm_w, hgrn_lb_logits, hgrn_norm_w, w_out,
          final_norm_w):
    w = w_in[0]
    o_qkv, o_za = 0, GDN_CONV_CH
    o_b = o_za + GDN_QK
    o_a = o_b + GDN_HEADS
    o_hq = o_a + GDN_HEADS
    gate_cols = jnp.pad(w[:, o_b:o_hq], ((0, 0), (0, LANES - 2 * GDN_HEADS)))
    w_in_r = jnp.concatenate([w[:, o_qkv:o_b], w[:, o_hq:], gate_cols], axis=1).astype(BF16)

    pad_row = lambda v: jnp.pad(v.astype(F32), (GDN_HEADS, LANES - 2 * GDN_HEADS))[None, :]
    lb = jnp.cumsum(jax.nn.softmax(hgrn_lb_logits.astype(F32), axis=0), axis=0)[0]
    params = (norm_w[0][None, :], w_in_r, conv_w[0], pad_row(gdn_A_log[0]), pad_row(gdn_dt_bias[0]),
              jnp.tile(gdn_norm_w[0], GDN_HALF // GDN_DK)[None, :], lb[None, :], hgrn_norm_w[0][None, :],
              w_out[0].astype(BF16), final_norm_w[None, :])
    return params, _consts()


def kernel(x_prompt, x_sample, state_conv, state_gdn, state_hgrn, norm_w, w_in, conv_w, gdn_A_log,
           gdn_dt_bias, gdn_norm_w, hgrn_lb_logits, hgrn_norm_w, w_out, final_norm_w):
    Bs, Ls, _ = x_sample.shape
    params, consts = _prep(norm_w, w_in, conv_w, gdn_A_log, gdn_dt_bias, gdn_norm_w, hgrn_lb_logits,
                           hgrn_norm_w, w_out, final_norm_w)

    y_p, c_p, g_p, r_p = _layer(x_prompt, None, None, None, params, consts,
                                n_valid=None, bb=1, n_chunks=8)

    xs = jnp.pad(x_sample, ((0, 0), (0, CHUNK - Ls), (0, 0)))
    y_s, c_s, g_s, r_s = _layer(xs, state_conv[0], state_gdn[0], state_hgrn[0], params, consts,
                                n_valid=Ls, bb=4, n_chunks=1)
    y_s = y_s[:, :Ls]

    return (y_p, y_s, c_p[None], g_p[None], r_p[None], c_s[None], g_s[None], r_s[None])
```

```python
import functools

import numpy as np
import jax
import jax.numpy as jnp
from jax import lax
from jax.experimental import pallas as pl
from jax.experimental.pallas import tpu as pltpu

F32 = jnp.float32
BF16 = jnp.bfloat16

D_MODEL = 1024
CHUNK = 64
GDN_HEADS = 8
GDN_DK = 64
GDN_QK = GDN_HEADS * GDN_DK
GDN_CONV_CH = 3 * GDN_QK
CONV_W = 4
HG_HEADS = 4
HG_DK = 128
HG_QK = HG_HEADS * HG_DK
EPS = 1e-6
LOG2E = 1.4426950408889634

LANES = 128
SUBLANES = 8
MXU_DIM = 256
SLAB = 512
OUT_ROWS = 256
GATE_COL = GDN_CONV_CH + GDN_QK + 4 * HG_QK
N_PROJ = GATE_COL + LANES
GDN_HALF = MXU_DIM
GDN_HALVES = GDN_QK // GDN_HALF
VMEM_LIMIT = 48 * 1024 * 1024


def _sigmoid(x):
    return 1.0 / (1.0 + jnp.exp(-x))


def _split3(x):
    hi = x.astype(BF16)
    r = x - hi.astype(F32)
    mid = r.astype(BF16)
    lo = (r - mid.astype(F32)).astype(BF16)
    return hi, mid, lo


def _dot(a, b):
    return jnp.dot(a, b, preferred_element_type=F32)


def _dot_nt(a, b):
    return lax.dot_general(a, b, (((1,), (1,)), ((), ())), preferred_element_type=F32)


def _dot_tn(a, b):
    return lax.dot_general(a, b, (((0,), (0,)), ((), ())), preferred_element_type=F32)


def _sel_left(sel01, x):
    hi, mid, lo = _split3(x)
    return _dot(sel01, hi) + _dot(sel01, mid) + _dot(sel01, lo)


def _batch_spec(shape, index):
    return pl.BlockSpec(shape, lambda b, t: (b,) + tuple(index(b, t)))


def _inproj_body(x_ref, nw_ref, w_ref, cw_ref, cs_ref, proj_ref, nc_ref, win, *, n_valid):
    t = pl.program_id(1)
    bb, T = x_ref.shape[0], x_ref.shape[1]

    @pl.when(t == 0)
    def _():
        head = jnp.zeros((bb, SUBLANES, GDN_CONV_CH), F32) if cs_ref is None else cs_ref[...]
        win[:, 0:SUBLANES, :] = head

    x = x_ref[...].reshape(bb * T, D_MODEL)
    h = x * lax.rsqrt(jnp.mean(x * x, axis=-1, keepdims=True) + EPS) * nw_ref[...]
    hb = h.astype(BF16)

    assert CONV_W == 4
    for s in range(0, GDN_CONV_CH, SLAB):
        sl = slice(s, s + SLAB)
        u_all = _dot(hb, w_ref[:, sl])
        for bi in range(bb):
            win[bi, SUBLANES:SUBLANES + T, sl] = u_all[bi * T:(bi + 1) * T]
    for s in list(range(GDN_CONV_CH, GATE_COL, SLAB)) + [GATE_COL]:
        sl = slice(s, min(s + SLAB, N_PROJ))
        res = _dot(hb, w_ref[:, sl])
        for bi in range(bb):
            proj_ref[bi, :, sl] = res[bi * T:(bi + 1) * T]

    for s in range(0, GDN_CONV_CH, SLAB):
        sl = slice(s, s + SLAB)
        for bi in range(bb):
            full = win[bi, :, sl]
            prev = pltpu.roll(full, 1, 0)
            pair = cw_ref[1:2, sl] * full + cw_ref[0:1, sl] * prev
            y = (cw_ref[3:4, sl] * full + cw_ref[2:3, sl] * prev + pltpu.roll(pair, 2, 0))[SUBLANES:]
            proj_ref[bi, :, sl] = y * _sigmoid(y)
            nc_ref[bi, :, sl] = full[n_valid:n_valid + SUBLANES]
            win[bi, 0:SUBLANES, sl] = full[T:T + SUBLANES]


def _consts():
    ltri = np.tril(np.ones((CHUNK, CHUNK), np.float32))
    bd = np.kron(np.eye(GDN_HALF // GDN_DK, dtype=np.float32), np.ones((GDN_DK, GDN_DK), np.float32))
    return (jnp.asarray(ltri, BF16), jnp.asarray(bd, BF16))


def _gdn_body(q_ref, k_ref, v_ref, z_ref, g_ref, s0_ref, alog_ref, dtb_ref, nw_ref, ltri_ref, bd_ref,
              o_ref, sn_ref, s_ref, *, n_chunks, n_valid):
    has_state = s0_ref is not None
    c = CHUNK
    W = GDN_HALF
    bb = q_ref.shape[0]
    t = pl.program_id(1)
    bd_mask = ((lax.broadcasted_iota(jnp.int32, (W, W), 0) & -GDN_DK)
               == (lax.broadcasted_iota(jnp.int32, (W, W), 1) & -GDN_DK))

    @pl.when(t == 0)
    def _():
        if has_state:
            for bi in range(bb):
                for a in range(GDN_HALVES):
                    wide = jnp.concatenate([s0_ref[bi, a]] * (W // GDN_DK), axis=1)
                    s_ref[bi, a] = jnp.where(bd_mask, wide, 0.0).T
        else:
            s_ref[...] = jnp.zeros(s_ref.shape, F32)

    row = lax.broadcasted_iota(jnp.int32, (c, W), 0)
    col = lax.broadcasted_iota(jnp.int32, (c, W), 1) & (GDN_DK - 1)
    eye_t = col == row
    causal_t = col <= row
    strict_t = col < row
    eye_f = eye_t.astype(F32)
    grow_ = lax.broadcasted_iota(jnp.int32, (c, LANES), 0)
    glane = lax.broadcasted_iota(jnp.int32, (c, LANES), 1)
    bd01 = bd_ref[...]
    neg_a = -jnp.exp(alog_ref[...]) * LOG2E
    ltri = ltri_ref[...]

    def block_diag(x_b):
        return jnp.concatenate([x_b] * (W // c), axis=0) * bd01

    groups = [(bi, ci) for bi in range(bb) for ci in range(n_chunks)]
    items = [(bi, ci, a) for (bi, ci) in groups for a in range(GDN_HALVES)]
    rows = lambda ci: slice(ci * c, (ci + 1) * c)
    lanes = lambda a: slice(a * W, (a + 1) * W)

    gbs = {}
    for (bi, ci) in groups:
        gz = g_ref[bi, rows(ci), :]
        xg = gz + dtb_ref[...]
        softplus = jnp.maximum(xg, 0.0) + jnp.log(1.0 + jnp.exp(-jnp.abs(xg)))
        gb = jnp.where(glane < GDN_HEADS, _sigmoid(gz),
                       jnp.where(glane < 2 * GDN_HEADS, neg_a * softplus, 0.0))
        if n_valid is not None:
            gb = jnp.where(grow_ + ci * c < n_valid, gb, 0.0)
        gbs[(bi, ci)] = gb
    gcums = {g: _sel_left(ltri, gbs[g]) for g in groups}
    lane_w = lax.broadcasted_iota(jnp.int32, (c, LANES), 1)

    def to_head_lanes(x, first):
        slabs = []
        for p in range(GDN_QK // LANES):
            even = jnp.broadcast_to(x[:, first + 2 * p:first + 2 * p + 1], (c, LANES))
            odd = jnp.broadcast_to(x[:, first + 2 * p + 1:first + 2 * p + 2], (c, LANES))
            slabs.append(jnp.where(lane_w < GDN_DK, even, odd))
        return jnp.concatenate(slabs, axis=1)

    bxs = {g: to_head_lanes(gbs[g], 0) for g in groups}
    gws = {g: to_head_lanes(gcums[g], GDN_HEADS) for g in groups}

    st = {}
    for (bi, ci, a) in items:
        qp = q_ref[bi, rows(ci), lanes(a)]
        kp = k_ref[bi, rows(ci), lanes(a)]
        ss = _dot(jnp.concatenate([qp * qp, kp * kp], axis=0).astype(BF16), bd01)
        st[(bi, ci, a)] = dict(qp=qp, kp=kp, ss=ss)
    for (bi, ci, a) in items:
        d = st[(bi, ci, a)]
        qp, kp, ss = d["qp"], d["kp"], d["ss"]
        vp = v_ref[bi, rows(ci), lanes(a)]
        qn = qp * lax.rsqrt(ss[:c] + EPS) * (GDN_DK ** -0.5)
        kn = kp * lax.rsqrt(ss[c:] + EPS)
        bp = bxs[(bi, ci)][:, lanes(a)]
        gp = gws[(bi, ci)][:, lanes(a)]
        glast = gp[c - 1:c, :]
        e_g = jnp.exp2(gp)
        kb = kn * bp
        gdiag = jnp.sum(jnp.where(eye_t, gp, 0.0), axis=0, keepdims=True)
        dec = jnp.where(causal_t, jnp.exp2(jnp.minimum(gp - gdiag, 0.0)), 0.0)
        kq = _dot_nt(jnp.concatenate([kb, qn], axis=0).astype(BF16), block_diag(kn.astype(BF16)))
        a_neg = jnp.where(strict_t, -(kq[:c] * dec), 0.0)
        st[(bi, ci, a)] = dict(
            a_neg=a_neg, attn=(kq[c:] * dec).astype(BF16), vb=(vp * bp).astype(BF16),
            kbg=(kb * e_g).astype(BF16), qg=qn * e_g, kdec=(kn * jnp.exp2(glast - gp)).astype(BF16),
            eglast=jnp.exp2(glast))

    for it in items:
        a_b = st[it]["a_neg"].astype(BF16)
        st[it]["pw"] = _dot(a_b, block_diag(a_b))
        st[it]["xk"] = eye_f + st[it]["a_neg"]
    span = 4
    while span < c:
        for it in items:
            pw, xk = st[it]["pw"], st[it]["xk"]
            both = _dot(jnp.concatenate([pw, xk], axis=0).astype(BF16), block_diag(pw.astype(BF16)))
            st[it]["pw"] = both[:c]
            st[it]["xk"] = xk + both[c:]
        span *= 2
    for it in items:
        pw, xk = st[it]["pw"], st[it]["xk"]
        st[it]["t_b"] = (xk + _dot(xk.astype(BF16), block_diag(pw.astype(BF16)))).astype(BF16)
    for it in items:
        d = st[it]
        d["u_b"] = _dot(d["t_b"], block_diag(d["vb"])).astype(BF16)
        d["w_b"] = _dot(d["t_b"], block_diag(d["kbg"])).astype(BF16)

    for it in items:
        d = st[it]
        d["q_eff"] = (d["qg"] - _dot(d["attn"], block_diag(d["w_b"]))).astype(BF16)
        d["o_loc"] = _dot(d["attn"], block_diag(d["u_b"]))
        d["n_t"] = jnp.where(bd_mask, _dot_tn(d["u_b"], d["kdec"]), 0.0)
        d["m_t"] = jnp.where(bd_mask, -_dot_tn(d["w_b"], d["kdec"]), 0.0).astype(BF16)

    for ci in range(n_chunks):
        for bi in range(bb):
            for a in range(GDN_HALVES):
                d = st[(bi, ci, a)]
                s_old = s_ref[bi, a]
                s_b = s_old.astype(BF16)
                d["o"] = _dot_nt(d["q_eff"], s_b) + d["o_loc"]
                s_ref[bi, a] = s_old * d["eglast"] + d["n_t"] + _dot(s_b, d["m_t"])

    for it in items:
        st[it]["ms"] = _dot((st[it]["o"] * st[it]["o"]).astype(BF16), bd01) * (1.0 / GDN_DK)
    for (bi, ci, a) in items:
        d = st[(bi, ci, a)]
        zp = z_ref[bi, rows(ci), lanes(a)]
        o_ref[bi, rows(ci), lanes(a)] = (d["o"] * lax.rsqrt(d["ms"] + EPS) * nw_ref[...]
                                         * (zp * _sigmoid(zp))).astype(o_ref.dtype)

    @pl.when(t == pl.num_programs(1) - 1)
    def _():
        for bi in range(bb):
            for a in range(GDN_HALVES):
                s_bd = s_ref[bi, a].T
                f = s_bd[:, :LANES] + s_bd[:, LANES:]
                sn_ref[bi, a] = f[:, :GDN_DK] + f[:, GDN_DK:]


def _hgrn_levels():
    out = []
    m = 1
    while m < CHUNK:
        out.append(m)
        m *= 2
    return out


def _level_ref_rows(b, m):
    c = b.shape[0]
    bcast = lambda r, n: jnp.broadcast_to(b[r:r + 1, :], (n, LANES))
    if 2 * m >= SUBLANES:
        parts = [bcast(blk * 2 * m + m, 2 * m) for blk in range(c // (2 * m))]
    else:
        sub = lax.broadcasted_iota(jnp.int32, (SUBLANES, LANES), 0)
        parts = []
        for g in range(c // SUBLANES):
            acc = bcast(g * SUBLANES + m, SUBLANES)
            for blk in range(1, SUBLANES // (2 * m)):
                acc = jnp.where(sub >= blk * 2 * m, bcast(g * SUBLANES + blk * 2 * m + m, SUBLANES), acc)
            parts.append(acc)
    return parts[0] if len(parts) == 1 else jnp.concatenate(parts, axis=0)


def _hgrn_body(q_ref, f_ref, i_ref, z_ref, s0_ref, lb_ref, nw_ref, ltri_ref, o_ref, sn_ref, s_ref,
               *, n_chunks, n_valid):
    has_state = s0_ref is not None
    c = CHUNK
    bb = q_ref.shape[0]
    t = pl.program_id(1)

    @pl.when(t == 0)
    def _():
        if has_state:
            for bi in range(bb):
                for h in range(HG_HEADS):
                    s_ref[bi, h] = s0_ref[bi, h].T
        else:
            s_ref[...] = jnp.zeros(s_ref.shape, F32)

    row = lax.broadcasted_iota(jnp.int32, (c, LANES), 0)
    ri = lax.broadcasted_iota(jnp.int32, (c, c), 0)
    ci_ = lax.broadcasted_iota(jnp.int32, (c, c), 1)
    levels = _hgrn_levels()
    ltri = ltri_ref[...]
    lb = lb_ref[...]
    eye_f = (ri == ci_).astype(F32)
    upper = [(row & m) != 0 for m in levels]
    sign_f = [jnp.where(u, 1.0, -1.0) for u in upper]
    keep = [((ri & -(2 * m)) == (ci_ & -(2 * m))) & ((ri & m) != 0) & ((ci_ & m) == 0) for m in levels]

    groups = [(bi, ci) for bi in range(bb) for ci in range(n_chunks)]
    items = [(bi, ci, h) for (bi, ci) in groups for h in range(HG_HEADS)]
    rows = lambda ci: slice(ci * c, (ci + 1) * c)
    lanes = lambda h: slice(h * LANES, (h + 1) * LANES)

    kks, bs = {}, {}
    for (bi, ci) in groups:
        f = lb + (1.0 - lb) * _sigmoid(f_ref[bi, rows(ci), :])
        logf = jnp.log2(f)
        kk = 1.0 - f
        if n_valid is not None:
            valid = lax.broadcasted_iota(jnp.int32, (c, HG_QK), 0) + ci * c < n_valid
            logf = jnp.where(valid, logf, 0.0)
            kk = jnp.where(valid, kk, 0.0)
        kks[(bi, ci)] = kk
        bs[(bi, ci)] = _sel_left(ltri, logf)

    st = {}
    for (bi, ci, h) in items:
        hq = q_ref[bi, rows(ci), lanes(h)]
        q = hq * _sigmoid(hq)
        k = kks[(bi, ci)][:, lanes(h)]
        v = i_ref[bi, rows(ci), lanes(h)]
        if n_valid is not None:
            v = jnp.where(row + ci * c < n_valid, v, 0.0)
        b = bs[(bi, ci)][:, lanes(h)]
        attn = eye_f * _dot_nt(q.astype(BF16), k.astype(BF16))
        for li, m in enumerate(levels):
            e = jnp.exp2((b - _level_ref_rows(b, m)) * sign_f[li])
            z_b = (jnp.where(upper[li], q, k) * e).astype(BF16)
            attn = jnp.where(keep[li], _dot_nt(z_b, z_b), attn)
        blast = b[c - 1:c, :]
        st[(bi, ci, h)] = dict(
            qe=(q * jnp.exp2(b)).astype(BF16), attn=attn.astype(BF16), v_b=v.astype(BF16),
            kdec=(k * jnp.exp2(blast - b)).astype(BF16), eblast=jnp.exp2(blast))
    for d in st.values():
        d["av"] = _dot(d["attn"], d["v_b"])
        d["upd"] = _dot_tn(d["v_b"], d["kdec"])

    for ci in range(n_chunks):
        for bi in range(bb):
            for h in range(HG_HEADS):
                d = st[(bi, ci, h)]
                s_old = s_ref[bi, h]
                o = _dot_nt(d["qe"], s_old.astype(BF16)) + d["av"]
                s_ref[bi, h] = s_old * d["eblast"] + d["upd"]

                ms = jnp.mean(o * o, axis=-1, keepdims=True)
                zp = z_ref[bi, rows(ci), lanes(h)]
                o_ref[bi, rows(ci), lanes(h)] = (o * lax.rsqrt(ms + EPS) * nw_ref[...]
                                                 * (zp * _sigmoid(zp))).astype(o_ref.dtype)

    @pl.when(t == pl.num_programs(1) - 1)
    def _():
        for bi in range(bb):
            for h in range(HG_HEADS):
                sn_ref[bi, h] = s_ref[bi, h].T


def _outproj_body(x_ref, oa_ref, ob_ref, w_ref, fw_ref, y_ref):
    bb, T = x_ref.shape[0], x_ref.shape[1]
    rows = min(T, OUT_ROWS)
    for bi in range(bb):
        for r in range(0, T, rows):
            o = jnp.concatenate([oa_ref[bi, r:r + rows, :], ob_ref[bi, r:r + rows, :]], axis=1)
            y = x_ref[bi, r:r + rows, :] + _dot(o, w_ref[...])
            y_ref[bi, r:r + rows, :] = (y * lax.rsqrt(jnp.mean(y * y, axis=-1, keepdims=True) + EPS)
                                        * fw_ref[...])


def _layer_kernel(*refs, n_chunks, n_valid, has_state):
    refs = list(refs)
    x_ref = refs.pop(0)
    cs_ref = sg0_ref = sh0_ref = None
    if has_state:
        cs_ref, sg0_ref, sh0_ref = refs[:3]
        del refs[:3]
    (nw_ref, w_in_ref, cw_ref, alog_ref, dtb_ref, gnw_ref, lb_ref, hnw_ref, w_out_ref, fw_ref, ltri_ref, bd_ref,
     y_ref, nc_ref, sgn_ref, shn_ref,
     proj, win, sg_ref, sh_ref, oa_ref, ob_ref) = refs
    T = x_ref.shape[1]
    _inproj_body(x_ref, nw_ref, w_in_ref, cw_ref, cs_ref, proj, nc_ref, win,
                 n_valid=T if n_valid is None else n_valid)
    col = lambda j: proj.at[:, :, j * SLAB:(j + 1) * SLAB]
    gates = proj.at[:, :, GATE_COL:N_PROJ]
    _gdn_body(col(0), col(1), col(2), col(3), gates, sg0_ref, alog_ref, dtb_ref, gnw_ref, ltri_ref, bd_ref,
              oa_ref, sgn_ref, sg_ref, n_chunks=n_chunks, n_valid=n_valid)
    _hgrn_body(col(4), col(5), col(6), col(7), sh0_ref, lb_ref, hnw_ref, ltri_ref,
               ob_ref, shn_ref, sh_ref, n_chunks=n_chunks, n_valid=n_valid)
    _outproj_body(x_ref, oa_ref, ob_ref, w_out_ref, fw_ref, y_ref)


def _layer(x, conv_state, s_gdn, s_hg, params, consts, *, n_valid, bb, n_chunks):
    B, L, _ = x.shape
    tile = n_chunks * CHUNK
    has_state = conv_state is not None
    kern = functools.partial(_layer_kernel, n_chunks=n_chunks, n_valid=n_valid, has_state=has_state)
    conv_state_spec = _batch_spec((bb, SUBLANES, GDN_CONV_CH), lambda b, t: (0, 0))
    gdn_state_shape = (B, GDN_HALVES, GDN_HALF, GDN_DK)
    gdn_state = _batch_spec((bb,) + gdn_state_shape[1:], lambda b, t: (0, 0, 0))
    hg_state = _batch_spec((bb, HG_HEADS, HG_DK, HG_DK), lambda b, t: (0, 0, 0))
    state_args, state_specs = [], []
    if has_state:
        cs8 = jnp.pad(conv_state, ((0, 0), (SUBLANES - (CONV_W - 1), 0), (0, 0)))
        state_args = [cs8, s_gdn.reshape(gdn_state_shape), s_hg]
        state_specs = [conv_state_spec, gdn_state, hg_state]
    const_args = list(params) + list(consts)
    const_specs = [pl.BlockSpec(a.shape, lambda b, t, n=a.ndim: (0,) * n, pipeline_mode=pl.Buffered(1))
                   for a in const_args]
    y, nc8, sgn, shn = pl.pallas_call(
        kern,
        grid=(B // bb, L // tile),
        in_specs=[_batch_spec((bb, tile, D_MODEL), lambda b, t: (t, 0))] + state_specs + const_specs,
        out_specs=[_batch_spec((bb, tile, D_MODEL), lambda b, t: (t, 0)), conv_state_spec, gdn_state, hg_state],
        out_shape=[
            jax.ShapeDtypeStruct((B, L, D_MODEL), F32),
            jax.ShapeDtypeStruct((B, SUBLANES, GDN_CONV_CH), F32),
            jax.ShapeDtypeStruct(gdn_state_shape, F32),
            jax.ShapeDtypeStruct((B, HG_HEADS, HG_DK, HG_DK), F32),
        ],
        scratch_shapes=[
            pltpu.VMEM((bb, tile, N_PROJ), F32),
            pltpu.VMEM((bb, SUBLANES + tile, GDN_CONV_CH), F32),
            pltpu.VMEM((bb, GDN_HALVES, GDN_HALF, GDN_HALF), F32),
            pltpu.VMEM((bb, HG_HEADS, HG_DK, HG_DK), F32),
            pltpu.VMEM((bb, tile, GDN_QK), BF16),
            pltpu.VMEM((bb, tile, HG_QK), BF16),
        ],
        compiler_params=pltpu.CompilerParams(
            dimension_semantics=("arbitrary", "arbitrary"), vmem_limit_bytes=VMEM_LIMIT),
        name="hybrid_layer",
    )(x, *state_args, *const_args)
    new_conv = nc8[:, SUBLANES - (CONV_W - 1):, :]
    return y, new_conv, sgn.reshape(B, GDN_HEADS, GDN_DK, GDN_DK), shn


def _prep(norm_w, w_in, conv_w, gdn_A_log, gdn_dt_bias, gdn_norm_w, hgrn_lb_logits, hgrn_norm_w, w_out,
          final_norm_w):
    w = w_in[0]
    o_qkv, o_za = 0, GDN_CONV_CH
    o_b = o_za + GDN_QK
    o_a = o_b + GDN_HEADS
    o_hq = o_a + GDN_HEADS
    gate_cols = jnp.pad(w[:, o_b:o_hq], ((0, 0), (0, LANES - 2 * GDN_HEADS)))
    w_in_r = jnp.concatenate([w[:, o_qkv:o_b], w[:, o_hq:], gate_cols], axis=1).astype(BF16)

    pad_row = lambda v: jnp.pad(v.astype(F32), (GDN_HEADS, LANES - 2 * GDN_HEADS))[None, :]
    lb = jnp.cumsum(jax.nn.softmax(hgrn_lb_logits.astype(F32), axis=0), axis=0)[0]
    params = (norm_w[0][None, :], w_in_r, conv_w[0], pad_row(gdn_A_log[0]), pad_row(gdn_dt_bias[0]),
              jnp.tile(gdn_norm_w[0], GDN_HALF // GDN_DK)[None, :], lb[None, :], hgrn_norm_w[0][None, :],
              w_out[0].astype(BF16), final_norm_w[None, :])
    return params, _consts()


def kernel(x_prompt, x_sample, state_conv, state_gdn, state_hgrn, norm_w, w_in, conv_w, gdn_A_log,
           gdn_dt_bias, gdn_norm_w, hgrn_lb_logits, hgrn_norm_w, w_out, final_norm_w):
    Bs, Ls, _ = x_sample.shape
    params, consts = _prep(norm_w, w_in, conv_w, gdn_A_log, gdn_dt_bias, gdn_norm_w, hgrn_lb_logits,
                           hgrn_norm_w, w_out, final_norm_w)

    y_p, c_p, g_p, r_p = _layer(x_prompt, None, None, None, params, consts,
                                n_valid=None, bb=1, n_chunks=8)

    xs = jnp.pad(x_sample, ((0, 0), (0, CHUNK - Ls), (0, 0)))
    y_s, c_s, g_s, r_s = _layer(xs, state_conv[0], state_gdn[0], state_hgrn[0], params, consts,
                                n_valid=Ls, bb=4, n_chunks=1)
    y_s = y_s[:, :Ls]

    return (y_p, y_s, c_p[None], g_p[None], r_p[None], c_s[None], g_s[None], r_s[None])
```

```python
import functools

import numpy as np
import jax
import jax.numpy as jnp
from jax import lax
from jax.experimental import pallas as pl
from jax.experimental.pallas import tpu as pltpu

F32 = jnp.float32
BF16 = jnp.bfloat16

D_MODEL = 1024
CHUNK = 64
GDN_HEADS = 8
GDN_DK = 64
GDN_QK = GDN_HEADS * GDN_DK
GDN_CONV_CH = 3 * GDN_QK
CONV_W = 4
HG_HEADS = 4
HG_DK = 128
HG_QK = HG_HEADS * HG_DK
EPS = 1e-6
LOG2E = 1.4426950408889634

LANES = 128
SUBLANES = 8
MXU_DIM = 256
SLAB = 512
OUT_ROWS = 256
GATE_COL = GDN_CONV_CH + GDN_QK + 4 * HG_QK
N_PROJ = GATE_COL + LANES
GDN_HALF = MXU_DIM
GDN_HALVES = GDN_QK // GDN_HALF
VMEM_LIMIT = 48 * 1024 * 1024


def _sigmoid(x):
    return 1.0 / (1.0 + jnp.exp(-x))


def _split3(x):
    hi = x.astype(BF16)
    r = x - hi.astype(F32)
    mid = r.astype(BF16)
    lo = (r - mid.astype(F32)).astype(BF16)
    return hi, mid, lo


def _dot(a, b):
    return jnp.dot(a, b, preferred_element_type=F32)


def _dot_nt(a, b):
    return lax.dot_general(a, b, (((1,), (1,)), ((), ())), preferred_element_type=F32)


def _dot_tn(a, b):
    return lax.dot_general(a, b, (((0,), (0,)), ((), ())), preferred_element_type=F32)


def _sel_left(sel01, x):
    hi, mid, lo = _split3(x)
    return _dot(sel01, hi) + _dot(sel01, mid) + _dot(sel01, lo)


def _batch_spec(shape, index):
    return pl.BlockSpec(shape, lambda b, t: (b,) + tuple(index(b, t)))


def _inproj_body(x_ref, nw_ref, w_ref, cw_ref, cs_ref, proj_ref, nc_ref, win, *, n_valid):
    t = pl.program_id(1)
    bb, T = x_ref.shape[0], x_ref.shape[1]

    @pl.when(t == 0)
    def _():
        head = jnp.zeros((bb, SUBLANES, GDN_CONV_CH), F32) if cs_ref is None else cs_ref[...]
        win[:, 0:SUBLANES, :] = head

    x = x_ref[...].reshape(bb * T, D_MODEL)
    h = x * lax.rsqrt(jnp.mean(x * x, axis=-1, keepdims=True) + EPS) * nw_ref[...]
    hb = h.astype(BF16)

    assert CONV_W == 4
    for s in range(0, GDN_CONV_CH, SLAB):
        sl = slice(s, s + SLAB)
        u_all = _dot(hb, w_ref[:, sl])
        for bi in range(bb):
            win[bi, SUBLANES:SUBLANES + T, sl] = u_all[bi * T:(bi + 1) * T]
    for s in list(range(GDN_CONV_CH, GATE_COL, SLAB)) + [GATE_COL]:
        sl = slice(s, min(s + SLAB, N_PROJ))
        res = _dot(hb, w_ref[:, sl])
        for bi in range(bb):
            proj_ref[bi, :, sl] = res[bi * T:(bi + 1) * T]

    for s in range(0, GDN_CONV_CH, SLAB):
        sl = slice(s, s + SLAB)
        for bi in range(bb):
            full = win[bi, :, sl]
            prev = pltpu.roll(full, 1, 0)
            pair = cw_ref[1:2, sl] * full + cw_ref[0:1, sl] * prev
            y = (cw_ref[3:4, sl] * full + cw_ref[2:3, sl] * prev + pltpu.roll(pair, 2, 0))[SUBLANES:]
            proj_ref[bi, :, sl] = y * _sigmoid(y)
            nc_ref[bi, :, sl] = full[n_valid:n_valid + SUBLANES]
            win[bi, 0:SUBLANES, sl] = full[T:T + SUBLANES]


def _consts():
    ltri = np.tril(np.ones((CHUNK, CHUNK), np.float32))
    bd = np.kron(np.eye(GDN_HALF // GDN_DK, dtype=np.float32), np.ones((GDN_DK, GDN_DK), np.float32))
    return (jnp.asarray(ltri, BF16), jnp.asarray(bd, BF16))


def _gdn_body(q_ref, k_ref, v_ref, z_ref, g_ref, s0_ref, alog_ref, dtb_ref, nw_ref, ltri_ref, bd_ref,
              o_ref, sn_ref, s_ref, *, n_chunks, n_valid):
    has_state = s0_ref is not None
    c = CHUNK
    W = GDN_HALF
    bb = q_ref.shape[0]
    t = pl.program_id(1)
    bd_mask = ((lax.broadcasted_iota(jnp.int32, (W, W), 0) & -GDN_DK)
               == (lax.broadcasted_iota(jnp.int32, (W, W), 1) & -GDN_DK))

    @pl.when(t == 0)
    def _():
        if has_state:
            for bi in range(bb):
                for a in range(GDN_HALVES):
                    wide = jnp.concatenate([s0_ref[bi, a]] * (W // GDN_DK), axis=1)
                    s_ref[bi, a] = jnp.where(bd_mask, wide, 0.0).T
        else:
            s_ref[...] = jnp.zeros(s_ref.shape, F32)

    row = lax.broadcasted_iota(jnp.int32, (c, W), 0)
    col = lax.broadcasted_iota(jnp.int32, (c, W), 1) & (GDN_DK - 1)
    eye_t = col == row
    causal_t = col <= row
    strict_t = col < row
    eye_f = eye_t.astype(F32)
    grow_ = lax.broadcasted_iota(jnp.int32, (c, LANES), 0)
    glane = lax.broadcasted_iota(jnp.int32, (c, LANES), 1)
    bd01 = bd_ref[...]
    neg_a = -jnp.exp(alog_ref[...]) * LOG2E
    ltri = ltri_ref[...]

    def block_diag(x_b):
        return jnp.concatenate([x_b] * (W // c), axis=0) * bd01

    groups = [(bi, ci) for bi in range(bb) for ci in range(n_chunks)]
    items = [(bi, ci, a) for (bi, ci) in groups for a in range(GDN_HALVES)]
    rows = lambda ci: slice(ci * c, (ci + 1) * c)
    lanes = lambda a: slice(a * W, (a + 1) * W)

    gbs = {}
    for (bi, ci) in groups:
        gz = g_ref[bi, rows(ci), :]
        xg = gz + dtb_ref[...]
        softplus = jnp.maximum(xg, 0.0) + jnp.log(1.0 + jnp.exp(-jnp.abs(xg)))
        gb = jnp.where(glane < GDN_HEADS, _sigmoid(gz),
                       jnp.where(glane < 2 * GDN_HEADS, neg_a * softplus, 0.0))
        if n_valid is not None:
            gb = jnp.where(grow_ + ci * c < n_valid, gb, 0.0)
        gbs[(bi, ci)] = gb
    gcums = {g: _sel_left(ltri, gbs[g]) for g in groups}
    lane_w = lax.broadcasted_iota(jnp.int32, (c, LANES), 1)

    def to_head_lanes(x, first):
        slabs = []
        for p in range(GDN_QK // LANES):
            src = first + 2 * p + (lane_w >= GDN_DK).astype(jnp.int32)
            slabs.append(jnp.take_along_axis(x, src, axis=1))
        return jnp.concatenate(slabs, axis=1)

    bxs = {g: to_head_lanes(gbs[g], 0) for g in groups}
    gws = {g: to_head_lanes(gcums[g], GDN_HEADS) for g in groups}

    st = {}
    for (bi, ci, a) in items:
        qp = q_ref[bi, rows(ci), lanes(a)]
        kp = k_ref[bi, rows(ci), lanes(a)]
        ss = _dot(jnp.concatenate([qp * qp, kp * kp], axis=0).astype(BF16), bd01)
        st[(bi, ci, a)] = dict(qp=qp, kp=kp, ss=ss)
    for (bi, ci, a) in items:
        d = st[(bi, ci, a)]
        qp, kp, ss = d["qp"], d["kp"], d["ss"]
        vp = v_ref[bi, rows(ci), lanes(a)]
        qn = qp * lax.rsqrt(ss[:c] + EPS) * (GDN_DK ** -0.5)
        kn = kp * lax.rsqrt(ss[c:] + EPS)
        bp = bxs[(bi, ci)][:, lanes(a)]
        gp = gws[(bi, ci)][:, lanes(a)]
        glast = gp[c - 1:c, :]
        e_g = jnp.exp2(gp)
        kb = kn * bp
        gdiag = jnp.sum(jnp.where(eye_t, gp, 0.0), axis=0, keepdims=True)
        dec = jnp.where(causal_t, jnp.exp2(jnp.minimum(gp - gdiag, 0.0)), 0.0)
        kq = _dot_nt(jnp.concatenate([kb, qn], axis=0).astype(BF16), block_diag(kn.astype(BF16)))
        a_neg = jnp.where(strict_t, -(kq[:c] * dec), 0.0)
        st[(bi, ci, a)] = dict(
            a_neg=a_neg, attn=(kq[c:] * dec).astype(BF16), vb=(vp * bp).astype(BF16),
            kbg=(kb * e_g).astype(BF16), qg=qn * e_g, kdec=(kn * jnp.exp2(glast - gp)).astype(BF16),
            eglast=jnp.exp2(glast))

    for it in items:
        a_b = st[it]["a_neg"].astype(BF16)
        st[it]["pw"] = _dot(a_b, block_diag(a_b))
        st[it]["xk"] = eye_f + st[it]["a_neg"]
    span = 4
    while span < c:
        for it in items:
            pw, xk = st[it]["pw"], st[it]["xk"]
            both = _dot(jnp.concatenate([pw, xk], axis=0).astype(BF16), block_diag(pw.astype(BF16)))
            st[it]["pw"] = both[:c]
            st[it]["xk"] = xk + both[c:]
        span *= 2
    for it in items:
        pw, xk = st[it]["pw"], st[it]["xk"]
        st[it]["t_b"] = (xk + _dot(xk.astype(BF16), block_diag(pw.astype(BF16)))).astype(BF16)
    for it in items:
        d = st[it]
        d["u_b"] = _dot(d["t_b"], block_diag(d["vb"])).astype(BF16)
        d["w_b"] = _dot(d["t_b"], block_diag(d["kbg"])).astype(BF16)

    for it in items:
        d = st[it]
        d["q_eff"] = (d["qg"] - _dot(d["attn"], block_diag(d["w_b"]))).astype(BF16)
        d["o_loc"] = _dot(d["attn"], block_diag(d["u_b"]))
        d["n_t"] = jnp.where(bd_mask, _dot_tn(d["u_b"], d["kdec"]), 0.0)
        d["m_t"] = jnp.where(bd_mask, -_dot_tn(d["w_b"], d["kdec"]), 0.0).astype(BF16)

    for ci in range(n_chunks):
        for bi in range(bb):
            for a in range(GDN_HALVES):
                d = st[(bi, ci, a)]
                s_old = s_ref[bi, a]
                s_b = s_old.astype(BF16)
                d["o"] = _dot_nt(d["q_eff"], s_b) + d["o_loc"]
                s_ref[bi, a] = s_old * d["eglast"] + d["n_t"] + _dot(s_b, d["m_t"])

    for it in items:
        st[it]["ms"] = _dot((st[it]["o"] * st[it]["o"]).astype(BF16), bd01) * (1.0 / GDN_DK)
    for (bi, ci, a) in items:
        d = st[(bi, ci, a)]
        zp = z_ref[bi, rows(ci), lanes(a)]
        o_ref[bi, rows(ci), lanes(a)] = (d["o"] * lax.rsqrt(d["ms"] + EPS) * nw_ref[...]
                                         * (zp * _sigmoid(zp))).astype(o_ref.dtype)

    @pl.when(t == pl.num_programs(1) - 1)
    def _():
        for bi in range(bb):
            for a in range(GDN_HALVES):
                s_bd = s_ref[bi, a].T
                f = s_bd[:, :LANES] + s_bd[:, LANES:]
                sn_ref[bi, a] = f[:, :GDN_DK] + f[:, GDN_DK:]


def _hgrn_levels():
    out = []
    m = 1
    while m < CHUNK:
        out.append(m)
        m *= 2
    return out


def _level_ref_rows(b, m):
    c = b.shape[0]
    bcast = lambda r, n: jnp.broadcast_to(b[r:r + 1, :], (n, LANES))
    if 2 * m >= SUBLANES:
        parts = [bcast(blk * 2 * m + m, 2 * m) for blk in range(c // (2 * m))]
    else:
        sub = lax.broadcasted_iota(jnp.int32, (SUBLANES, LANES), 0)
        parts = []
        for g in range(c // SUBLANES):
            acc = bcast(g * SUBLANES + m, SUBLANES)
            for blk in range(1, SUBLANES // (2 * m)):
                acc = jnp.where(sub >= blk * 2 * m, bcast(g * SUBLANES + blk * 2 * m + m, SUBLANES), acc)
            parts.append(acc)
    return parts[0] if len(parts) == 1 else jnp.concatenate(parts, axis=0)


def _hgrn_body(q_ref, f_ref, i_ref, z_ref, s0_ref, lb_ref, nw_ref, ltri_ref, o_ref, sn_ref, s_ref,
               *, n_chunks, n_valid):
    has_state = s0_ref is not None
    c = CHUNK
    bb = q_ref.shape[0]
    t = pl.program_id(1)

    @pl.when(t == 0)
    def _():
        if has_state:
            for bi in range(bb):
                for h in range(HG_HEADS):
                    s_ref[bi, h] = s0_ref[bi, h].T
        else:
            s_ref[...] = jnp.zeros(s_ref.shape, F32)

    row = lax.broadcasted_iota(jnp.int32, (c, LANES), 0)
    ri = lax.broadcasted_iota(jnp.int32, (c, c), 0)
    ci_ = lax.broadcasted_iota(jnp.int32, (c, c), 1)
    levels = _hgrn_levels()
    ltri = ltri_ref[...]
    lb = lb_ref[...]
    eye_f = (ri == ci_).astype(F32)
    upper = [(row & m) != 0 for m in levels]
    sign_f = [jnp.where(u, 1.0, -1.0) for u in upper]
    keep = [((ri & -(2 * m)) == (ci_ & -(2 * m))) & ((ri & m) != 0) & ((ci_ & m) == 0) for m in levels]

    groups = [(bi, ci) for bi in range(bb) for ci in range(n_chunks)]
    items = [(bi, ci, h) for (bi, ci) in groups for h in range(HG_HEADS)]
    rows = lambda ci: slice(ci * c, (ci + 1) * c)
    lanes = lambda h: slice(h * LANES, (h + 1) * LANES)

    kks, bs = {}, {}
    for (bi, ci) in groups:
        f = lb + (1.0 - lb) * _sigmoid(f_ref[bi, rows(ci), :])
        logf = jnp.log2(f)
        kk = 1.0 - f
        if n_valid is not None:
            valid = lax.broadcasted_iota(jnp.int32, (c, HG_QK), 0) + ci * c < n_valid
            logf = jnp.where(valid, logf, 0.0)
            kk = jnp.where(valid, kk, 0.0)
        kks[(bi, ci)] = kk
        bs[(bi, ci)] = _sel_left(ltri, logf)

    st = {}
    for (bi, ci, h) in items:
        hq = q_ref[bi, rows(ci), lanes(h)]
        q = hq * _sigmoid(hq)
        k = kks[(bi, ci)][:, lanes(h)]
        v = i_ref[bi, rows(ci), lanes(h)]
        if n_valid is not None:
            v = jnp.where(row + ci * c < n_valid, v, 0.0)
        b = bs[(bi, ci)][:, lanes(h)]
        attn = eye_f * _dot_nt(q.astype(BF16), k.astype(BF16))
        for li, m in enumerate(levels):
            e = jnp.exp2((b - _level_ref_rows(b, m)) * sign_f[li])
            z_b = (jnp.where(upper[li], q, k) * e).astype(BF16)
            attn = jnp.where(keep[li], _dot_nt(z_b, z_b), attn)
        blast = b[c - 1:c, :]
        st[(bi, ci, h)] = dict(
            qe=(q * jnp.exp2(b)).astype(BF16), attn=attn.astype(BF16), v_b=v.astype(BF16),
            kdec=(k * jnp.exp2(blast - b)).astype(BF16), eblast=jnp.exp2(blast))
    for d in st.values():
        d["av"] = _dot(d["attn"], d["v_b"])
        d["upd"] = _dot_tn(d["v_b"], d["kdec"])

    for ci in range(n_chunks):
        for bi in range(bb):
            for h in range(HG_HEADS):
                d = st[(bi, ci, h)]
                s_old = s_ref[bi, h]
                o = _dot_nt(d["qe"], s_old.astype(BF16)) + d["av"]
                s_ref[bi, h] = s_old * d["eblast"] + d["upd"]

                ms = jnp.mean(o * o, axis=-1, keepdims=True)
                zp = z_ref[bi, rows(ci), lanes(h)]
                o_ref[bi, rows(ci), lanes(h)] = (o * lax.rsqrt(ms + EPS) * nw_ref[...]
                                                 * (zp * _sigmoid(zp))).astype(o_ref.dtype)

    @pl.when(t == pl.num_programs(1) - 1)
    def _():
        for bi in range(bb):
            for h in range(HG_HEADS):
                sn_ref[bi, h] = s_ref[bi, h].T


def _outproj_body(x_ref, oa_ref, ob_ref, w_ref, fw_ref, y_ref):
    bb, T = x_ref.shape[0], x_ref.shape[1]
    rows = min(T, OUT_ROWS)
    for bi in range(bb):
        for r in range(0, T, rows):
            o = jnp.concatenate([oa_ref[bi, r:r + rows, :], ob_ref[bi, r:r + rows, :]], axis=1)
            y = x_ref[bi, r:r + rows, :] + _dot(o, w_ref[...])
            y_ref[bi, r:r + rows, :] = (y * lax.rsqrt(jnp.mean(y * y, axis=-1, keepdims=True) + EPS)
                                        * fw_ref[...])


def _layer_kernel(*refs, n_chunks, n_valid, has_state):
    refs = list(refs)
    x_ref = refs.pop(0)
    cs_ref = sg0_ref = sh0_ref = None
    if has_state:
        cs_ref, sg0_ref, sh0_ref = refs[:3]
        del refs[:3]
    (nw_ref, w_in_ref, cw_ref, alog_ref, dtb_ref, gnw_ref, lb_ref, hnw_ref, w_out_ref, fw_ref, ltri_ref, bd_ref,
     y_ref, nc_ref, sgn_ref, shn_ref,
     proj, win, sg_ref, sh_ref, oa_ref, ob_ref) = refs
    T = x_ref.shape[1]
    _inproj_body(x_ref, nw_ref, w_in_ref, cw_ref, cs_ref, proj, nc_ref, win,
                 n_valid=T if n_valid is None else n_valid)
    col = lambda j: proj.at[:, :, j * GDN_QK:(j + 1) * GDN_QK]
    gates = proj.at[:, :, GATE_COL:N_PROJ]
    _gdn_body(col(0), col(1), col(2), col(3), gates, sg0_ref, alog_ref, dtb_ref, gnw_ref, ltri_ref, bd_ref,
              oa_ref, sgn_ref, sg_ref, n_chunks=n_chunks, n_valid=n_valid)
    _hgrn_body(col(4), col(5), col(6), col(7), sh0_ref, lb_ref, hnw_ref, ltri_ref,
               ob_ref, shn_ref, sh_ref, n_chunks=n_chunks, n_valid=n_valid)
    _outproj_body(x_ref, oa_ref, ob_ref, w_out_ref, fw_ref, y_ref)


def _layer(x, conv_state, s_gdn, s_hg, params, consts, *, n_valid, bb, n_chunks):
    B, L, _ = x.shape
    tile = n_chunks * CHUNK
    has_state = conv_state is not None
    kern = functools.partial(_layer_kernel, n_chunks=n_chunks, n_valid=n_valid, has_state=has_state)
    conv_state_spec = _batch_spec((bb, SUBLANES, GDN_CONV_CH), lambda b, t: (0, 0))
    gdn_state_shape = (B, GDN_HALVES, GDN_HALF, GDN_DK)
    gdn_state = _batch_spec((bb,) + gdn_state_shape[1:], lambda b, t: (0, 0, 0))
    hg_state = _batch_spec((bb, HG_HEADS, HG_DK, HG_DK), lambda b, t: (0, 0, 0))
    state_args, state_specs = [], []
    if has_state:
        cs8 = jnp.pad(conv_state, ((0, 0), (SUBLANES - (CONV_W - 1), 0), (0, 0)))
        state_args = [cs8, s_gdn.reshape(gdn_state_shape), s_hg]
        state_specs = [conv_state_spec, gdn_state, hg_state]
    const_args = list(params) + list(consts)
    const_specs = [pl.BlockSpec(a.shape, lambda b, t, n=a.ndim: (0,) * n, pipeline_mode=pl.Buffered(1))
                   for a in const_args]
    y, nc8, sgn, shn = pl.pallas_call(
        kern,
        grid=(B // bb, L // tile),
        in_specs=[_batch_spec((bb, tile, D_MODEL), lambda b, t: (t, 0))] + state_specs + const_specs,
        out_specs=[_batch_spec((bb, tile, D_MODEL), lambda b, t: (t, 0)), conv_state_spec, gdn_state, hg_state],
        out_shape=[
            jax.ShapeDtypeStruct((B, L, D_MODEL), F32),
            jax.ShapeDtypeStruct((B, SUBLANES, GDN_CONV_CH), F32),
            jax.ShapeDtypeStruct(gdn_state_shape, F32),
            jax.ShapeDtypeStruct((B, HG_HEADS, HG_DK, HG_DK), F32),
        ],
        scratch_shapes=[
            pltpu.VMEM((bb, tile, N_PROJ), F32),
            pltpu.VMEM((bb, SUBLANES + tile, GDN_CONV_CH), F32),
            pltpu.VMEM((bb, GDN_HALVES, GDN_HALF, GDN_HALF), F32),
            pltpu.VMEM((bb, HG_HEADS, HG_DK, HG_DK), F32),
            pltpu.VMEM((bb, tile, GDN_QK), BF16),
            pltpu.VMEM((bb, tile, HG_QK), BF16),
        ],
        compiler_params=pltpu.CompilerParams(
            dimension_semantics=("arbitrary", "arbitrary"), vmem_limit_bytes=VMEM_LIMIT),
        name="hybrid_layer",
    )(x, *state_args, *const_args)
    new_conv = nc8[:, SUBLANES - (CONV_W - 1):, :]
    return y, new_conv, sgn.reshape(B, GDN_HEADS, GDN_DK, GDN_DK), shn


def _prep(norm_w, w_in, conv_w, gdn_A_log, gdn_dt_bias, gdn_norm_w, hgrn_lb_logits, hgrn_norm_w, w_out,
          final_norm_w):
    w = w_in[0]
    o_qkv, o_za = 0, GDN_CONV_CH
    o_b = o_za + GDN_QK
    o_a = o_b + GDN_HEADS
    o_hq = o_a + GDN_HEADS
    gate_cols = jnp.pad(w[:, o_b:o_hq], ((0, 0), (0, LANES - 2 * GDN_HEADS)))
    w_in_r = jnp.concatenate([w[:, o_qkv:o_b].astype(BF16), w[:, o_hq:].astype(BF16), gate_cols.astype(BF16)],
                             axis=1)

    pad_row = lambda v: jnp.pad(v.astype(F32), (GDN_HEADS, LANES - 2 * GDN_HEADS))[None, :]
    lb = jnp.cumsum(jax.nn.softmax(hgrn_lb_logits.astype(F32), axis=0), axis=0)[0]
    params = (norm_w[0][None, :], w_in_r, conv_w[0], pad_row(gdn_A_log[0]), pad_row(gdn_dt_bias[0]),
              jnp.tile(gdn_norm_w[0], GDN_HALF // GDN_DK)[None, :], lb[None, :], hgrn_norm_w[0][None, :],
              w_out[0].astype(BF16), final_norm_w[None, :])
    return params, _consts()


def kernel(x_prompt, x_sample, state_conv, state_gdn, state_hgrn, norm_w, w_in, conv_w, gdn_A_log,
           gdn_dt_bias, gdn_norm_w, hgrn_lb_logits, hgrn_norm_w, w_out, final_norm_w):
    Bs, Ls, _ = x_sample.shape
    params, consts = _prep(norm_w, w_in, conv_w, gdn_A_log, gdn_dt_bias, gdn_norm_w, hgrn_lb_logits,
                           hgrn_norm_w, w_out, final_norm_w)

    y_p, c_p, g_p, r_p = _layer(x_prompt, None, None, None, params, consts,
                                n_valid=None, bb=1, n_chunks=8)

    xs = jnp.pad(x_sample, ((0, 0), (0, CHUNK - Ls), (0, 0)))
    y_s, c_s, g_s, r_s = _layer(xs, state_conv[0], state_gdn[0], state_hgrn[0], params, consts,
                                n_valid=Ls, bb=4, n_chunks=1)
    y_s = y_s[:, :Ls]

    return (y_p, y_s, c_p[None], g_p[None], r_p[None], c_s[None], g_s[None], r_s[None])
```

```python
import functools

import numpy as np
import jax
import jax.numpy as jnp
from jax import lax
from jax.experimental import pallas as pl
from jax.experimental.pallas import tpu as pltpu

F32 = jnp.float32
BF16 = jnp.bfloat16

D_MODEL = 1024
CHUNK = 64
GDN_HEADS = 8
GDN_DK = 64
GDN_QK = GDN_HEADS * GDN_DK
GDN_CONV_CH = 3 * GDN_QK
CONV_W = 4
HG_HEADS = 4
HG_DK = 128
HG_QK = HG_HEADS * HG_DK
EPS = 1e-6
LOG2E = 1.4426950408889634

LANES = 128
SUBLANES = 8
MXU_DIM = 256
SLAB = 512
OUT_ROWS = 256
LATE_SLABS = 2
GATE_COL = GDN_CONV_CH + GDN_QK + 4 * HG_QK
N_PROJ = GATE_COL + LANES
GDN_HALF = MXU_DIM
GDN_HALVES = GDN_QK // GDN_HALF
VMEM_LIMIT = 48 * 1024 * 1024


def _sigmoid(x):
    return 1.0 / (1.0 + jnp.exp(-x))


def _split3(x):
    hi = x.astype(BF16)
    r = x - hi.astype(F32)
    mid = r.astype(BF16)
    lo = (r - mid.astype(F32)).astype(BF16)
    return hi, mid, lo


def _dot(a, b):
    return jnp.dot(a, b, preferred_element_type=F32)


def _dot_nt(a, b):
    return lax.dot_general(a, b, (((1,), (1,)), ((), ())), preferred_element_type=F32)


def _dot_tn(a, b):
    return lax.dot_general(a, b, (((0,), (0,)), ((), ())), preferred_element_type=F32)


def _sel_left(sel01, x):
    hi, mid, lo = _split3(x)
    return _dot(sel01, hi) + _dot(sel01, mid) + _dot(sel01, lo)


def _batch_spec(shape, index):
    return pl.BlockSpec(shape, lambda b, t: (b,) + tuple(index(b, t)))


def _inproj_body(x_ref, nw_ref, w_ref, cw_ref, cs_ref, proj_ref, nc_ref, win, *, n_valid,
                 early, after_early, before_late):
    t = pl.program_id(1)
    bb, T = x_ref.shape[0], x_ref.shape[1]

    @pl.when(t == 0)
    def _():
        head = jnp.zeros((bb, SUBLANES, GDN_CONV_CH), F32) if cs_ref is None else cs_ref[...]
        win[:, 0:SUBLANES, :] = head

    x = x_ref[...].reshape(bb * T, D_MODEL)
    h = x * lax.rsqrt(jnp.mean(x * x, axis=-1, keepdims=True) + EPS) * nw_ref[...]
    hb = h.astype(BF16)

    assert CONV_W == 4
    for s in range(0, GDN_CONV_CH, SLAB):
        sl = slice(s, s + SLAB)
        u_all = _dot(hb, w_ref[:, sl])
        for bi in range(bb):
            win[bi, SUBLANES:SUBLANES + T, sl] = u_all[bi * T:(bi + 1) * T]
    def project(s):
        sl = slice(s, min(s + SLAB, N_PROJ))
        res = _dot(hb, w_ref[:, sl])
        for bi in range(bb):
            proj_ref[bi, :, sl] = res[bi * T:(bi + 1) * T]

    for s in early:
        project(s)
    after_early()
    rest = [s for s in list(range(GDN_CONV_CH, GATE_COL, SLAB)) + [GATE_COL] if s not in early]
    for i, s in enumerate(rest):
        if i == len(rest) - LATE_SLABS:
            before_late()
        project(s)

    for s in range(0, GDN_CONV_CH, SLAB):
        sl = slice(s, s + SLAB)
        for bi in range(bb):
            full = win[bi, :, sl]
            prev = pltpu.roll(full, 1, 0)
            pair = cw_ref[1:2, sl] * full + cw_ref[0:1, sl] * prev
            y = (cw_ref[3:4, sl] * full + cw_ref[2:3, sl] * prev + pltpu.roll(pair, 2, 0))[SUBLANES:]
            proj_ref[bi, :, sl] = y * _sigmoid(y)
            nc_ref[bi, :, sl] = full[n_valid:n_valid + SUBLANES]
            win[bi, 0:SUBLANES, sl] = full[T:T + SUBLANES]


def _consts():
    ltri = np.tril(np.ones((CHUNK, CHUNK), np.float32))
    bd = np.kron(np.eye(GDN_HALF // GDN_DK, dtype=np.float32), np.ones((GDN_DK, GDN_DK), np.float32))
    return (jnp.asarray(ltri, BF16), jnp.asarray(bd, BF16))


def _gdn_gates(g_ref, alog_ref, dtb_ref, *, n_chunks, n_valid):
    c = CHUNK
    bb = g_ref.shape[0]
    grow_ = lax.broadcasted_iota(jnp.int32, (c, LANES), 0)
    glane = lax.broadcasted_iota(jnp.int32, (c, LANES), 1)
    neg_a = -jnp.exp(alog_ref[...]) * LOG2E
    gbs = {}
    for bi in range(bb):
        for ci in range(n_chunks):
            gz = g_ref[bi, ci * c:(ci + 1) * c, :]
            xg = gz + dtb_ref[...]
            softplus = jnp.maximum(xg, 0.0) + jnp.log(1.0 + jnp.exp(-jnp.abs(xg)))
            gb = jnp.where(glane < GDN_HEADS, _sigmoid(gz),
                           jnp.where(glane < 2 * GDN_HEADS, neg_a * softplus, 0.0))
            if n_valid is not None:
                gb = jnp.where(grow_ + ci * c < n_valid, gb, 0.0)
            gbs[(bi, ci)] = gb
    return gbs


def _gdn_gate_lanes(gbs, ltri):
    c = CHUNK
    lane_w = lax.broadcasted_iota(jnp.int32, (c, LANES), 1)

    def to_head_lanes(x, first):
        slabs = []
        for p in range(GDN_QK // LANES):
            src = first + 2 * p + (lane_w >= GDN_DK).astype(jnp.int32)
            slabs.append(jnp.take_along_axis(x, src, axis=1))
        return jnp.concatenate(slabs, axis=1)

    gcums = {g: _sel_left(ltri, gb) for g, gb in gbs.items()}
    bxs = {g: to_head_lanes(gb, 0) for g, gb in gbs.items()}
    gws = {g: to_head_lanes(gc, GDN_HEADS) for g, gc in gcums.items()}
    return bxs, gws


def _gdn_body(q_ref, k_ref, v_ref, z_ref, gate_lanes, s0_ref, nw_ref, ltri_ref, bd_ref,
              o_ref, sn_ref, s_ref, *, n_chunks):
    has_state = s0_ref is not None
    c = CHUNK
    W = GDN_HALF
    bb = q_ref.shape[0]
    t = pl.program_id(1)
    bd_mask = ((lax.broadcasted_iota(jnp.int32, (W, W), 0) & -GDN_DK)
               == (lax.broadcasted_iota(jnp.int32, (W, W), 1) & -GDN_DK))

    @pl.when(t == 0)
    def _():
        if has_state:
            for bi in range(bb):
                for a in range(GDN_HALVES):
                    wide = jnp.concatenate([s0_ref[bi, a]] * (W // GDN_DK), axis=1)
                    s_ref[bi, a] = jnp.where(bd_mask, wide, 0.0).T
        else:
            s_ref[...] = jnp.zeros(s_ref.shape, F32)

    row = lax.broadcasted_iota(jnp.int32, (c, W), 0)
    col = lax.broadcasted_iota(jnp.int32, (c, W), 1) & (GDN_DK - 1)
    eye_t = col == row
    causal_t = col <= row
    strict_t = col < row
    eye_f = eye_t.astype(F32)
    bd01 = bd_ref[...]
    ltri = ltri_ref[...]

    def block_diag(x_b):
        return jnp.concatenate([x_b] * (W // c), axis=0) * bd01

    groups = [(bi, ci) for bi in range(bb) for ci in range(n_chunks)]
    items = [(bi, ci, a) for (bi, ci) in groups for a in range(GDN_HALVES)]
    rows = lambda ci: slice(ci * c, (ci + 1) * c)
    lanes = lambda a: slice(a * W, (a + 1) * W)

    bxs, gws = gate_lanes

    st = {}
    for (bi, ci, a) in items:
        qp = q_ref[bi, rows(ci), lanes(a)]
        kp = k_ref[bi, rows(ci), lanes(a)]
        ss = _dot(jnp.concatenate([qp * qp, kp * kp], axis=0).astype(BF16), bd01)
        st[(bi, ci, a)] = dict(qp=qp, kp=kp, ss=ss)
    for (bi, ci, a) in items:
        d = st[(bi, ci, a)]
        qp, kp, ss = d["qp"], d["kp"], d["ss"]
        vp = v_ref[bi, rows(ci), lanes(a)]
        qn = qp * lax.rsqrt(ss[:c] + EPS) * (GDN_DK ** -0.5)
        kn = kp * lax.rsqrt(ss[c:] + EPS)
        bp = bxs[(bi, ci)][:, lanes(a)]
        gp = gws[(bi, ci)][:, lanes(a)]
        glast = gp[c - 1:c, :]
        e_g = jnp.exp2(gp)
        kb = kn * bp
        gdiag = jnp.sum(jnp.where(eye_t, gp, 0.0), axis=0, keepdims=True)
        dec = jnp.where(causal_t, jnp.exp2(jnp.minimum(gp - gdiag, 0.0)), 0.0)
        kq = _dot_nt(jnp.concatenate([kb, qn], axis=0).astype(BF16), block_diag(kn.astype(BF16)))
        a_neg = jnp.where(strict_t, -(kq[:c] * dec), 0.0)
        st[(bi, ci, a)] = dict(
            a_neg=a_neg, attn=(kq[c:] * dec).astype(BF16), vb=(vp * bp).astype(BF16),
            kbg=(kb * e_g).astype(BF16), qg=qn * e_g, kdec=(kn * jnp.exp2(glast - gp)).astype(BF16),
            eglast=jnp.exp2(glast))

    for it in items:
        a_b = st[it]["a_neg"].astype(BF16)
        st[it]["pw"] = _dot(a_b, block_diag(a_b))
        st[it]["xk"] = eye_f + st[it]["a_neg"]
    span = 4
    while span < c:
        for it in items:
            pw, xk = st[it]["pw"], st[it]["xk"]
            both = _dot(jnp.concatenate([pw, xk], axis=0).astype(BF16), block_diag(pw.astype(BF16)))
            st[it]["pw"] = both[:c]
            st[it]["xk"] = xk + both[c:]
        span *= 2
    for it in items:
        pw, xk = st[it]["pw"], st[it]["xk"]
        st[it]["t_b"] = (xk + _dot(xk.astype(BF16), block_diag(pw.astype(BF16)))).astype(BF16)
    for it in items:
        d = st[it]
        d["u_b"] = _dot(d["t_b"], block_diag(d["vb"])).astype(BF16)
        d["w_b"] = _dot(d["t_b"], block_diag(d["kbg"])).astype(BF16)

    for it in items:
        d = st[it]
        d["q_eff"] = (d["qg"] - _dot(d["attn"], block_diag(d["w_b"]))).astype(BF16)
        d["o_loc"] = _dot(d["attn"], block_diag(d["u_b"]))
        d["n_t"] = jnp.where(bd_mask, _dot_tn(d["u_b"], d["kdec"]), 0.0)
        d["m_t"] = jnp.where(bd_mask, -_dot_tn(d["w_b"], d["kdec"]), 0.0).astype(BF16)

    for ci in range(n_chunks):
        for bi in range(bb):
            for a in range(GDN_HALVES):
                d = st[(bi, ci, a)]
                s_old = s_ref[bi, a]
                s_b = s_old.astype(BF16)
                d["o"] = _dot_nt(d["q_eff"], s_b) + d["o_loc"]
                s_ref[bi, a] = s_old * d["eglast"] + d["n_t"] + _dot(s_b, d["m_t"])

    for it in items:
        st[it]["ms"] = _dot((st[it]["o"] * st[it]["o"]).astype(BF16), bd01) * (1.0 / GDN_DK)
    for (bi, ci, a) in items:
        d = st[(bi, ci, a)]
        zp = z_ref[bi, rows(ci), lanes(a)]
        o_ref[bi, rows(ci), lanes(a)] = (d["o"] * lax.rsqrt(d["ms"] + EPS) * nw_ref[...]
                                         * (zp * _sigmoid(zp))).astype(o_ref.dtype)

    @pl.when(t == pl.num_programs(1) - 1)
    def _():
        for bi in range(bb):
            for a in range(GDN_HALVES):
                s_bd = s_ref[bi, a].T
                f = s_bd[:, :LANES] + s_bd[:, LANES:]
                sn_ref[bi, a] = f[:, :GDN_DK] + f[:, GDN_DK:]


def _hgrn_levels():
    out = []
    m = 1
    while m < CHUNK:
        out.append(m)
        m *= 2
    return out


def _level_ref_rows(b, m):
    c = b.shape[0]
    bcast = lambda r, n: jnp.broadcast_to(b[r:r + 1, :], (n, LANES))
    if 2 * m >= SUBLANES:
        parts = [bcast(blk * 2 * m + m, 2 * m) for blk in range(c // (2 * m))]
    else:
        sub = lax.broadcasted_iota(jnp.int32, (SUBLANES, LANES), 0)
        parts = []
        for g in range(c // SUBLANES):
            acc = bcast(g * SUBLANES + m, SUBLANES)
            for blk in range(1, SUBLANES // (2 * m)):
                acc = jnp.where(sub >= blk * 2 * m, bcast(g * SUBLANES + blk * 2 * m + m, SUBLANES), acc)
            parts.append(acc)
    return parts[0] if len(parts) == 1 else jnp.concatenate(parts, axis=0)


def _hgrn_gates(f_ref, lb_ref, *, n_chunks, n_valid):
    c = CHUNK
    bb = f_ref.shape[0]
    lb = lb_ref[...]
    kks, logfs = {}, {}
    for bi in range(bb):
        for ci in range(n_chunks):
            f = lb + (1.0 - lb) * _sigmoid(f_ref[bi, ci * c:(ci + 1) * c, :])
            logf = jnp.log2(f)
            kk = 1.0 - f
            if n_valid is not None:
                valid = lax.broadcasted_iota(jnp.int32, (c, HG_QK), 0) + ci * c < n_valid
                logf = jnp.where(valid, logf, 0.0)
                kk = jnp.where(valid, kk, 0.0)
            kks[(bi, ci)] = kk
            logfs[(bi, ci)] = logf
    return kks, logfs


def _hgrn_body(q_ref, gates, i_ref, z_ref, s0_ref, nw_ref, ltri_ref, o_ref, sn_ref, s_ref,
               *, n_chunks, n_valid):
    has_state = s0_ref is not None
    c = CHUNK
    bb = q_ref.shape[0]
    t = pl.program_id(1)

    @pl.when(t == 0)
    def _():
        if has_state:
            for bi in range(bb):
                for h in range(HG_HEADS):
                    s_ref[bi, h] = s0_ref[bi, h].T
        else:
            s_ref[...] = jnp.zeros(s_ref.shape, F32)

    row = lax.broadcasted_iota(jnp.int32, (c, LANES), 0)
    ri = lax.broadcasted_iota(jnp.int32, (c, c), 0)
    ci_ = lax.broadcasted_iota(jnp.int32, (c, c), 1)
    levels = _hgrn_levels()
    ltri = ltri_ref[...]
    eye_f = (ri == ci_).astype(F32)
    upper = [(row & m) != 0 for m in levels]
    sign_f = [jnp.where(u, 1.0, -1.0) for u in upper]
    keep = [((ri & -(2 * m)) == (ci_ & -(2 * m))) & ((ri & m) != 0) & ((ci_ & m) == 0) for m in levels]

    groups = [(bi, ci) for bi in range(bb) for ci in range(n_chunks)]
    items = [(bi, ci, h) for (bi, ci) in groups for h in range(HG_HEADS)]
    rows = lambda ci: slice(ci * c, (ci + 1) * c)
    lanes = lambda h: slice(h * LANES, (h + 1) * LANES)

    kks, bs = gates

    st = {}
    for (bi, ci, h) in items:
        hq = q_ref[bi, rows(ci), lanes(h)]
        q = hq * _sigmoid(hq)
        k = kks[(bi, ci)][:, lanes(h)]
        v = i_ref[bi, rows(ci), lanes(h)]
        if n_valid is not None:
            v = jnp.where(row + ci * c < n_valid, v, 0.0)
        b = bs[(bi, ci)][:, lanes(h)]
        attn = eye_f * _dot_nt(q.astype(BF16), k.astype(BF16))
        for li, m in enumerate(levels):
            e = jnp.exp2((b - _level_ref_rows(b, m)) * sign_f[li])
            z_b = (jnp.where(upper[li], q, k) * e).astype(BF16)
            attn = jnp.where(keep[li], _dot_nt(z_b, z_b), attn)
        blast = b[c - 1:c, :]
        st[(bi, ci, h)] = dict(
            qe=(q * jnp.exp2(b)).astype(BF16), attn=attn.astype(BF16), v_b=v.astype(BF16),
            kdec=(k * jnp.exp2(blast - b)).astype(BF16), eblast=jnp.exp2(blast))
    for d in st.values():
        d["av"] = _dot(d["attn"], d["v_b"])
        d["upd"] = _dot_tn(d["v_b"], d["kdec"])

    for ci in range(n_chunks):
        for bi in range(bb):
            for h in range(HG_HEADS):
                d = st[(bi, ci, h)]
                s_old = s_ref[bi, h]
                o = _dot_nt(d["qe"], s_old.astype(BF16)) + d["av"]
                s_ref[bi, h] = s_old * d["eblast"] + d["upd"]

                ms = jnp.mean(o * o, axis=-1, keepdims=True)
                zp = z_ref[bi, rows(ci), lanes(h)]
                o_ref[bi, rows(ci), lanes(h)] = (o * lax.rsqrt(ms + EPS) * nw_ref[...]
                                                 * (zp * _sigmoid(zp))).astype(o_ref.dtype)

    @pl.when(t == pl.num_programs(1) - 1)
    def _():
        for bi in range(bb):
            for h in range(HG_HEADS):
                sn_ref[bi, h] = s_ref[bi, h].T


def _outproj_body(x_ref, oa_ref, ob_ref, w_ref, fw_ref, y_ref):
    bb, T = x_ref.shape[0], x_ref.shape[1]
    rows = min(T, OUT_ROWS)
    for bi in range(bb):
        for r in range(0, T, rows):
            o = jnp.concatenate([oa_ref[bi, r:r + rows, :], ob_ref[bi, r:r + rows, :]], axis=1)
            y = x_ref[bi, r:r + rows, :] + _dot(o, w_ref[...])
            y_ref[bi, r:r + rows, :] = (y * lax.rsqrt(jnp.mean(y * y, axis=-1, keepdims=True) + EPS)
                                        * fw_ref[...])


def _layer_kernel(*refs, n_chunks, n_valid, has_state):
    refs = list(refs)
    x_ref = refs.pop(0)
    cs_ref = sg0_ref = sh0_ref = None
    if has_state:
        cs_ref, sg0_ref, sh0_ref = refs[:3]
        del refs[:3]
    (nw_ref, w_in_ref, cw_ref, alog_ref, dtb_ref, gnw_ref, lb_ref, hnw_ref, w_out_ref, fw_ref, ltri_ref, bd_ref,
     y_ref, nc_ref, sgn_ref, shn_ref,
     proj, win, sg_ref, sh_ref, oa_ref, ob_ref) = refs
    T = x_ref.shape[1]
    col = lambda j: proj.at[:, :, j * GDN_QK:(j + 1) * GDN_QK]
    gate_vals = {}

    def gate_math():
        gate_vals["gdn"] = _gdn_gates(proj.at[:, :, GATE_COL:N_PROJ], alog_ref, dtb_ref,
                                      n_chunks=n_chunks, n_valid=n_valid)
        gate_vals["hgrn"] = _hgrn_gates(col(5), lb_ref, n_chunks=n_chunks, n_valid=n_valid)

    def gate_sums():
        ltri = ltri_ref[...]
        gate_vals["gdn"] = _gdn_gate_lanes(gate_vals["gdn"], ltri)
        kks, logfs = gate_vals["hgrn"]
        gate_vals["hgrn"] = (kks, {g: _sel_left(ltri, lf) for g, lf in logfs.items()})

    _inproj_body(x_ref, nw_ref, w_in_ref, cw_ref, cs_ref, proj, nc_ref, win,
                 n_valid=T if n_valid is None else n_valid,
                 early=(GATE_COL, 5 * HG_QK), after_early=gate_math, before_late=gate_sums)
    _gdn_body(col(0), col(1), col(2), col(3), gate_vals["gdn"], sg0_ref, gnw_ref, ltri_ref, bd_ref,
              oa_ref, sgn_ref, sg_ref, n_chunks=n_chunks)
    _hgrn_body(col(4), gate_vals["hgrn"], col(6), col(7), sh0_ref, hnw_ref, ltri_ref,
               ob_ref, shn_ref, sh_ref, n_chunks=n_chunks, n_valid=n_valid)
    _outproj_body(x_ref, oa_ref, ob_ref, w_out_ref, fw_ref, y_ref)


def _layer(x, conv_state, s_gdn, s_hg, params, consts, *, n_valid, bb, n_chunks):
    B, L, _ = x.shape
    tile = n_chunks * CHUNK
    has_state = conv_state is not None
    kern = functools.partial(_layer_kernel, n_chunks=n_chunks, n_valid=n_valid, has_state=has_state)
    conv_state_spec = _batch_spec((bb, SUBLANES, GDN_CONV_CH), lambda b, t: (0, 0))
    gdn_state_shape = (B, GDN_HALVES, GDN_HALF, GDN_DK)
    gdn_state = _batch_spec((bb,) + gdn_state_shape[1:], lambda b, t: (0, 0, 0))
    hg_state = _batch_spec((bb, HG_HEADS, HG_DK, HG_DK), lambda b, t: (0, 0, 0))
    state_args, state_specs = [], []
    if has_state:
        cs8 = jnp.pad(conv_state, ((0, 0), (SUBLANES - (CONV_W - 1), 0), (0, 0)))
        state_args = [cs8, s_gdn.reshape(gdn_state_shape), s_hg]
        state_specs = [conv_state_spec, gdn_state, hg_state]
    const_args = list(params) + list(consts)
    const_specs = [pl.BlockSpec(a.shape, lambda b, t, n=a.ndim: (0,) * n, pipeline_mode=pl.Buffered(1))
                   for a in const_args]
    y, nc8, sgn, shn = pl.pallas_call(
        kern,
        grid=(B // bb, L // tile),
        in_specs=[_batch_spec((bb, tile, D_MODEL), lambda b, t: (t, 0))] + state_specs + const_specs,
        out_specs=[_batch_spec((bb, tile, D_MODEL), lambda b, t: (t, 0)), conv_state_spec, gdn_state, hg_state],
        out_shape=[
            jax.ShapeDtypeStruct((B, L, D_MODEL), F32),
            jax.ShapeDtypeStruct((B, SUBLANES, GDN_CONV_CH), F32),
            jax.ShapeDtypeStruct(gdn_state_shape, F32),
            jax.ShapeDtypeStruct((B, HG_HEADS, HG_DK, HG_DK), F32),
        ],
        scratch_shapes=[
            pltpu.VMEM((bb, tile, N_PROJ), F32),
            pltpu.VMEM((bb, SUBLANES + tile, GDN_CONV_CH), F32),
            pltpu.VMEM((bb, GDN_HALVES, GDN_HALF, GDN_HALF), F32),
            pltpu.VMEM((bb, HG_HEADS, HG_DK, HG_DK), F32),
            pltpu.VMEM((bb, tile, GDN_QK), BF16),
            pltpu.VMEM((bb, tile, HG_QK), BF16),
        ],
        compiler_params=pltpu.CompilerParams(
            dimension_semantics=("arbitrary", "arbitrary"), vmem_limit_bytes=VMEM_LIMIT),
        name="hybrid_layer",
    )(x, *state_args, *const_args)
    new_conv = nc8[:, SUBLANES - (CONV_W - 1):, :]
    return y, new_conv, sgn.reshape(B, GDN_HEADS, GDN_DK, GDN_DK), shn


def _prep(norm_w, w_in, conv_w, gdn_A_log, gdn_dt_bias, gdn_norm_w, hgrn_lb_logits, hgrn_norm_w, w_out,
          final_norm_w):
    w = w_in[0]
    o_qkv, o_za = 0, GDN_CONV_CH
    o_b = o_za + GDN_QK
    o_a = o_b + GDN_HEADS
    o_hq = o_a + GDN_HEADS
    gate_cols = jnp.pad(w[:, o_b:o_hq], ((0, 0), (0, LANES - 2 * GDN_HEADS)))
    w_in_r = jnp.concatenate([w[:, o_qkv:o_b].astype(BF16), w[:, o_hq:].astype(BF16), gate_cols.astype(BF16)],
                             axis=1)

    pad_row = lambda v: jnp.pad(v.astype(F32), (GDN_HEADS, LANES - 2 * GDN_HEADS))[None, :]
    lb = jnp.cumsum(jax.nn.softmax(hgrn_lb_logits.astype(F32), axis=0), axis=0)[0]
    params = (norm_w[0][None, :], w_in_r, conv_w[0], pad_row(gdn_A_log[0]), pad_row(gdn_dt_bias[0]),
              jnp.tile(gdn_norm_w[0], GDN_HALF // GDN_DK)[None, :], lb[None, :], hgrn_norm_w[0][None, :],
              w_out[0].astype(BF16), final_norm_w[None, :])
    return params, _consts()


def kernel(x_prompt, x_sample, state_conv, state_gdn, state_hgrn, norm_w, w_in, conv_w, gdn_A_log,
           gdn_dt_bias, gdn_norm_w, hgrn_lb_logits, hgrn_norm_w, w_out, final_norm_w):
    Bs, Ls, _ = x_sample.shape
    params, consts = _prep(norm_w, w_in, conv_w, gdn_A_log, gdn_dt_bias, gdn_norm_w, hgrn_lb_logits,
                           hgrn_norm_w, w_out, final_norm_w)

    y_p, c_p, g_p, r_p = _layer(x_prompt, None, None, None, params, consts,
                                n_valid=None, bb=1, n_chunks=8)

    xs = jnp.pad(x_sample, ((0, 0), (0, CHUNK - Ls), (0, 0)))
    y_s, c_s, g_s, r_s = _layer(xs, state_conv[0], state_gdn[0], state_hgrn[0], params, consts,
                                n_valid=Ls, bb=4, n_chunks=1)
    y_s = y_s[:, :Ls]

    return (y_p, y_s, c_p[None], g_p[None], r_p[None], c_s[None], g_s[None], r_s[None])
```

```python
import functools

import numpy as np
import jax
import jax.numpy as jnp
from jax import lax
from jax.experimental import pallas as pl
from jax.experimental.pallas import tpu as pltpu

F32 = jnp.float32
BF16 = jnp.bfloat16

D_MODEL = 1024
CHUNK = 64
GDN_HEADS = 8
GDN_DK = 64
GDN_QK = GDN_HEADS * GDN_DK
GDN_CONV_CH = 3 * GDN_QK
CONV_W = 4
HG_HEADS = 4
HG_DK = 128
HG_QK = HG_HEADS * HG_DK
EPS = 1e-6
LOG2E = 1.4426950408889634

LANES = 128
SUBLANES = 8
MXU_DIM = 256
SLAB = 512
OUT_ROWS = 256
LATE_SLABS = 2
SILU_COLS = (GDN_CONV_CH, GDN_CONV_CH + GDN_QK, GDN_CONV_CH + GDN_QK + 3 * HG_QK)
GATE_COL = GDN_CONV_CH + GDN_QK + 4 * HG_QK
N_PROJ = GATE_COL + LANES
GDN_HALF = MXU_DIM
GDN_HALVES = GDN_QK // GDN_HALF
VMEM_LIMIT = 48 * 1024 * 1024


def _sigmoid(x):
    return 1.0 / (1.0 + jnp.exp(-x))


def _split3(x):
    hi = x.astype(BF16)
    r = x - hi.astype(F32)
    mid = r.astype(BF16)
    lo = (r - mid.astype(F32)).astype(BF16)
    return hi, mid, lo


def _dot(a, b):
    return jnp.dot(a, b, preferred_element_type=F32)


def _dot_nt(a, b):
    return lax.dot_general(a, b, (((1,), (1,)), ((), ())), preferred_element_type=F32)


def _dot_tn(a, b):
    return lax.dot_general(a, b, (((0,), (0,)), ((), ())), preferred_element_type=F32)


def _sel_left(sel01, x):
    hi, mid, lo = _split3(x)
    return _dot(sel01, hi) + _dot(sel01, mid) + _dot(sel01, lo)


def _batch_spec(shape, index):
    return pl.BlockSpec(shape, lambda b, t: (b,) + tuple(index(b, t)))


def _inproj_body(x_ref, nw_ref, w_ref, cw_ref, cs_ref, proj_ref, nc_ref, win, *, n_valid,
                 early, after_early, before_late):
    t = pl.program_id(1)
    bb, T = x_ref.shape[0], x_ref.shape[1]

    @pl.when(t == 0)
    def _():
        head = jnp.zeros((bb, SUBLANES, GDN_CONV_CH), F32) if cs_ref is None else cs_ref[...]
        win[:, 0:SUBLANES, :] = head

    x = x_ref[...].reshape(bb * T, D_MODEL)
    h = x * lax.rsqrt(jnp.mean(x * x, axis=-1, keepdims=True) + EPS) * nw_ref[...]
    hb = h.astype(BF16)

    assert CONV_W == 4
    for s in range(0, GDN_CONV_CH, SLAB):
        sl = slice(s, s + SLAB)
        u_all = _dot(hb, w_ref[:, sl])
        for bi in range(bb):
            win[bi, SUBLANES:SUBLANES + T, sl] = u_all[bi * T:(bi + 1) * T]
    def project(s):
        sl = slice(s, min(s + SLAB, N_PROJ))
        res = _dot(hb, w_ref[:, sl])
        if s in SILU_COLS:
            res = res * _sigmoid(res)
        for bi in range(bb):
            proj_ref[bi, :, sl] = res[bi * T:(bi + 1) * T]

    for s in early:
        project(s)
    after_early()
    rest = [s for s in list(range(GDN_CONV_CH, GATE_COL, SLAB)) + [GATE_COL] if s not in early]
    for i, s in enumerate(rest):
        if i == len(rest) - LATE_SLABS:
            before_late()
        project(s)

    for s in range(0, GDN_CONV_CH, SLAB):
        sl = slice(s, s + SLAB)
        for bi in range(bb):
            full = win[bi, :, sl]
            prev = pltpu.roll(full, 1, 0)
            pair = cw_ref[1:2, sl] * full + cw_ref[0:1, sl] * prev
            y = (cw_ref[3:4, sl] * full + cw_ref[2:3, sl] * prev + pltpu.roll(pair, 2, 0))[SUBLANES:]
            proj_ref[bi, :, sl] = y * _sigmoid(y)
            nc_ref[bi, :, sl] = full[n_valid:n_valid + SUBLANES]
            win[bi, 0:SUBLANES, sl] = full[T:T + SUBLANES]


def _consts():
    ltri = np.tril(np.ones((CHUNK, CHUNK), np.float32))
    bd = np.kron(np.eye(GDN_HALF // GDN_DK, dtype=np.float32), np.ones((GDN_DK, GDN_DK), np.float32))
    return (jnp.asarray(ltri, BF16), jnp.asarray(bd, BF16))


def _gdn_gates(g_ref, alog_ref, dtb_ref, *, n_chunks, n_valid):
    c = CHUNK
    bb = g_ref.shape[0]
    grow_ = lax.broadcasted_iota(jnp.int32, (c, LANES), 0)
    glane = lax.broadcasted_iota(jnp.int32, (c, LANES), 1)
    neg_a = -jnp.exp(alog_ref[...]) * LOG2E
    gbs = {}
    for bi in range(bb):
        for ci in range(n_chunks):
            gz = g_ref[bi, ci * c:(ci + 1) * c, :]
            xg = gz + dtb_ref[...]
            softplus = jnp.maximum(xg, 0.0) + jnp.log(1.0 + jnp.exp(-jnp.abs(xg)))
            gb = jnp.where(glane < GDN_HEADS, _sigmoid(gz),
                           jnp.where(glane < 2 * GDN_HEADS, neg_a * softplus, 0.0))
            if n_valid is not None:
                gb = jnp.where(grow_ + ci * c < n_valid, gb, 0.0)
            gbs[(bi, ci)] = gb
    return gbs


def _gdn_gate_lanes(gbs, ltri):
    c = CHUNK
    lane_w = lax.broadcasted_iota(jnp.int32, (c, LANES), 1)

    def to_head_lanes(x, first):
        slabs = []
        for p in range(GDN_QK // LANES):
            src = first + 2 * p + (lane_w >= GDN_DK).astype(jnp.int32)
            slabs.append(jnp.take_along_axis(x, src, axis=1))
        return jnp.concatenate(slabs, axis=1)

    gcums = {g: _sel_left(ltri, gb) for g, gb in gbs.items()}
    row = lax.broadcasted_iota(jnp.int32, (c, GDN_QK), 0)
    col = lax.broadcasted_iota(jnp.int32, (c, GDN_QK), 1) & (GDN_DK - 1)
    out = {}
    for g, gb in gbs.items():
        gw = to_head_lanes(gcums[g], GDN_HEADS)
        glast = gw[c - 1:c, :]
        gdiag = jnp.sum(jnp.where(col == row, gw, 0.0), axis=0, keepdims=True)
        out[g] = dict(
            beta=to_head_lanes(gb, 0),
            e_g=jnp.exp2(gw),
            dec=jnp.where(col <= row, jnp.exp2(jnp.minimum(gw - gdiag, 0.0)), 0.0),
            e_rest=jnp.exp2(glast - gw),
            e_last=jnp.exp2(glast))
    return out


def _gdn_body(q_ref, k_ref, v_ref, z_ref, gate_lanes, s0_ref, nw_ref, bd_ref,
              o_ref, sn_ref, s_ref, *, n_chunks):
    has_state = s0_ref is not None
    c = CHUNK
    W = GDN_HALF
    bb = q_ref.shape[0]
    t = pl.program_id(1)
    bd_mask = ((lax.broadcasted_iota(jnp.int32, (W, W), 0) & -GDN_DK)
               == (lax.broadcasted_iota(jnp.int32, (W, W), 1) & -GDN_DK))

    @pl.when(t == 0)
    def _():
        if has_state:
            for bi in range(bb):
                for a in range(GDN_HALVES):
                    wide = jnp.concatenate([s0_ref[bi, a]] * (W // GDN_DK), axis=1)
                    s_ref[bi, a] = jnp.where(bd_mask, wide, 0.0).T
        else:
            s_ref[...] = jnp.zeros(s_ref.shape, F32)

    row = lax.broadcasted_iota(jnp.int32, (c, W), 0)
    col = lax.broadcasted_iota(jnp.int32, (c, W), 1) & (GDN_DK - 1)
    strict_t = col < row
    eye_f = (col == row).astype(F32)
    bd01 = bd_ref[...]

    def block_diag(x_b):
        return jnp.concatenate([x_b] * (W // c), axis=0) * bd01

    groups = [(bi, ci) for bi in range(bb) for ci in range(n_chunks)]
    items = [(bi, ci, a) for (bi, ci) in groups for a in range(GDN_HALVES)]
    rows = lambda ci: slice(ci * c, (ci + 1) * c)
    lanes = lambda a: slice(a * W, (a + 1) * W)

    st = {}
    for (bi, ci, a) in items:
        qp = q_ref[bi, rows(ci), lanes(a)]
        kp = k_ref[bi, rows(ci), lanes(a)]
        ss = _dot(jnp.concatenate([qp * qp, kp * kp], axis=0).astype(BF16), bd01)
        st[(bi, ci, a)] = dict(qp=qp, kp=kp, ss=ss)
    for (bi, ci, a) in items:
        d = st[(bi, ci, a)]
        qp, kp, ss = d["qp"], d["kp"], d["ss"]
        vp = v_ref[bi, rows(ci), lanes(a)]
        qn = qp * lax.rsqrt(ss[:c] + EPS) * (GDN_DK ** -0.5)
        kn = kp * lax.rsqrt(ss[c:] + EPS)
        gl = gate_lanes[(bi, ci)]
        bp, e_g, dec = gl["beta"][:, lanes(a)], gl["e_g"][:, lanes(a)], gl["dec"][:, lanes(a)]
        kb = kn * bp
        kq = _dot_nt(jnp.concatenate([kb, qn], axis=0).astype(BF16), block_diag(kn.astype(BF16)))
        a_neg = jnp.where(strict_t, -(kq[:c] * dec), 0.0)
        st[(bi, ci, a)] = dict(
            a_neg=a_neg, attn=(kq[c:] * dec).astype(BF16), vb=(vp * bp).astype(BF16),
            kbg=(kb * e_g).astype(BF16), qg=qn * e_g, kdec=(kn * gl["e_rest"][:, lanes(a)]).astype(BF16),
            eglast=gl["e_last"][:, lanes(a)])

    for it in items:
        a_b = st[it]["a_neg"].astype(BF16)
        st[it]["pw"] = _dot(a_b, block_diag(a_b))
        st[it]["xk"] = eye_f + st[it]["a_neg"]
    span = 4
    while span < c:
        for it in items:
            pw, xk = st[it]["pw"], st[it]["xk"]
            both = _dot(jnp.concatenate([pw, xk], axis=0).astype(BF16), block_diag(pw.astype(BF16)))
            st[it]["pw"] = both[:c]
            st[it]["xk"] = xk + both[c:]
        span *= 2
    for it in items:
        pw, xk = st[it]["pw"], st[it]["xk"]
        st[it]["t_b"] = (xk + _dot(xk.astype(BF16), block_diag(pw.astype(BF16)))).astype(BF16)
    for it in items:
        d = st[it]
        d["u_b"] = _dot(d["t_b"], block_diag(d["vb"])).astype(BF16)
        d["w_b"] = _dot(d["t_b"], block_diag(d["kbg"])).astype(BF16)

    for it in items:
        d = st[it]
        d["q_eff"] = (d["qg"] - _dot(d["attn"], block_diag(d["w_b"]))).astype(BF16)
        d["o_loc"] = _dot(d["attn"], block_diag(d["u_b"]))
        d["n_t"] = jnp.where(bd_mask, _dot_tn(d["u_b"], d["kdec"]), 0.0)
        d["m_t"] = jnp.where(bd_mask, -_dot_tn(d["w_b"], d["kdec"]), 0.0).astype(BF16)

    for ci in range(n_chunks):
        for bi in range(bb):
            for a in range(GDN_HALVES):
                d = st[(bi, ci, a)]
                s_old = s_ref[bi, a]
                s_b = s_old.astype(BF16)
                d["o"] = _dot_nt(d["q_eff"], s_b) + d["o_loc"]
                s_ref[bi, a] = s_old * d["eglast"] + d["n_t"] + _dot(s_b, d["m_t"])

    for it in items:
        st[it]["ms"] = _dot((st[it]["o"] * st[it]["o"]).astype(BF16), bd01) * (1.0 / GDN_DK)
    for (bi, ci, a) in items:
        d = st[(bi, ci, a)]
        zs = z_ref[bi, rows(ci), lanes(a)]
        o_ref[bi, rows(ci), lanes(a)] = (d["o"] * lax.rsqrt(d["ms"] + EPS) * nw_ref[...] * zs).astype(o_ref.dtype)

    @pl.when(t == pl.num_programs(1) - 1)
    def _():
        for bi in range(bb):
            for a in range(GDN_HALVES):
                s_bd = s_ref[bi, a].T
                f = s_bd[:, :LANES] + s_bd[:, LANES:]
                sn_ref[bi, a] = f[:, :GDN_DK] + f[:, GDN_DK:]


def _hgrn_levels():
    out = []
    m = 1
    while m < CHUNK:
        out.append(m)
        m *= 2
    return out


def _level_ref_rows(b, m):
    c = b.shape[0]
    bcast = lambda r, n: jnp.broadcast_to(b[r:r + 1, :], (n, LANES))
    if 2 * m >= SUBLANES:
        parts = [bcast(blk * 2 * m + m, 2 * m) for blk in range(c // (2 * m))]
    else:
        sub = lax.broadcasted_iota(jnp.int32, (SUBLANES, LANES), 0)
        parts = []
        for g in range(c // SUBLANES):
            acc = bcast(g * SUBLANES + m, SUBLANES)
            for blk in range(1, SUBLANES // (2 * m)):
                acc = jnp.where(sub >= blk * 2 * m, bcast(g * SUBLANES + blk * 2 * m + m, SUBLANES), acc)
            parts.append(acc)
    return parts[0] if len(parts) == 1 else jnp.concatenate(parts, axis=0)


def _hgrn_gates(f_ref, lb_ref, *, n_chunks, n_valid):
    c = CHUNK
    bb = f_ref.shape[0]
    lb = lb_ref[...]
    kks, logfs = {}, {}
    for bi in range(bb):
        for ci in range(n_chunks):
            f = lb + (1.0 - lb) * _sigmoid(f_ref[bi, ci * c:(ci + 1) * c, :])
            logf = jnp.log2(f)
            kk = 1.0 - f
            if n_valid is not None:
                valid = lax.broadcasted_iota(jnp.int32, (c, HG_QK), 0) + ci * c < n_valid
                logf = jnp.where(valid, logf, 0.0)
                kk = jnp.where(valid, kk, 0.0)
            kks[(bi, ci)] = kk
            logfs[(bi, ci)] = logf
    return kks, logfs


def _hgrn_body(q_ref, gates, i_ref, z_ref, s0_ref, nw_ref, o_ref, sn_ref, s_ref,
               *, n_chunks, n_valid):
    has_state = s0_ref is not None
    c = CHUNK
    bb = q_ref.shape[0]
    t = pl.program_id(1)

    @pl.when(t == 0)
    def _():
        if has_state:
            for bi in range(bb):
                for h in range(HG_HEADS):
                    s_ref[bi, h] = s0_ref[bi, h].T
        else:
            s_ref[...] = jnp.zeros(s_ref.shape, F32)

    row = lax.broadcasted_iota(jnp.int32, (c, LANES), 0)
    ri = lax.broadcasted_iota(jnp.int32, (c, c), 0)
    ci_ = lax.broadcasted_iota(jnp.int32, (c, c), 1)
    levels = _hgrn_levels()
    eye_f = (ri == ci_).astype(F32)
    upper = [(row & m) != 0 for m in levels]
    sign_f = [jnp.where(u, 1.0, -1.0) for u in upper]
    keep = [((ri & -(2 * m)) == (ci_ & -(2 * m))) & ((ri & m) != 0) & ((ci_ & m) == 0) for m in levels]

    groups = [(bi, ci) for bi in range(bb) for ci in range(n_chunks)]
    items = [(bi, ci, h) for (bi, ci) in groups for h in range(HG_HEADS)]
    rows = lambda ci: slice(ci * c, (ci + 1) * c)
    lanes = lambda h: slice(h * LANES, (h + 1) * LANES)

    kks, bs = gates

    st = {}
    for (bi, ci, h) in items:
        q = q_ref[bi, rows(ci), lanes(h)]
        k = kks[(bi, ci)][:, lanes(h)]
        v = i_ref[bi, rows(ci), lanes(h)]
        if n_valid is not None:
            v = jnp.where(row + ci * c < n_valid, v, 0.0)
        b = bs[(bi, ci)][:, lanes(h)]
        attn = eye_f * _dot_nt(q.astype(BF16), k.astype(BF16))
        for li, m in enumerate(levels):
            e = jnp.exp2((b - _level_ref_rows(b, m)) * sign_f[li])
            z_b = (jnp.where(upper[li], q, k) * e).astype(BF16)
            attn = jnp.where(keep[li], _dot_nt(z_b, z_b), attn)
        blast = b[c - 1:c, :]
        st[(bi, ci, h)] = dict(
            qe=(q * jnp.exp2(b)).astype(BF16), attn=attn.astype(BF16), v_b=v.astype(BF16),
            kdec=(k * jnp.exp2(blast - b)).astype(BF16), eblast=jnp.exp2(blast))
    for d in st.values():
        d["av"] = _dot(d["attn"], d["v_b"])
        d["upd"] = _dot_tn(d["v_b"], d["kdec"])

    for ci in range(n_chunks):
        for bi in range(bb):
            for h in range(HG_HEADS):
                d = st[(bi, ci, h)]
                s_old = s_ref[bi, h]
                o = _dot_nt(d["qe"], s_old.astype(BF16)) + d["av"]
                s_ref[bi, h] = s_old * d["eblast"] + d["upd"]

                ms = jnp.mean(o * o, axis=-1, keepdims=True)
                zs = z_ref[bi, rows(ci), lanes(h)]
                o_ref[bi, rows(ci), lanes(h)] = (o * lax.rsqrt(ms + EPS) * nw_ref[...] * zs).astype(o_ref.dtype)

    @pl.when(t == pl.num_programs(1) - 1)
    def _():
        for bi in range(bb):
            for h in range(HG_HEADS):
                sn_ref[bi, h] = s_ref[bi, h].T


def _outproj_body(x_ref, oa_ref, ob_ref, w_ref, fw_ref, y_ref):
    bb, T = x_ref.shape[0], x_ref.shape[1]
    rows = min(T, OUT_ROWS)
    for bi in range(bb):
        for r in range(0, T, rows):
            o = jnp.concatenate([oa_ref[bi, r:r + rows, :], ob_ref[bi, r:r + rows, :]], axis=1)
            y = x_ref[bi, r:r + rows, :] + _dot(o, w_ref[...])
            y_ref[bi, r:r + rows, :] = (y * lax.rsqrt(jnp.mean(y * y, axis=-1, keepdims=True) + EPS)
                                        * fw_ref[...])


def _layer_kernel(*refs, n_chunks, n_valid, has_state):
    refs = list(refs)
    x_ref = refs.pop(0)
    cs_ref = sg0_ref = sh0_ref = None
    if has_state:
        cs_ref, sg0_ref, sh0_ref = refs[:3]
        del refs[:3]
    (nw_ref, w_in_ref, cw_ref, alog_ref, dtb_ref, gnw_ref, lb_ref, hnw_ref, w_out_ref, fw_ref, ltri_ref, bd_ref,
     y_ref, nc_ref, sgn_ref, shn_ref,
     proj, win, sg_ref, sh_ref, oa_ref, ob_ref) = refs
    T = x_ref.shape[1]
    col = lambda j: proj.at[:, :, j * GDN_QK:(j + 1) * GDN_QK]
    gate_vals = {}

    def gate_math():
        gate_vals["gdn"] = _gdn_gates(proj.at[:, :, GATE_COL:N_PROJ], alog_ref, dtb_ref,
                                      n_chunks=n_chunks, n_valid=n_valid)
        gate_vals["hgrn"] = _hgrn_gates(col(5), lb_ref, n_chunks=n_chunks, n_valid=n_valid)

    def gate_sums():
        ltri = ltri_ref[...]
        gate_vals["gdn"] = _gdn_gate_lanes(gate_vals["gdn"], ltri)
        kks, logfs = gate_vals["hgrn"]
        gate_vals["hgrn"] = (kks, {g: _sel_left(ltri, lf) for g, lf in logfs.items()})

    _inproj_body(x_ref, nw_ref, w_in_ref, cw_ref, cs_ref, proj, nc_ref, win,
                 n_valid=T if n_valid is None else n_valid,
                 early=(GATE_COL, 5 * HG_QK), after_early=gate_math, before_late=gate_sums)
    _gdn_body(col(0), col(1), col(2), col(3), gate_vals["gdn"], sg0_ref, gnw_ref, bd_ref,
              oa_ref, sgn_ref, sg_ref, n_chunks=n_chunks)
    _hgrn_body(col(4), gate_vals["hgrn"], col(6), col(7), sh0_ref, hnw_ref,
               ob_ref, shn_ref, sh_ref, n_chunks=n_chunks, n_valid=n_valid)
    _outproj_body(x_ref, oa_ref, ob_ref, w_out_ref, fw_ref, y_ref)


def _layer(x, conv_state, s_gdn, s_hg, params, consts, *, n_valid, bb, n_chunks):
    B, L, _ = x.shape
    tile = n_chunks * CHUNK
    has_state = conv_state is not None
    kern = functools.partial(_layer_kernel, n_chunks=n_chunks, n_valid=n_valid, has_state=has_state)
    conv_state_spec = _batch_spec((bb, SUBLANES, GDN_CONV_CH), lambda b, t: (0, 0))
    gdn_state_shape = (B, GDN_HALVES, GDN_HALF, GDN_DK)
    gdn_state = _batch_spec((bb,) + gdn_state_shape[1:], lambda b, t: (0, 0, 0))
    hg_state = _batch_spec((bb, HG_HEADS, HG_DK, HG_DK), lambda b, t: (0, 0, 0))
    state_args, state_specs = [], []
    if has_state:
        cs8 = jnp.pad(conv_state, ((0, 0), (SUBLANES - (CONV_W - 1), 0), (0, 0)))
        state_args = [cs8, s_gdn.reshape(gdn_state_shape), s_hg]
        state_specs = [conv_state_spec, gdn_state, hg_state]
    const_args = list(params) + list(consts)
    const_specs = [pl.BlockSpec(a.shape, lambda b, t, n=a.ndim: (0,) * n, pipeline_mode=pl.Buffered(1))
                   for a in const_args]
    y, nc8, sgn, shn = pl.pallas_call(
        kern,
        grid=(B // bb, L // tile),
        in_specs=[_batch_spec((bb, tile, D_MODEL), lambda b, t: (t, 0))] + state_specs + const_specs,
        out_specs=[_batch_spec((bb, tile, D_MODEL), lambda b, t: (t, 0)), conv_state_spec, gdn_state, hg_state],
        out_shape=[
            jax.ShapeDtypeStruct((B, L, D_MODEL), F32),
            jax.ShapeDtypeStruct((B, SUBLANES, GDN_CONV_CH), F32),
            jax.ShapeDtypeStruct(gdn_state_shape, F32),
            jax.ShapeDtypeStruct((B, HG_HEADS, HG_DK, HG_DK), F32),
        ],
        scratch_shapes=[
            pltpu.VMEM((bb, tile, N_PROJ), F32),
            pltpu.VMEM((bb, SUBLANES + tile, GDN_CONV_CH), F32),
            pltpu.VMEM((bb, GDN_HALVES, GDN_HALF, GDN_HALF), F32),
            pltpu.VMEM((bb, HG_HEADS, HG_DK, HG_DK), F32),
            pltpu.VMEM((bb, tile, GDN_QK), BF16),
            pltpu.VMEM((bb, tile, HG_QK), BF16),
        ],
        compiler_params=pltpu.CompilerParams(
            dimension_semantics=("arbitrary", "arbitrary"), vmem_limit_bytes=VMEM_LIMIT),
        name="hybrid_layer",
    )(x, *state_args, *const_args)
    new_conv = nc8[:, SUBLANES - (CONV_W - 1):, :]
    return y, new_conv, sgn.reshape(B, GDN_HEADS, GDN_DK, GDN_DK), shn


def _prep(norm_w, w_in, conv_w, gdn_A_log, gdn_dt_bias, gdn_norm_w, hgrn_lb_logits, hgrn_norm_w, w_out,
          final_norm_w):
    w = w_in[0]
    o_qkv, o_za = 0, GDN_CONV_CH
    o_b = o_za + GDN_QK
    o_a = o_b + GDN_HEADS
    o_hq = o_a + GDN_HEADS
    gate_cols = jnp.pad(w[:, o_b:o_hq], ((0, 0), (0, LANES - 2 * GDN_HEADS)))
    w_in_r = jnp.concatenate([w[:, o_qkv:o_b].astype(BF16), w[:, o_hq:].astype(BF16), gate_cols.astype(BF16)],
                             axis=1)

    pad_row = lambda v: jnp.pad(v.astype(F32), (GDN_HEADS, LANES - 2 * GDN_HEADS))[None, :]
    lb = jnp.cumsum(jax.nn.softmax(hgrn_lb_logits.astype(F32), axis=0), axis=0)[0]
    params = (norm_w[0][None, :], w_in_r, conv_w[0], pad_row(gdn_A_log[0]), pad_row(gdn_dt_bias[0]),
              jnp.tile(gdn_norm_w[0], GDN_HALF // GDN_DK)[None, :], lb[None, :], hgrn_norm_w[0][None, :],
              w_out[0].astype(BF16), final_norm_w[None, :])
    return params, _consts()


def kernel(x_prompt, x_sample, state_conv, state_gdn, state_hgrn, norm_w, w_in, conv_w, gdn_A_log,
           gdn_dt_bias, gdn_norm_w, hgrn_lb_logits, hgrn_norm_w, w_out, final_norm_w):
    Bs, Ls, _ = x_sample.shape
    params, consts = _prep(norm_w, w_in, conv_w, gdn_A_log, gdn_dt_bias, gdn_norm_w, hgrn_lb_logits,
                           hgrn_norm_w, w_out, final_norm_w)

    y_p, c_p, g_p, r_p = _layer(x_prompt, None, None, None, params, consts,
                                n_valid=None, bb=1, n_chunks=8)

    xs = jnp.pad(x_sample, ((0, 0), (0, CHUNK - Ls), (0, 0)))
    y_s, c_s, g_s, r_s = _layer(xs, state_conv[0], state_gdn[0], state_hgrn[0], params, consts,
                                n_valid=Ls, bb=4, n_chunks=1)
    y_s = y_s[:, :Ls]

    return (y_p, y_s, c_p[None], g_p[None], r_p[None], c_s[None], g_s[None], r_s[None])
```

```python
import functools

import numpy as np
import jax
import jax.numpy as jnp
from jax import lax
from jax.experimental import pallas as pl
from jax.experimental.pallas import tpu as pltpu

F32 = jnp.float32
BF16 = jnp.bfloat16

D_MODEL = 1024
CHUNK = 64
GDN_HEADS = 8
GDN_DK = 64
GDN_QK = GDN_HEADS * GDN_DK
GDN_CONV_CH = 3 * GDN_QK
CONV_W = 4
HG_HEADS = 4
HG_DK = 128
HG_QK = HG_HEADS * HG_DK
EPS = 1e-6
LOG2E = 1.4426950408889634

LANES = 128
SUBLANES = 8
MXU_DIM = 256
SLAB = 512
OUT_ROWS = 256
LATE_SLABS = 2
SILU_COLS = (GDN_CONV_CH, GDN_CONV_CH + GDN_QK, GDN_CONV_CH + GDN_QK + 3 * HG_QK)
GATE_COL = GDN_CONV_CH + GDN_QK + 4 * HG_QK
N_PROJ = GATE_COL + LANES
GDN_HALF = MXU_DIM
GDN_HALVES = GDN_QK // GDN_HALF
VMEM_LIMIT = 48 * 1024 * 1024


def _sigmoid(x):
    return 1.0 / (1.0 + jnp.exp(-x))


def _split3(x):
    hi = x.astype(BF16)
    r = x - hi.astype(F32)
    mid = r.astype(BF16)
    lo = (r - mid.astype(F32)).astype(BF16)
    return hi, mid, lo


def _dot(a, b):
    return jnp.dot(a, b, preferred_element_type=F32)


def _dot_nt(a, b):
    return lax.dot_general(a, b, (((1,), (1,)), ((), ())), preferred_element_type=F32)


def _dot_tn(a, b):
    return lax.dot_general(a, b, (((0,), (0,)), ((), ())), preferred_element_type=F32)


def _sel_left(sel01, x):
    hi, mid, lo = _split3(x)
    return _dot(sel01, hi) + _dot(sel01, mid) + _dot(sel01, lo)


def _batch_spec(shape, index):
    return pl.BlockSpec(shape, lambda b, t: (b,) + tuple(index(b, t)))


def _inproj_body(x_ref, nw_ref, w_ref, cw_ref, cs_ref, proj_ref, nc_ref, win, *, n_valid,
                 early, after_early, before_late):
    t = pl.program_id(1)
    bb, T = x_ref.shape[0], x_ref.shape[1]

    @pl.when(t == 0)
    def _():
        head = jnp.zeros((bb, SUBLANES, GDN_CONV_CH), F32) if cs_ref is None else cs_ref[...]
        win[:, 0:SUBLANES, :] = head

    x = x_ref[...].reshape(bb * T, D_MODEL)
    h = x * lax.rsqrt(jnp.mean(x * x, axis=-1, keepdims=True) + EPS) * nw_ref[...]
    hb = h.astype(BF16)

    assert CONV_W == 4
    for s in range(0, GDN_CONV_CH, SLAB):
        sl = slice(s, s + SLAB)
        u_all = _dot(hb, w_ref[:, sl])
        for bi in range(bb):
            win[bi, SUBLANES:SUBLANES + T, sl] = u_all[bi * T:(bi + 1) * T]
    def project(s):
        sl = slice(s, min(s + SLAB, N_PROJ))
        res = _dot(hb, w_ref[:, sl])
        if s in SILU_COLS:
            res = res * _sigmoid(res)
        for bi in range(bb):
            proj_ref[bi, :, sl] = res[bi * T:(bi + 1) * T]

    for s in early:
        project(s)
    after_early()
    rest = [s for s in list(range(GDN_CONV_CH, GATE_COL, SLAB)) + [GATE_COL] if s not in early]
    for i, s in enumerate(rest):
        if i == len(rest) - LATE_SLABS:
            before_late()
        project(s)

    for s in range(0, GDN_CONV_CH, SLAB):
        sl = slice(s, s + SLAB)
        for bi in range(bb):
            full = win[bi, :, sl]
            prev = pltpu.roll(full, 1, 0)
            pair = cw_ref[1:2, sl] * full + cw_ref[0:1, sl] * prev
            y = (cw_ref[3:4, sl] * full + cw_ref[2:3, sl] * prev + pltpu.roll(pair, 2, 0))[SUBLANES:]
            proj_ref[bi, :, sl] = y * _sigmoid(y)
            nc_ref[bi, :, sl] = full[n_valid:n_valid + SUBLANES]
            win[bi, 0:SUBLANES, sl] = full[T:T + SUBLANES]


def _consts():
    ltri = np.tril(np.ones((CHUNK, CHUNK), np.float32))
    bd = np.kron(np.eye(GDN_HALF // GDN_DK, dtype=np.float32), np.ones((GDN_DK, GDN_DK), np.float32))
    return (jnp.asarray(ltri, BF16), jnp.asarray(bd, BF16))


def _gdn_gates(g_ref, alog_ref, dtb_ref, *, n_chunks, n_valid):
    c = CHUNK
    bb = g_ref.shape[0]
    grow_ = lax.broadcasted_iota(jnp.int32, (c, LANES), 0)
    glane = lax.broadcasted_iota(jnp.int32, (c, LANES), 1)
    neg_a = -jnp.exp(alog_ref[...]) * LOG2E
    gbs = {}
    for bi in range(bb):
        for ci in range(n_chunks):
            gz = g_ref[bi, ci * c:(ci + 1) * c, :]
            xg = gz + dtb_ref[...]
            softplus = jnp.maximum(xg, 0.0) + jnp.log(1.0 + jnp.exp(-jnp.abs(xg)))
            gb = jnp.where(glane < GDN_HEADS, _sigmoid(gz),
                           jnp.where(glane < 2 * GDN_HEADS, neg_a * softplus, 0.0))
            if n_valid is not None:
                gb = jnp.where(grow_ + ci * c < n_valid, gb, 0.0)
            gbs[(bi, ci)] = gb
    return gbs


def _gdn_gate_lanes(gbs, ltri):
    c = CHUNK
    lane_w = lax.broadcasted_iota(jnp.int32, (c, LANES), 1)

    def to_head_lanes(x, first):
        slabs = []
        for p in range(GDN_QK // LANES):
            src = first + 2 * p + (lane_w >= GDN_DK).astype(jnp.int32)
            slabs.append(jnp.take_along_axis(x, src, axis=1))
        return jnp.concatenate(slabs, axis=1)

    gcums = {g: _sel_left(ltri, gb) for g, gb in gbs.items()}
    row = lax.broadcasted_iota(jnp.int32, (c, GDN_QK), 0)
    col = lax.broadcasted_iota(jnp.int32, (c, GDN_QK), 1) & (GDN_DK - 1)
    out = {}
    for g, gb in gbs.items():
        gw = to_head_lanes(gcums[g], GDN_HEADS)
        glast = gw[c - 1:c, :]
        gdiag = jnp.sum(jnp.where(col == row, gw, 0.0), axis=0, keepdims=True)
        out[g] = dict(
            beta=to_head_lanes(gb, 0),
            e_g=jnp.exp2(gw),
            dec=jnp.where(col <= row, jnp.exp2(jnp.minimum(gw - gdiag, 0.0)), 0.0),
            e_rest=jnp.exp2(glast - gw),
            e_last=jnp.exp2(glast))
    return out


def _gdn_body(q_ref, k_ref, v_ref, z_ref, gate_lanes, s0_ref, nw_ref, bd_ref,
              o_ref, sn_ref, s_ref, *, n_chunks):
    has_state = s0_ref is not None
    c = CHUNK
    W = GDN_HALF
    bb = q_ref.shape[0]
    t = pl.program_id(1)
    bd_mask = ((lax.broadcasted_iota(jnp.int32, (W, W), 0) & -GDN_DK)
               == (lax.broadcasted_iota(jnp.int32, (W, W), 1) & -GDN_DK))

    @pl.when(t == 0)
    def _():
        if has_state:
            for bi in range(bb):
                for a in range(GDN_HALVES):
                    wide = jnp.concatenate([s0_ref[bi, a]] * (W // GDN_DK), axis=1)
                    s_ref[bi, a] = jnp.where(bd_mask, wide, 0.0).T
        else:
            s_ref[...] = jnp.zeros(s_ref.shape, F32)

    row = lax.broadcasted_iota(jnp.int32, (c, W), 0)
    col = lax.broadcasted_iota(jnp.int32, (c, W), 1) & (GDN_DK - 1)
    strict_t = col < row
    eye_f = (col == row).astype(F32)
    bd01 = bd_ref[...]

    def block_diag(x_b):
        return jnp.concatenate([x_b] * (W // c), axis=0) * bd01

    groups = [(bi, ci) for bi in range(bb) for ci in range(n_chunks)]
    items = [(bi, ci, a) for (bi, ci) in groups for a in range(GDN_HALVES)]
    rows = lambda ci: slice(ci * c, (ci + 1) * c)
    lanes = lambda a: slice(a * W, (a + 1) * W)

    st = {}
    for (bi, ci, a) in items:
        qp = q_ref[bi, rows(ci), lanes(a)]
        kp = k_ref[bi, rows(ci), lanes(a)]
        ss = _dot(jnp.concatenate([qp * qp, kp * kp], axis=0).astype(BF16), bd01)
        st[(bi, ci, a)] = dict(qp=qp, kp=kp, ss=ss)
    for (bi, ci, a) in items:
        d = st[(bi, ci, a)]
        qp, kp, ss = d["qp"], d["kp"], d["ss"]
        vp = v_ref[bi, rows(ci), lanes(a)]
        qn = qp * lax.rsqrt(ss[:c] + EPS) * (GDN_DK ** -0.5)
        kn = kp * lax.rsqrt(ss[c:] + EPS)
        gl = gate_lanes[(bi, ci)]
        bp, e_g, dec = gl["beta"][:, lanes(a)], gl["e_g"][:, lanes(a)], gl["dec"][:, lanes(a)]
        kb = kn * bp
        kq = _dot_nt(jnp.concatenate([kb, qn], axis=0).astype(BF16), block_diag(kn.astype(BF16)))
        a_neg = jnp.where(strict_t, -(kq[:c] * dec), 0.0)
        st[(bi, ci, a)] = dict(
            a_neg=a_neg, attn=(kq[c:] * dec).astype(BF16), vb=(vp * bp).astype(BF16),
            kbg=(kb * e_g).astype(BF16), qg=qn * e_g, kdec=(kn * gl["e_rest"][:, lanes(a)]).astype(BF16),
            eglast=gl["e_last"][:, lanes(a)])

    for it in items:
        a_b = st[it]["a_neg"].astype(BF16)
        st[it]["pw"] = _dot(a_b, block_diag(a_b))
        st[it]["xk"] = eye_f + st[it]["a_neg"]
    span = 4
    while span < c:
        for it in items:
            pw, xk = st[it]["pw"], st[it]["xk"]
            both = _dot(jnp.concatenate([pw, xk], axis=0).astype(BF16), block_diag(pw.astype(BF16)))
            st[it]["pw"] = both[:c]
            st[it]["xk"] = xk + both[c:]
        span *= 2
    for it in items:
        pw, xk = st[it]["pw"], st[it]["xk"]
        st[it]["t_b"] = (xk + _dot(xk.astype(BF16), block_diag(pw.astype(BF16)))).astype(BF16)
    for it in items:
        d = st[it]
        d["u_b"] = _dot(d["t_b"], block_diag(d["vb"])).astype(BF16)
        d["w_b"] = _dot(d["t_b"], block_diag(d["kbg"])).astype(BF16)

    for it in items:
        d = st[it]
        d["q_eff"] = (d["qg"] - _dot(d["attn"], block_diag(d["w_b"]))).astype(BF16)
        d["o_loc"] = _dot(d["attn"], block_diag(d["u_b"]))
        d["n_t"] = jnp.where(bd_mask, _dot_tn(d["u_b"], d["kdec"]), 0.0)
        d["m_t"] = jnp.where(bd_mask, -_dot_tn(d["w_b"], d["kdec"]), 0.0).astype(BF16)

    for ci in range(n_chunks):
        for bi in range(bb):
            for a in range(GDN_HALVES):
                d = st[(bi, ci, a)]
                s_old = s_ref[bi, a]
                s_b = s_old.astype(BF16)
                d["o"] = _dot_nt(d["q_eff"], s_b) + d["o_loc"]
                s_ref[bi, a] = s_old * d["eglast"] + d["n_t"] + _dot(s_b, d["m_t"])

    for it in items:
        st[it]["ms"] = _dot((st[it]["o"] * st[it]["o"]).astype(BF16), bd01) * (1.0 / GDN_DK)
    for (bi, ci, a) in items:
        d = st[(bi, ci, a)]
        zs = z_ref[bi, rows(ci), lanes(a)]
        o_ref[bi, rows(ci), lanes(a)] = (d["o"] * lax.rsqrt(d["ms"] + EPS) * nw_ref[...] * zs).astype(o_ref.dtype)

    @pl.when(t == pl.num_programs(1) - 1)
    def _():
        for bi in range(bb):
            for a in range(GDN_HALVES):
                s_bd = s_ref[bi, a].T
                f = s_bd[:, :LANES] + s_bd[:, LANES:]
                sn_ref[bi, a] = f[:, :GDN_DK] + f[:, GDN_DK:]


def _hgrn_levels():
    out = []
    m = 1
    while m < CHUNK:
        out.append(m)
        m *= 2
    return out


def _level_ref_rows(b, m):
    c = b.shape[0]
    bcast = lambda r, n: jnp.broadcast_to(b[r:r + 1, :], (n, LANES))
    if 2 * m >= SUBLANES:
        parts = [bcast(blk * 2 * m + m, 2 * m) for blk in range(c // (2 * m))]
    else:
        sub = lax.broadcasted_iota(jnp.int32, (SUBLANES, LANES), 0)
        parts = []
        for g in range(c // SUBLANES):
            acc = bcast(g * SUBLANES + m, SUBLANES)
            for blk in range(1, SUBLANES // (2 * m)):
                acc = jnp.where(sub >= blk * 2 * m, bcast(g * SUBLANES + blk * 2 * m + m, SUBLANES), acc)
            parts.append(acc)
    return parts[0] if len(parts) == 1 else jnp.concatenate(parts, axis=0)


def _hgrn_gates(f_ref, lb_ref, *, n_chunks, n_valid):
    c = CHUNK
    bb = f_ref.shape[0]
    lb = lb_ref[...]
    kks, logfs = {}, {}
    for bi in range(bb):
        for ci in range(n_chunks):
            f = lb + (1.0 - lb) * _sigmoid(f_ref[bi, ci * c:(ci + 1) * c, :])
            logf = jnp.log2(f)
            kk = 1.0 - f
            if n_valid is not None:
                valid = lax.broadcasted_iota(jnp.int32, (c, HG_QK), 0) + ci * c < n_valid
                logf = jnp.where(valid, logf, 0.0)
                kk = jnp.where(valid, kk, 0.0)
            kks[(bi, ci)] = kk
            logfs[(bi, ci)] = logf
    return kks, logfs


def _hgrn_body(q_ref, gates, i_ref, z_ref, s0_ref, nw_ref, o_ref, sn_ref, s_ref,
               *, n_chunks, n_valid):
    has_state = s0_ref is not None
    c = CHUNK
    bb = q_ref.shape[0]
    t = pl.program_id(1)

    @pl.when(t == 0)
    def _():
        if has_state:
            for bi in range(bb):
                for h in range(HG_HEADS):
                    s_ref[bi, h] = s0_ref[bi, h].T
        else:
            s_ref[...] = jnp.zeros(s_ref.shape, F32)

    row = lax.broadcasted_iota(jnp.int32, (c, LANES), 0)
    ri = lax.broadcasted_iota(jnp.int32, (c, c), 0)
    ci_ = lax.broadcasted_iota(jnp.int32, (c, c), 1)
    levels = _hgrn_levels()
    eye_f = (ri == ci_).astype(F32)
    upper = [(row & m) != 0 for m in levels]
    sign_f = [jnp.where(u, 1.0, -1.0) for u in upper]
    keep = [((ri & -(2 * m)) == (ci_ & -(2 * m))) & ((ri & m) != 0) & ((ci_ & m) == 0) for m in levels]

    groups = [(bi, ci) for bi in range(bb) for ci in range(n_chunks)]
    items = [(bi, ci, h) for (bi, ci) in groups for h in range(HG_HEADS)]
    rows = lambda ci: slice(ci * c, (ci + 1) * c)
    lanes = lambda h: slice(h * LANES, (h + 1) * LANES)

    kks, bs = gates

    st = {}
    for (bi, ci, h) in items:
        q = q_ref[bi, rows(ci), lanes(h)]
        k = kks[(bi, ci)][:, lanes(h)]
        v = i_ref[bi, rows(ci), lanes(h)]
        if n_valid is not None:
            v = jnp.where(row + ci * c < n_valid, v, 0.0)
        b = bs[(bi, ci)][:, lanes(h)]
        attn = eye_f * _dot_nt(q.astype(BF16), k.astype(BF16))
        for li, m in enumerate(levels):
            if m == 1:
                z_b = jnp.where(upper[li], q, k * pltpu.roll(1.0 - k, c - 1, 0)).astype(BF16)
            else:
                e = jnp.exp2((b - _level_ref_rows(b, m)) * sign_f[li])
                z_b = (jnp.where(upper[li], q, k) * e).astype(BF16)
            attn = jnp.where(keep[li], _dot_nt(z_b, z_b), attn)
        blast = b[c - 1:c, :]
        st[(bi, ci, h)] = dict(
            qe=(q * jnp.exp2(b)).astype(BF16), attn=attn.astype(BF16), v_b=v.astype(BF16),
            kdec=(k * jnp.exp2(blast - b)).astype(BF16), eblast=jnp.exp2(blast))
    for d in st.values():
        d["av"] = _dot(d["attn"], d["v_b"])
        d["upd"] = _dot_tn(d["v_b"], d["kdec"])

    for ci in range(n_chunks):
        for bi in range(bb):
            for h in range(HG_HEADS):
                d = st[(bi, ci, h)]
                s_old = s_ref[bi, h]
                o = _dot_nt(d["qe"], s_old.astype(BF16)) + d["av"]
                s_ref[bi, h] = s_old * d["eblast"] + d["upd"]

                ms = jnp.mean(o * o, axis=-1, keepdims=True)
                zs = z_ref[bi, rows(ci), lanes(h)]
                o_ref[bi, rows(ci), lanes(h)] = (o * lax.rsqrt(ms + EPS) * nw_ref[...] * zs).astype(o_ref.dtype)

    @pl.when(t == pl.num_programs(1) - 1)
    def _():
        for bi in range(bb):
            for h in range(HG_HEADS):
                sn_ref[bi, h] = s_ref[bi, h].T


def _outproj_body(x_ref, oa_ref, ob_ref, w_ref, fw_ref, y_ref):
    bb, T = x_ref.shape[0], x_ref.shape[1]
    rows = min(T, OUT_ROWS)
    for bi in range(bb):
        for r in range(0, T, rows):
            o = jnp.concatenate([oa_ref[bi, r:r + rows, :], ob_ref[bi, r:r + rows, :]], axis=1)
            y = x_ref[bi, r:r + rows, :] + _dot(o, w_ref[...])
            y_ref[bi, r:r + rows, :] = (y * lax.rsqrt(jnp.mean(y * y, axis=-1, keepdims=True) + EPS)
                                        * fw_ref[...])


def _layer_kernel(*refs, n_chunks, n_valid, has_state):
    refs = list(refs)
    x_ref = refs.pop(0)
    cs_ref = sg0_ref = sh0_ref = None
    if has_state:
        cs_ref, sg0_ref, sh0_ref = refs[:3]
        del refs[:3]
    (nw_ref, w_in_ref, cw_ref, alog_ref, dtb_ref, gnw_ref, lb_ref, hnw_ref, w_out_ref, fw_ref, ltri_ref, bd_ref,
     y_ref, nc_ref, sgn_ref, shn_ref,
     proj, win, sg_ref, sh_ref, oa_ref, ob_ref) = refs
    T = x_ref.shape[1]
    col = lambda j: proj.at[:, :, j * GDN_QK:(j + 1) * GDN_QK]
    gate_vals = {}

    def gate_math():
        gate_vals["gdn"] = _gdn_gates(proj.at[:, :, GATE_COL:N_PROJ], alog_ref, dtb_ref,
                                      n_chunks=n_chunks, n_valid=n_valid)
        gate_vals["hgrn"] = _hgrn_gates(col(5), lb_ref, n_chunks=n_chunks, n_valid=n_valid)

    def gate_sums():
        ltri = ltri_ref[...]
        gate_vals["gdn"] = _gdn_gate_lanes(gate_vals["gdn"], ltri)
        kks, logfs = gate_vals["hgrn"]
        gate_vals["hgrn"] = (kks, {g: _sel_left(ltri, lf) for g, lf in logfs.items()})

    _inproj_body(x_ref, nw_ref, w_in_ref, cw_ref, cs_ref, proj, nc_ref, win,
                 n_valid=T if n_valid is None else n_valid,
                 early=(GATE_COL, 5 * HG_QK), after_early=gate_math, before_late=gate_sums)
    _gdn_body(col(0), col(1), col(2), col(3), gate_vals["gdn"], sg0_ref, gnw_ref, bd_ref,
              oa_ref, sgn_ref, sg_ref, n_chunks=n_chunks)
    _hgrn_body(col(4), gate_vals["hgrn"], col(6), col(7), sh0_ref, hnw_ref,
               ob_ref, shn_ref, sh_ref, n_chunks=n_chunks, n_valid=n_valid)
    _outproj_body(x_ref, oa_ref, ob_ref, w_out_ref, fw_ref, y_ref)


def _layer(x, conv_state, s_gdn, s_hg, params, consts, *, n_valid, bb, n_chunks):
    B, L, _ = x.shape
    tile = n_chunks * CHUNK
    has_state = conv_state is not None
    kern = functools.partial(_layer_kernel, n_chunks=n_chunks, n_valid=n_valid, has_state=has_state)
    conv_state_spec = _batch_spec((bb, SUBLANES, GDN_CONV_CH), lambda b, t: (0, 0))
    gdn_state_shape = (B, GDN_HALVES, GDN_HALF, GDN_DK)
    gdn_state = _batch_spec((bb,) + gdn_state_shape[1:], lambda b, t: (0, 0, 0))
    hg_state = _batch_spec((bb, HG_HEADS, HG_DK, HG_DK), lambda b, t: (0, 0, 0))
    state_args, state_specs = [], []
    if has_state:
        cs8 = jnp.pad(conv_state, ((0, 0), (SUBLANES - (CONV_W - 1), 0), (0, 0)))
        state_args = [cs8, s_gdn.reshape(gdn_state_shape), s_hg]
        state_specs = [conv_state_spec, gdn_state, hg_state]
    const_args = list(params) + list(consts)
    const_specs = [pl.BlockSpec(a.shape, lambda b, t, n=a.ndim: (0,) * n, pipeline_mode=pl.Buffered(1))
                   for a in const_args]
    y, nc8, sgn, shn = pl.pallas_call(
        kern,
        grid=(B // bb, L // tile),
        in_specs=[_batch_spec((bb, tile, D_MODEL), lambda b, t: (t, 0))] + state_specs + const_specs,
        out_specs=[_batch_spec((bb, tile, D_MODEL), lambda b, t: (t, 0)), conv_state_spec, gdn_state, hg_state],
        out_shape=[
            jax.ShapeDtypeStruct((B, L, D_MODEL), F32),
            jax.ShapeDtypeStruct((B, SUBLANES, GDN_CONV_CH), F32),
            jax.ShapeDtypeStruct(gdn_state_shape, F32),
            jax.ShapeDtypeStruct((B, HG_HEADS, HG_DK, HG_DK), F32),
        ],
        scratch_shapes=[
            pltpu.VMEM((bb, tile, N_PROJ), F32),
            pltpu.VMEM((bb, SUBLANES + tile, GDN_CONV_CH), F32),
            pltpu.VMEM((bb, GDN_HALVES, GDN_HALF, GDN_HALF), F32),
            pltpu.VMEM((bb, HG_HEADS, HG_DK, HG_DK), F32),
            pltpu.VMEM((bb, tile, GDN_QK), BF16),
            pltpu.VMEM((bb, tile, HG_QK), BF16),
        ],
        compiler_params=pltpu.CompilerParams(
            dimension_semantics=("arbitrary", "arbitrary"), vmem_limit_bytes=VMEM_LIMIT),
        name="hybrid_layer",
    )(x, *state_args, *const_args)
    new_conv = nc8[:, SUBLANES - (CONV_W - 1):, :]
    return y, new_conv, sgn.reshape(B, GDN_HEADS, GDN_DK, GDN_DK), shn


def _prep(norm_w, w_in, conv_w, gdn_A_log, gdn_dt_bias, gdn_norm_w, hgrn_lb_logits, hgrn_norm_w, w_out,
          final_norm_w):
    w = w_in[0]
    o_qkv, o_za = 0, GDN_CONV_CH
    o_b = o_za + GDN_QK
    o_a = o_b + GDN_HEADS
    o_hq = o_a + GDN_HEADS
    gate_cols = jnp.pad(w[:, o_b:o_hq], ((0, 0), (0, LANES - 2 * GDN_HEADS)))
    w_in_r = jnp.concatenate([w[:, o_qkv:o_b].astype(BF16), w[:, o_hq:].astype(BF16), gate_cols.astype(BF16)],
                             axis=1)

    pad_row = lambda v: jnp.pad(v.astype(F32), (GDN_HEADS, LANES - 2 * GDN_HEADS))[None, :]
    lb = jnp.cumsum(jax.nn.softmax(hgrn_lb_logits.astype(F32), axis=0), axis=0)[0]
    params = (norm_w[0][None, :], w_in_r, conv_w[0], pad_row(gdn_A_log[0]), pad_row(gdn_dt_bias[0]),
              jnp.tile(gdn_norm_w[0], GDN_HALF // GDN_DK)[None, :], lb[None, :], hgrn_norm_w[0][None, :],
              w_out[0].astype(BF16), final_norm_w[None, :])
    return params, _consts()


def kernel(x_prompt, x_sample, state_conv, state_gdn, state_hgrn, norm_w, w_in, conv_w, gdn_A_log,
           gdn_dt_bias, gdn_norm_w, hgrn_lb_logits, hgrn_norm_w, w_out, final_norm_w):
    Bs, Ls, _ = x_sample.shape
    params, consts = _prep(norm_w, w_in, conv_w, gdn_A_log, gdn_dt_bias, gdn_norm_w, hgrn_lb_logits,
                           hgrn_norm_w, w_out, final_norm_w)

    y_p, c_p, g_p, r_p = _layer(x_prompt, None, None, None, params, consts,
                                n_valid=None, bb=1, n_chunks=8)

    xs = jnp.pad(x_sample, ((0, 0), (0, CHUNK - Ls), (0, 0)))
    y_s, c_s, g_s, r_s = _layer(xs, state_conv[0], state_gdn[0], state_hgrn[0], params, consts,
                                n_valid=Ls, bb=4, n_chunks=1)
    y_s = y_s[:, :Ls]

    return (y_p, y_s, c_p[None], g_p[None], r_p[None], c_s[None], g_s[None], r_s[None])
```
